```python
import jax, jax.numpy as jnp
from jax import lax
import numpy as np

D_MODEL = 1024
BATCH = 8
SEQ = 2048
DEPTH = 2
DEC_BATCH = 128
DEC_SEQ = 1
PAST_LEN = 16384
PAGE_SIZE = 128

N_MIXERS = 2
N_MLSTM_LAYERS = (DEPTH + 1) // 2
N_CONV_LAYERS = DEPTH // 2
INNER = 2 * D_MODEL
MLSTM_HEADS = 4
MLSTM_DH = INNER // MLSTM_HEADS
MLSTM_CONV = 4
CHUNK = 64
CONV_WIDTH = 31
EPS = 1e-6
NEG = -1e30

kernel_name = "hybrid_mlstm_conformer_adaln_step"


def rmsnorm(x, g):
    xf = x.astype(jnp.float32)
    y = xf * lax.rsqrt(jnp.mean(xf * xf, axis=-1, keepdims=True) + EPS)
    return (y * g).astype(x.dtype)


def layernorm(x, g, b=None):
    xf = x.astype(jnp.float32)
    mu = jnp.mean(xf, axis=-1, keepdims=True)
    var = jnp.mean(jnp.square(xf - mu), axis=-1, keepdims=True)
    y = (xf - mu) * lax.rsqrt(var + EPS) * g
    if b is not None:
        y = y + b
    return y.astype(x.dtype)


def adaln(c, w, b):
    mod = jax.nn.silu(c) @ w + b
    shift, scale, gate = jnp.split(mod, 3, axis=-1)
    return shift[:, None, :], scale[:, None, :], gate[:, None, :]


def causal_dwconv(x_full, w):
    return lax.conv_general_dilated(
        x_full, w[:, None, :].astype(x_full.dtype), (1,), 'VALID',
        dimension_numbers=('NWC', 'WIO', 'NWC'), feature_group_count=x_full.shape[-1])


def mlstm_chunk(C, n, m, q, k, v, ig, lf):
    L = q.shape[2]
    b = jnp.cumsum(lf, axis=-1)
    causal = jnp.tril(jnp.ones((L, L), dtype=bool))
    dmat = jnp.where(causal, b[..., :, None] - b[..., None, :] + ig[..., None, :], NEG)
    inter = b + m[..., None]
    m_t = jnp.maximum(inter, jnp.max(dmat, axis=-1))
    w_intra = jnp.exp(dmat - m_t[..., None])
    w_inter = jnp.exp(inter - m_t)
    s = jnp.einsum('bhtd,bhsd->bhts', q, k) * w_intra
    num = jnp.einsum('bhts,bhsd->bhtd', s, v) + w_inter[..., None] * jnp.einsum('bhtk,bhkv->bhtv', q, C)
    den = jnp.sum(s, axis=-1) + w_inter * jnp.einsum('bhtk,bhk->bht', q, n)
    h = num / jnp.maximum(jnp.abs(den), jnp.exp(-m_t))[..., None]
    m_new = m_t[..., -1]
    b_last = b[..., -1]
    decay = jnp.exp(b_last + m - m_new)
    wk = jnp.exp(b_last[..., None] - b + ig - m_new[..., None])
    C_new = decay[..., None, None] * C + jnp.einsum('bhs,bhsk,bhsv->bhkv', wk, k, v)
    n_new = decay[..., None] * n + jnp.einsum('bhs,bhsk->bhk', wk, k)
    return h, C_new, n_new, m_new


def mlstm_cell(q, k, v, ig, lf, C, n, m):
    B, H, T, DH = q.shape
    L = CHUNK if T % CHUNK == 0 else T
    NC = T // L

    def to_chunks(a):
        return jnp.moveaxis(a.reshape((B, H, NC, L) + a.shape[3:]), 2, 0)

    def step(carry, xs):
        Cc, nc, mc = carry
        h, Cc, nc, mc = mlstm_chunk(Cc, nc, mc, *xs)
        return (Cc, nc, mc), h

    (C, n, m), hs = lax.scan(step, (C, n, m), tuple(to_chunks(a) for a in (q, k, v, ig, lf)))
    h = jnp.moveaxis(hs, 0, 2).reshape(B, H, T, DH)
    return h, C, n, m


def mlstm_branch(h, conv_buf, C, n, m, w_in, w_conv, b_conv, w_q, w_k, w_v,
                 w_ig, b_ig, w_fg, b_fg, ln_g, skip, w_out):
    B, T, _ = h.shape
    xm, z = jnp.split(h @ w_in, 2, axis=-1)
    xfull = jnp.concatenate([conv_buf.astype(xm.dtype), xm], axis=1)
    xc = jax.nn.silu(causal_dwconv(xfull, w_conv) + b_conv)
    xch = xc.reshape(B, T, MLSTM_HEADS, MLSTM_DH)
    q = jnp.einsum('bthd,hde->bthe', xch, w_q)
    k = jnp.einsum('bthd,hde->bthe', xch, w_k)
    v = jnp.einsum('bthd,hde->bthe', xm.reshape(B, T, MLSTM_HEADS, MLSTM_DH), w_v)
    qkv = jnp.concatenate([q.reshape(B, T, INNER), k.reshape(B, T, INNER), v.reshape(B, T, INNER)], axis=-1)
    ig = qkv @ w_ig + b_ig
    lf = jax.nn.log_sigmoid((qkv @ w_fg + b_fg).astype(jnp.float32))

    def tr(a):
        return jnp.moveaxis(a.astype(jnp.float32), 1, 2)

    hh, C, n, m = mlstm_cell(tr(q), tr(k) * (MLSTM_DH ** -0.5), tr(v), tr(ig), tr(lf),
                             C.astype(jnp.float32), n.astype(jnp.float32), m.astype(jnp.float32))
    hh = layernorm(hh, ln_g.astype(jnp.float32)[:, None, :])
    hh = jnp.moveaxis(hh, 1, 2).reshape(B, T, INNER).astype(xm.dtype)
    out = ((hh + skip * xc) * jax.nn.silu(z)) @ w_out
    return out, C, n, m, xfull[:, -(MLSTM_CONV - 1):]


def conformer_branch(h, conv_buf, w_in, b_in, w_dw, b_dw, ln_g, ln_b, w_out):
    a, g, z = jnp.split(h @ w_in + b_in, 3, axis=-1)
    u = a * jax.nn.sigmoid(g)
    ufull = jnp.concatenate([conv_buf.astype(u.dtype), u], axis=1)
    y = causal_dwconv(ufull, w_dw) + b_dw
    y = layernorm(y, ln_g, ln_b)
    out = (jax.nn.silu(y) * jax.nn.silu(z)) @ w_out
    return out, ufull[:, -(CONV_WIDTH - 1):]


def setup_inputs(seed: int = 0) -> dict:
    key = jax.random.key(seed)
    keys = jax.random.split(key, 48)
    cnt = [0]

    def nk():
        cnt[0] += 1
        return keys[cnt[0] - 1]

    def nrm(shape, scale):
        return jax.random.normal(nk(), shape, jnp.float32) * scale

    NA, NB, E, H, DH, D = N_MLSTM_LAYERS, N_CONV_LAYERS, INNER, MLSTM_HEADS, MLSTM_DH, D_MODEL
    inp = {}
    inp['x_prompt'] = nrm((BATCH, SEQ, D), 1.0)
    inp['x_sample'] = nrm((DEC_BATCH, DEC_SEQ, D), 1.0)
    inp['c_prompt'] = nrm((BATCH, D), 1.0)
    inp['c_sample'] = nrm((DEC_BATCH, D), 1.0)
    inp['state_mlstm_C'] = nrm((NA, DEC_BATCH, H, DH, DH), 0.05)
    inp['state_mlstm_n'] = nrm((NA, DEC_BATCH, H, DH), 0.1)
    inp['state_mlstm_m'] = jax.random.uniform(nk(), (NA, DEC_BATCH, H), jnp.float32, 0.0, 2.0)
    inp['state_mlstm_conv'] = nrm((NA, DEC_BATCH, MLSTM_CONV - 1, E), 1.0)
    inp['state_conf_conv'] = nrm((NB, DEC_BATCH, CONV_WIDTH - 1, E), 0.5)
    inp['norm_g'] = 1.0 + nrm((DEPTH, D), 0.02)
    inp['w_ada'] = nrm((DEPTH, D, 3 * D), 0.5 * D ** -0.5)
    inp['b_ada'] = nrm((DEPTH, 3 * D), 0.02)
    inp['ml_w_in'] = nrm((NA, D, 2 * E), D ** -0.5)
    inp['ml_w_conv'] = nrm((NA, MLSTM_CONV, E), MLSTM_CONV ** -0.5)
    inp['ml_b_conv'] = nrm((NA, E), 0.02)
    inp['ml_w_q'] = nrm((NA, H, DH, DH), DH ** -0.5)
    inp['ml_w_k'] = nrm((NA, H, DH, DH), DH ** -0.5)
    inp['ml_w_v'] = nrm((NA, H, DH, DH), DH ** -0.5)
    inp['ml_w_ig'] = nrm((NA, 3 * E, H), (3 * E) ** -0.5)
    inp['ml_b_ig'] = nrm((NA, H), 0.1)
    inp['ml_w_fg'] = nrm((NA, 3 * E, H), (3 * E) ** -0.5)
    inp['ml_b_fg'] = jnp.broadcast_to(jnp.linspace(3.0, 6.0, H, dtype=jnp.float32), (NA, H)) + nrm((NA, H), 0.1)
    inp['ml_ln_g'] = 1.0 + nrm((NA, H, DH), 0.02)
    inp['ml_skip'] = 1.0 + nrm((NA, E), 0.02)
    inp['ml_w_out'] = nrm((NA, E, D), E ** -0.5)
    inp['cf_w_in'] = nrm((NB, D, 3 * E), D ** -0.5)
    inp['cf_b_in'] = nrm((NB, 3 * E), 0.02)
    inp['cf_w_dw'] = nrm((NB, CONV_WIDTH, E), CONV_WIDTH ** -0.5)
    inp['cf_b_dw'] = nrm((NB, E), 0.02)
    inp['cf_ln_g'] = 1.0 + nrm((NB, E), 0.02)
    inp['cf_ln_b'] = nrm((NB, E), 0.02)
    inp['cf_w_out'] = nrm((NB, E, D), E ** -0.5)
    inp['final_g'] = 1.0 + nrm((D,), 0.02)
    return inp


def reference(x_prompt, x_sample, c_prompt, c_sample, state_mlstm_C, state_mlstm_n, state_mlstm_m,
              state_mlstm_conv, state_conf_conv, norm_g, w_ada, b_ada, ml_w_in, ml_w_conv, ml_b_conv,
              ml_w_q, ml_w_k, ml_w_v, ml_w_ig, ml_b_ig, ml_w_fg, ml_b_fg, ml_ln_g, ml_skip, ml_w_out,
              cf_w_in, cf_b_in, cf_w_dw, cf_b_dw, cf_ln_g, cf_ln_b, cf_w_out, final_g):

    def run(x, c, mC, mn, mm, mconv, cconv):
        out_C, out_n, out_m, out_mconv, out_cconv = [], [], [], [], []
        for i in range(DEPTH):
            shift, scale, gate = adaln(c, w_ada[i], b_ada[i])
            h = rmsnorm(x, norm_g[i]) * (1.0 + scale) + shift
            j = i // N_MIXERS
            if i % N_MIXERS == 0:
                out, C, n, m, buf = mlstm_branch(
                    h, mconv[j], mC[j], mn[j], mm[j], ml_w_in[j], ml_w_conv[j], ml_b_conv[j],
                    ml_w_q[j], ml_w_k[j], ml_w_v[j], ml_w_ig[j], ml_b_ig[j], ml_w_fg[j], ml_b_fg[j],
                    ml_ln_g[j], ml_skip[j], ml_w_out[j])
                out_C.append(C); out_n.append(n); out_m.append(m); out_mconv.append(buf)
            else:
                out, buf = conformer_branch(h, cconv[j], cf_w_in[j], cf_b_in[j], cf_w_dw[j],
                                            cf_b_dw[j], cf_ln_g[j], cf_ln_b[j], cf_w_out[j])
                out_cconv.append(buf)
            x = x + gate * out
        y = rmsnorm(x, final_g)
        return (y, jnp.stack(out_C), jnp.stack(out_n), jnp.stack(out_m),
                jnp.stack(out_mconv), jnp.stack(out_cconv))

    B = x_prompt.shape[0]
    f32 = jnp.float32
    p_C0 = jnp.zeros((N_MLSTM_LAYERS, B, MLSTM_HEADS, MLSTM_DH, MLSTM_DH), f32)
    p_n0 = jnp.zeros((N_MLSTM_LAYERS, B, MLSTM_HEADS, MLSTM_DH), f32)
    p_m0 = jnp.zeros((N_MLSTM_LAYERS, B, MLSTM_HEADS), f32)
    p_mc0 = jnp.zeros((N_MLSTM_LAYERS, B, MLSTM_CONV - 1, INNER), x_prompt.dtype)
    p_cc0 = jnp.zeros((N_CONV_LAYERS, B, CONV_WIDTH - 1, INNER), x_prompt.dtype)

    y_prompt, C_p, n_p, m_p, mconv_p, cconv_p = run(x_prompt, c_prompt, p_C0, p_n0, p_m0, p_mc0, p_cc0)
    y_sample, C_s, n_s, m_s, mconv_s, cconv_s = run(x_sample, c_sample, state_mlstm_C, state_mlstm_n,
                                                    state_mlstm_m, state_mlstm_conv, state_conf_conv)
    return (y_prompt, y_sample, C_p, C_s, n_p, n_s, m_p, m_s, mconv_p, mconv_s, cconv_p, cconv_s)
```

```python
import functools

import jax
import jax.numpy as jnp
from jax import lax
from jax.experimental import pallas as pl
from jax.experimental.pallas import tpu as pltpu

D_MODEL = 1024
INNER = 2 * D_MODEL
HEADS = 4
DH = INNER // HEADS
MCONV = 4
CCONV = 31
EPS = 1e-6
NEG = -1e30
K_SCALE = DH ** -0.5

LANES = 128
SUBLANES = 8
VMEM_LIMIT = 56 * 2 ** 20

ROW_TILE = 256
CHUNK = 256
STEP_BATCH = 8
CONV_ROWS = 32
CONV_COLS = 256
CCONV_HIST = 32
SAMPLE_TILE = 16

f32 = jnp.float32
bf16 = jnp.bfloat16


def _bf(x):
    return x.astype(bf16)


def _dot(a, b):
    return jnp.dot(a, b, preferred_element_type=f32)


def _silu(x):
    return x * jax.nn.sigmoid(x)


def _log_sigmoid(x):
    return jnp.minimum(x, 0.0) - jnp.log1p(jnp.exp(-jnp.abs(x)))


def _rms(x, g):
    return x * lax.rsqrt(jnp.mean(x * x, axis=-1, keepdims=True) + EPS) * g


def _rms_mod(x, g, mod):
    return _rms(x, g) * (1.0 + mod[:, D_MODEL:2 * D_MODEL]) + mod[:, :D_MODEL]


def _layernorm(x, g, b=None):
    mu = jnp.mean(x, axis=-1, keepdims=True)
    xc = x - mu
    var = jnp.mean(xc * xc, axis=-1, keepdims=True)
    y = xc * lax.rsqrt(var + EPS) * g
    return y if b is None else y + b


def _const_spec(shape):
    n = len(shape)
    return pl.BlockSpec(shape, lambda *_: (0,) * n, pipeline_mode=pl.Buffered(1))


def _params(*sem):
    return pltpu.CompilerParams(dimension_semantics=sem, vmem_limit_bytes=VMEM_LIMIT)


def _ada_kernel(c_ref, w_ref, b_ref, o_ref):
    o_ref[...] = _dot(_bf(_silu(c_ref[...])), _bf(w_ref[...])) + b_ref[...]


def _ada(c_all, w_ada, b_ada):
    depth, d, d3 = w_ada.shape
    rows = c_all.shape[0]
    tn = D_MODEL
    return pl.pallas_call(
        _ada_kernel,
        grid=(depth, d3 // tn),
        in_specs=[
            pl.BlockSpec((rows, d), lambda i, j: (0, 0)),
            pl.BlockSpec((None, d, tn), lambda i, j: (i, 0, j)),
            pl.BlockSpec((None, 1, tn), lambda i, j: (i, 0, j)),
        ],
        out_specs=pl.BlockSpec((None, rows, tn), lambda i, j: (i, 0, j)),
        out_shape=jax.ShapeDtypeStruct((depth, rows, d3), f32),
        compiler_params=_params("arbitrary", "arbitrary"),
        name="ada",
    )(c_all, w_ada, b_ada.reshape(depth, 1, d3))


def _ml_qkv(xm, xc, wq_ref, wk_ref, wv_ref, wigf_ref, bigf_ref):
    xcb, xmb = _bf(xc), _bf(xm)
    qs, ks, vs = [], [], []
    for h in range(HEADS):
        seg = slice(h * DH, (h + 1) * DH)
        qs.append(_dot(xcb[:, seg], wq_ref[h]))
        ks.append(_dot(xcb[:, seg], wk_ref[h]))
        vs.append(_dot(xmb[:, seg], wv_ref[h]))
    q = jnp.concatenate(qs, axis=1)
    k = jnp.concatenate(ks, axis=1)
    v = jnp.concatenate(vs, axis=1)
    pre = (_dot(_bf(q), wigf_ref[0:INNER, :]) + _dot(_bf(k), wigf_ref[INNER:2 * INNER, :])
           + _dot(_bf(v), wigf_ref[2 * INNER:3 * INNER, :]) + bigf_ref[...])
    lane = lax.broadcasted_iota(jnp.int32, pre.shape, 1)
    gates = jnp.where(lane >= HEADS, _log_sigmoid(pre), pre)
    return q, k, v, gates


def _ml_front_prompt_kernel(x_ref, mod_ref, g_ref, win_ref, wconv_ref, bconv_ref, wq_ref, wk_ref, wv_ref,
                            wigf_ref, bigf_ref,
                            q_ref, k_ref, v_ref, xc_ref, z_ref, igf_ref, mconv_ref, xbuf_ref, *, tm):
    @pl.when(pl.program_id(1) == 0)
    def _():
        xbuf_ref[0:SUBLANES, :] = jnp.zeros((SUBLANES, INNER), f32)

    h = _rms_mod(x_ref[...], g_ref[...], mod_ref[...])
    xmz = _dot(_bf(h), win_ref[...])
    xm = xmz[:, :INNER]
    z_ref[...] = xmz[:, INNER:]
    xbuf_ref[SUBLANES:SUBLANES + tm, :] = xm
    acc = bconv_ref[...] + wconv_ref[MCONV - 1:MCONV, :] * xm
    for j in range(MCONV - 1):
        start = SUBLANES - (MCONV - 1) + j
        acc = acc + wconv_ref[j:j + 1, :] * xbuf_ref[start:start + tm, :]
    xc = _silu(acc)
    xc_ref[...] = xc
    mconv_ref[...] = xbuf_ref[SUBLANES + tm - (MCONV - 1):SUBLANES + tm, :]
    xbuf_ref[0:SUBLANES, :] = xbuf_ref[tm:tm + SUBLANES, :]

    q, k, v, gates = _ml_qkv(xm, xc, wq_ref, wk_ref, wv_ref, wigf_ref, bigf_ref)
    q_ref[...] = _bf(q)
    k_ref[...] = _bf(k * K_SCALE)
    v_ref[...] = _bf(v)
    igf_ref[...] = gates


def _ml_front_prompt(x, mod4, g, w):
    B, T, D = x.shape
    tm = ROW_TILE
    row = lambda b, t: (b, t, 0)
    act = lambda dt: jax.ShapeDtypeStruct((B, T, INNER), dt)
    return pl.pallas_call(
        functools.partial(_ml_front_prompt_kernel, tm=tm),
        grid=(B, T // tm),
        in_specs=[
            pl.BlockSpec((None, tm, D), row),
            pl.BlockSpec((None, None, 1, 3 * D), lambda b, t: (0, b, 0, 0)),
            _const_spec((1, D)),
            _const_spec((D, 2 * INNER)),
            _const_spec((MCONV, INNER)),
            _const_spec((1, INNER)),
            _const_spec((HEADS, DH, DH)),
            _const_spec((HEADS, DH, DH)),
            _const_spec((HEADS, DH, DH)),
            _const_spec((3 * INNER, LANES)),
            _const_spec((1, LANES)),
        ],
        out_specs=[
            pl.BlockSpec((None, tm, INNER), row),
            pl.BlockSpec((None, tm, INNER), row),
            pl.BlockSpec((None, tm, INNER), row),
            pl.BlockSpec((None, tm, INNER), row),
            pl.BlockSpec((None, tm, INNER), row),
            pl.BlockSpec((None, tm, LANES), row),
            pl.BlockSpec((None, MCONV - 1, INNER), lambda b, t: (b, 0, 0)),
        ],
        out_shape=[act(bf16), act(bf16), act(bf16), act(f32), act(f32),
                   jax.ShapeDtypeStruct((B, T, LANES), f32),
                   jax.ShapeDtypeStruct((B, MCONV - 1, INNER), f32)],
        scratch_shapes=[pltpu.VMEM((SUBLANES + tm, INNER), f32)],
        compiler_params=_params("arbitrary", "arbitrary"),
        name="ml_front_prompt",
    )(x, mod4, g, w["ml_w_in"], w["ml_w_conv"], w["ml_b_conv"], w["ml_w_q"], w["ml_w_k"], w["ml_w_v"],
      w["ml_w_igf"], w["ml_b_igf"])


def _ml_front_sample_kernel(x_ref, mod_ref, g_ref, win_ref, wconv_ref, bconv_ref, wq_ref, wk_ref, wv_ref,
                            wigf_ref, bigf_ref, st_ref, n_ref, m_ref,
                            q_ref, kw_ref, v_ref, xc_ref, z_ref, nnew_ref, a1_ref, a2_ref, dec_ref, mnew_ref,
                            mconv_ref):
    h = _rms_mod(x_ref[...], g_ref[...], mod_ref[...])
    xmz = _dot(_bf(h), win_ref[...])
    xm = xmz[:, :INNER]
    z_ref[...] = xmz[:, INNER:]
    acc = bconv_ref[...] + wconv_ref[MCONV - 1:MCONV, :] * xm
    for j in range(MCONV - 1):
        acc = acc + wconv_ref[j:j + 1, :] * st_ref[:, j * INNER:(j + 1) * INNER]
    xc = _silu(acc)
    xc_ref[...] = xc
    mconv_ref[:, 0:(MCONV - 2) * INNER] = st_ref[:, INNER:(MCONV - 1) * INNER]
    mconv_ref[:, (MCONV - 2) * INNER:] = xm

    q, k, v, gates = _ml_qkv(xm, xc, wq_ref, wk_ref, wv_ref, wigf_ref, bigf_ref)
    ks = k * K_SCALE
    q_ref[...] = q
    v_ref[...] = v

    ig = gates
    lf = pltpu.roll(gates, LANES - HEADS, axis=1)
    inter = lf + m_ref[...]
    m_new = jnp.maximum(inter, ig)
    w_intra = jnp.exp(ig - m_new)
    w_inter = jnp.exp(inter - m_new)
    lane = lax.broadcasted_iota(jnp.int32, ig.shape, 1)
    qk = jnp.zeros_like(ig)
    qn = jnp.zeros_like(ig)
    for h in range(HEADS):
        seg = slice(h * DH, (h + 1) * DH)
        qk = jnp.where(lane == h, jnp.sum(q[:, seg] * ks[:, seg], axis=1, keepdims=True), qk)
        qn = jnp.where(lane == h, jnp.sum(q[:, seg] * n_ref[:, seg], axis=1, keepdims=True), qn)
        kw = w_intra[:, h:h + 1] * ks[:, seg]
        kw_ref[:, seg] = kw
        nnew_ref[:, seg] = w_inter[:, h:h + 1] * n_ref[:, seg] + kw
    s = qk * w_intra
    den = s + w_inter * qn
    dn = jnp.maximum(jnp.abs(den), jnp.exp(-m_new))
    valid = lane < HEADS
    a1_ref[...] = jnp.where(valid, s / dn, 0.0)
    a2_ref[...] = jnp.where(valid, w_inter / dn, 0.0)
    dec_ref[...] = jnp.where(valid, w_inter, 0.0)
    mnew_ref[...] = jnp.where(valid, m_new, 0.0)


def _ml_front_sample(x, mod, g, w, st, n, m):
    R, D = x.shape
    full = lambda shape: pl.BlockSpec(shape, lambda i: (0,) * len(shape))
    act = jax.ShapeDtypeStruct((R, INNER), f32)
    small = jax.ShapeDtypeStruct((R, LANES), f32)
    return pl.pallas_call(
        _ml_front_sample_kernel,
        grid=(1,),
        in_specs=[
            full((R, D)), full((R, 3 * D)), full((1, D)), full((D, 2 * INNER)), full((MCONV, INNER)),
            full((1, INNER)), full((HEADS, DH, DH)), full((HEADS, DH, DH)), full((HEADS, DH, DH)),
            full((3 * INNER, LANES)), full((1, LANES)),
            full((R, (MCONV - 1) * INNER)), full((R, INNER)), full((R, LANES)),
        ],
        out_specs=[full((R, INNER))] * 6 + [full((R, LANES))] * 4 + [full((R, (MCONV - 1) * INNER))],
        out_shape=[act] * 6 + [small] * 4 + [jax.ShapeDtypeStruct((R, (MCONV - 1) * INNER), f32)],
        compiler_params=_params("arbitrary"),
        name="ml_front_sample",
    )(x, mod, g, w["ml_w_in"], w["ml_w_conv"], w["ml_b_conv"], w["ml_w_q"], w["ml_w_k"], w["ml_w_v"],
      w["ml_w_igf"], w["ml_b_igf"], st, n, m)


def _mlstm_prompt_kernel(q_ref, k_ref, v_ref, igf_ref, lng_ref, hh_ref, C_ref, n_ref, m_ref, *, L):
    @pl.when(pl.program_id(1) == 0)
    def _():
        C_ref[...] = jnp.zeros_like(C_ref)
        n_ref[...] = jnp.zeros_like(n_ref)
        m_ref[...] = jnp.zeros_like(m_ref)

    igf = igf_ref[...]
    row = lax.broadcasted_iota(jnp.int32, (L, L), 0)
    col = lax.broadcasted_iota(jnp.int32, (L, L), 1)
    causal = row >= col
    tri = jnp.where(causal, 1.0, 0.0).astype(bf16)
    hi = _bf(igf)
    lo = _bf(igf - hi.astype(f32))
    bcum = _dot(tri, hi) + _dot(tri, lo)
    igf_t = igf.T
    bcum_t = bcum.T

    for h in range(HEADS):
        seg = slice(h * DH, (h + 1) * DH)
        ig_col = igf[:, h:h + 1]
        b_col = bcum[:, HEADS + h:HEADS + h + 1]
        ig_row = igf_t[h:h + 1, :]
        b_row = bcum_t[HEADS + h:HEADS + h + 1, :]
        m_prev = m_ref[:, h:h + 1]
        qh, kh, vh = q_ref[:, seg], k_ref[:, seg], v_ref[:, seg]

        dmat = jnp.where(causal, b_col - b_row + ig_row, NEG)
        inter = b_col + m_prev
        m_t = jnp.maximum(inter, jnp.max(dmat, axis=1, keepdims=True))
        w_intra = jnp.exp(dmat - m_t)
        w_inter = jnp.exp(inter - m_t)
        s = lax.dot_general(qh, kh, (((1,), (1,)), ((), ())), preferred_element_type=f32) * w_intra
        num = _dot(_bf(s), vh) + w_inter * _dot(qh, _bf(C_ref[h]))
        qn = jnp.sum(qh.astype(f32) * n_ref[h:h + 1, :], axis=1, keepdims=True)
        den = jnp.sum(s, axis=1, keepdims=True) + w_inter * qn
        hc = num / jnp.maximum(jnp.abs(den), jnp.exp(-m_t))
        hh_ref[:, seg] = _layernorm(hc, lng_ref[:, seg])

        m_new = m_t[L - 1:L, :]
        b_last = b_col[L - 1:L, :]
        decay = jnp.exp(b_last + m_prev - m_new)
        kw = kh.astype(f32) * jnp.exp(b_last - b_col + ig_col - m_new)
        C_ref[h] = decay * C_ref[h] + lax.dot_general(_bf(kw), vh, (((0,), (0,)), ((), ())),
                                                      preferred_element_type=f32)
        n_ref[h:h + 1, :] = decay * n_ref[h:h + 1, :] + jnp.sum(kw, axis=0, keepdims=True)
        m_ref[:, h:h + 1] = m_new


def _mlstm_prompt(q, k, v, igf, ln_g):
    B, T, _ = q.shape
    L = CHUNK
    row = lambda b, c: (b, c, 0)
    return pl.pallas_call(
        functools.partial(_mlstm_prompt_kernel, L=L),
        grid=(B, T // L),
        in_specs=[
            pl.BlockSpec((None, L, INNER), row),
            pl.BlockSpec((None, L, INNER), row),
            pl.BlockSpec((None, L, INNER), row),
            pl.BlockSpec((None, L, LANES), row),
            _const_spec((1, INNER)),
        ],
        out_specs=[
            pl.BlockSpec((None, L, INNER), row),
            pl.BlockSpec((None, HEADS, DH, DH), lambda b, c: (b, 0, 0, 0)),
            pl.BlockSpec((None, HEADS, DH), lambda b, c: (b, 0, 0)),
            pl.BlockSpec((None, 1, LANES), lambda b, c: (b, 0, 0)),
        ],
        out_shape=[
            jax.ShapeDtypeStruct((B, T, INNER), f32),
            jax.ShapeDtypeStruct((B, HEADS, DH, DH), f32),
            jax.ShapeDtypeStruct((B, HEADS, DH), f32),
            jax.ShapeDtypeStruct((B, 1, LANES), f32),
        ],
        compiler_params=_params("arbitrary", "arbitrary"),
        name="mlstm_prompt",
    )(q, k, v, igf, ln_g)


def _mlstm_step_kernel(dec_ref, c_ref, qt_ref, kwt_ref, v_ref, cout_ref, qc_ref, *, bb):
    g = pl.program_id(0)
    h = pl.program_id(1)
    for i in range(bb):
        decay = dec_ref[(g * bb + i) * HEADS + h]
        c = c_ref[i]
        qc_ref[i:i + 1, :] = jnp.sum(c * qt_ref[:, i:i + 1], axis=0, keepdims=True)
        cout_ref[i] = decay * c + kwt_ref[:, i:i + 1] * v_ref[i:i + 1, :]


def _mlstm_step(dec_flat, C, q, kw, v):
    R = C.shape[0]
    bb = STEP_BATCH
    G = R // bb
    cols = lambda a: a.reshape(G, bb, HEADS, DH).transpose(0, 2, 3, 1)
    rows = lambda a: a.reshape(G, bb, HEADS, DH).transpose(0, 2, 1, 3)
    gh = lambda g, h: (g, h, 0, 0)
    c_new, qc = pl.pallas_call(
        functools.partial(_mlstm_step_kernel, bb=bb),
        grid=(G, HEADS),
        in_specs=[
            pl.BlockSpec(memory_space=pltpu.SMEM),
            pl.BlockSpec((bb, None, DH, DH), gh),
            pl.BlockSpec((None, None, DH, bb), gh),
            pl.BlockSpec((None, None, DH, bb), gh),
            pl.BlockSpec((None, None, bb, DH), gh),
        ],
        out_specs=[
            pl.BlockSpec((bb, None, DH, DH), gh),
            pl.BlockSpec((None, None, bb, DH), gh),
        ],
        out_shape=[
            jax.ShapeDtypeStruct(C.shape, f32),
            jax.ShapeDtypeStruct((G, HEADS, bb, DH), f32),
        ],
        compiler_params=_params("arbitrary", "arbitrary"),
        name="mlstm_step",
    )(dec_flat, C, cols(q), cols(kw), rows(v))
    return c_new, qc.transpose(0, 2, 1, 3).reshape(R, INNER)


def _ml_back(hh, xc, z, x, mod, skip, wout_ref):
    act = (hh + skip * xc) * _silu(z)
    return x + mod[:, 2 * D_MODEL:] * _dot(_bf(act), wout_ref[...])


def _ml_back_prompt_kernel(hh_ref, xc_ref, z_ref, x_ref, mod_ref, skip_ref, wout_ref, o_ref):
    o_ref[...] = _ml_back(hh_ref[...], xc_ref[...], z_ref[...], x_ref[...], mod_ref[...], skip_ref[...], wout_ref)


def _ml_back_prompt(hh, xc, z, x, mod4, w):
    B, T, D = x.shape
    tm = ROW_TILE
    row = lambda b, t: (b, t, 0)
    return pl.pallas_call(
        _ml_back_prompt_kernel,
        grid=(B, T // tm),
        in_specs=[
            pl.BlockSpec((None, tm, INNER), row),
            pl.BlockSpec((None, tm, INNER), row),
            pl.BlockSpec((None, tm, INNER), row),
            pl.BlockSpec((None, tm, D), row),
            pl.BlockSpec((None, None, 1, 3 * D), lambda b, t: (0, b, 0, 0)),
            _const_spec((1, INNER)),
            _const_spec((INNER, D)),
        ],
        out_specs=pl.BlockSpec((None, tm, D), row),
        out_shape=jax.ShapeDtypeStruct((B, T, D), f32),
        compiler_params=_params("arbitrary", "arbitrary"),
        name="ml_back_prompt",
    )(hh, xc, z, x, mod4, w["ml_skip"], w["ml_w_out"])


def _ml_back_sample_kernel(a1_ref, a2_ref, v_ref, qc_ref, lng_ref, xc_ref, z_ref, x_ref, mod_ref, skip_ref,
                           wout_ref, o_ref):
    hs = []
    for h in range(HEADS):
        seg = slice(h * DH, (h + 1) * DH)
        hc = a1_ref[:, h:h + 1] * v_ref[:, seg] + a2_ref[:, h:h + 1] * qc_ref[:, seg]
        hs.append(_layernorm(hc, lng_ref[:, seg]))
    hh = jnp.concatenate(hs, axis=1)
    o_ref[...] = _ml_back(hh, xc_ref[...], z_ref[...], x_ref[...], mod_ref[...], skip_ref[...], wout_ref)


def _ml_back_sample(a1, a2, v, qc, xc, z, x, mod, w):
    R, D = x.shape
    full = lambda shape: pl.BlockSpec(shape, lambda i: (0,) * len(shape))
    return pl.pallas_call(
        _ml_back_sample_kernel,
        grid=(1,),
        in_specs=[full((R, LANES)), full((R, LANES)), full((R, INNER)), full((R, INNER)), full((1, INNER)),
                  full((R, INNER)), full((R, INNER)), full((R, D)), full((R, 3 * D)), full((1, INNER)),
                  full((INNER, D))],
        out_specs=full((R, D)),
        out_shape=jax.ShapeDtypeStruct((R, D), f32),
        compiler_params=_params("arbitrary"),
        name="ml_back_sample",
    )(a1, a2, v, qc, w["ml_ln_g"], xc, z, x, mod, w["ml_skip"], w["ml_w_out"])


def _cf_in(x, mod, g_ref, win_ref, bin_ref):
    h = _rms_mod(x, g_ref[...], mod)
    agz = _dot(_bf(h), win_ref[...]) + bin_ref[...]
    u = agz[:, :INNER] * jax.nn.sigmoid(agz[:, INNER:2 * INNER])
    return u, agz[:, 2 * INNER:]


def _cf_out(y, z, x, mod, lng_ref, lnb_ref, wout_ref, fg_ref):
    yn = _layernorm(y, lng_ref[...], lnb_ref[...])
    out = _dot(_bf(_silu(yn) * _silu(z)), wout_ref[...])
    return _rms(x + mod[:, 2 * D_MODEL:] * out, fg_ref[...])


def _dwconv_tile(ubuf_ref, wdw_ref, bdw_ref, y_ref, tm):
    base = CCONV_HIST - (CCONV - 1)

    def col_body(ci, carry):
        cols = pl.ds(pl.multiple_of(ci * CONV_COLS, CONV_COLS), CONV_COLS)
        w = wdw_ref[:, cols]
        b = bdw_ref[:, cols]
        for r in range(tm // CONV_ROWS):
            acc = jnp.broadcast_to(b, (CONV_ROWS, CONV_COLS))
            for j in range(CCONV):
                acc = acc + ubuf_ref[pl.ds(base + j + CONV_ROWS * r, CONV_ROWS), cols] * w[j:j + 1, :]
            y_ref[pl.ds(CONV_ROWS * r, CONV_ROWS), cols] = acc
        return carry

    lax.fori_loop(0, INNER // CONV_COLS, col_body, 0)


def _cf_prompt_kernel(x_ref, mod_ref, g_ref, win_ref, bin_ref, wdw_ref, bdw_ref, lng_ref, lnb_ref, wout_ref,
                      fg_ref, o_ref, cconv_ref, ubuf_ref, y_ref, *, tm):
    @pl.when(pl.program_id(1) == 0)
    def _():
        ubuf_ref[0:CCONV_HIST, :] = jnp.zeros((CCONV_HIST, INNER), f32)

    x = x_ref[...]
    mod = mod_ref[...]
    u, z = _cf_in(x, mod, g_ref, win_ref, bin_ref)
    ubuf_ref[CCONV_HIST:CCONV_HIST + tm, :] = u
    _dwconv_tile(ubuf_ref, wdw_ref, bdw_ref, y_ref, tm)
    cconv_ref[...] = ubuf_ref[CCONV_HIST + tm - (CCONV - 1):CCONV_HIST + tm, :]
    ubuf_ref[0:CCONV_HIST, :] = ubuf_ref[tm:tm + CCONV_HIST, :]
    o_ref[...] = _cf_out(y_ref[...], z, x, mod, lng_ref, lnb_ref, wout_ref, fg_ref)


def _cf_prompt(x, mod4, g, w, final_g):
    B, T, D = x.shape
    tm = ROW_TILE
    row = lambda b, t: (b, t, 0)
    return pl.pallas_call(
        functools.partial(_cf_prompt_kernel, tm=tm),
        grid=(B, T // tm),
        in_specs=[
            pl.BlockSpec((None, tm, D), row),
            pl.BlockSpec((None, None, 1, 3 * D), lambda b, t: (1, b, 0, 0)),
            _const_spec((1, D)),
            _const_spec((D, 3 * INNER)),
            _const_spec((1, 3 * INNER)),
            _const_spec((CCONV, INNER)),
            _const_spec((1, INNER)),
            _const_spec((1, INNER)),
            _const_spec((1, INNER)),
            _const_spec((INNER, D)),
            _const_spec((1, D)),
        ],
        out_specs=[
            pl.BlockSpec((None, tm, D), row),
            pl.BlockSpec((None, CCONV - 1, INNER), lambda b, t: (b, 0, 0)),
        ],
        out_shape=[
            jax.ShapeDtypeStruct((B, T, D), f32),
            jax.ShapeDtypeStruct((B, CCONV - 1, INNER), f32),
        ],
        scratch_shapes=[pltpu.VMEM((CCONV_HIST + tm, INNER), f32), pltpu.VMEM((tm, INNER), f32)],
        compiler_params=_params("arbitrary", "arbitrary"),
        name="cf_prompt",
    )(x, mod4, g, w["cf_w_in"], w["cf_b_in"], w["cf_w_dw"], w["cf_b_dw"], w["cf_ln_g"], w["cf_ln_b"],
      w["cf_w_out"], final_g)


def _cf_sample_kernel(x_ref, mod_ref, g_ref, win_ref, bin_ref, wdw_ref, bdw_ref, lng_ref, lnb_ref, wout_ref,
                      fg_ref, st_ref, o_ref, stout_ref, u_ref, z_ref, y_ref, *, tb):
    i = pl.program_id(0)

    @pl.when(i == 0)
    def _():
        u, z = _cf_in(x_ref[...], mod_ref[...], g_ref, win_ref, bin_ref)
        u_ref[...] = u
        z_ref[...] = z

    rows = pl.ds(pl.multiple_of(i * tb, tb), tb)
    u = u_ref[rows, :]
    acc = bdw_ref[...] + wdw_ref[CCONV - 1:CCONV, :] * u
    for j in range(CCONV - 1):
        acc = acc + wdw_ref[j:j + 1, :] * st_ref[:, j * INNER:(j + 1) * INNER]
    y_ref[rows, :] = acc
    stout_ref[:, 0:(CCONV - 2) * INNER] = st_ref[:, INNER:(CCONV - 1) * INNER]
    stout_ref[:, (CCONV - 2) * INNER:] = u

    @pl.when(i == pl.num_programs(0) - 1)
    def _():
        o_ref[...] = _cf_out(y_ref[...], z_ref[...], x_ref[...], mod_ref[...], lng_ref, lnb_ref, wout_ref, fg_ref)


def _cf_sample(x, mod, g, w, final_g, st):
    R, D = x.shape
    tb = SAMPLE_TILE
    W = (CCONV - 1) * INNER
    return pl.pallas_call(
        functools.partial(_cf_sample_kernel, tb=tb),
        grid=(R // tb,),
        in_specs=[
            _const_spec((R, D)),
            _const_spec((R, 3 * D)),
            _const_spec((1, D)),
            _const_spec((D, 3 * INNER)),
            _const_spec((1, 3 * INNER)),
            _const_spec((CCONV, INNER)),
            _const_spec((1, INNER)),
            _const_spec((1, INNER)),
            _const_spec((1, INNER)),
            _const_spec((INNER, D)),
            _const_spec((1, D)),
            pl.BlockSpec((tb, W), lambda i: (i, 0)),
        ],
        out_specs=[
            pl.BlockSpec((R, D), lambda i: (0, 0)),
            pl.BlockSpec((tb, W), lambda i: (i, 0)),
        ],
        out_shape=[
            jax.ShapeDtypeStruct((R, D), f32),
            jax.ShapeDtypeStruct((R, W), f32),
        ],
        scratch_shapes=[pltpu.VMEM((R, INNER), f32)] * 3,
        compiler_params=_params("arbitrary"),
        name="cf_sample",
    )(x, mod, g, w["cf_w_in"], w["cf_b_in"], w["cf_w_dw"], w["cf_b_dw"], w["cf_ln_g"], w["cf_ln_b"],
      w["cf_w_out"], final_g, st)


def kernel(x_prompt, x_sample, c_prompt, c_sample, state_mlstm_C, state_mlstm_n, state_mlstm_m, state_mlstm_conv, state_conf_conv, norm_g, w_ada, b_ada, ml_w_in, ml_w_conv, ml_b_conv, ml_w_q, ml_w_k, ml_w_v, ml_w_ig, ml_b_ig, ml_w_fg, ml_b_fg, ml_ln_g, ml_skip, ml_w_out, cf_w_in, cf_b_in, cf_w_dw, cf_b_dw, cf_ln_g, cf_ln_b, cf_w_out, final_g):
    B = x_prompt.shape[0]
    R = x_sample.shape[0]
    D = D_MODEL

    gate_pad = LANES - 2 * HEADS
    w = {
        "ml_w_in": _bf(ml_w_in[0]),
        "ml_w_conv": ml_w_conv[0],
        "ml_b_conv": ml_b_conv,
        "ml_w_q": _bf(ml_w_q[0]),
        "ml_w_k": _bf(ml_w_k[0]),
        "ml_w_v": _bf(ml_w_v[0]),
        "ml_w_igf": _bf(jnp.pad(jnp.concatenate([ml_w_ig[0], ml_w_fg[0]], axis=1), ((0, 0), (0, gate_pad)))),
        "ml_b_igf": jnp.pad(jnp.concatenate([ml_b_ig, ml_b_fg], axis=1), ((0, 0), (0, gate_pad))),
        "ml_ln_g": ml_ln_g.reshape(1, INNER),
        "ml_skip": ml_skip,
        "ml_w_out": _bf(ml_w_out[0]),
        "cf_w_in": _bf(cf_w_in[0]),
        "cf_b_in": cf_b_in,
        "cf_w_dw": cf_w_dw[0],
        "cf_b_dw": cf_b_dw,
        "cf_ln_g": cf_ln_g,
        "cf_ln_b": cf_ln_b,
        "cf_w_out": _bf(cf_w_out[0]),
    }
    final_g2 = final_g.reshape(1, D)

    mod = _ada(jnp.concatenate([c_prompt, c_sample], axis=0), w_ada, b_ada)
    mod4 = mod.reshape(mod.shape[0], B + R, 1, 3 * D)
    mod_s = mod[:, B:]

    q, k, v, xc, z, igf, mconv_p = _ml_front_prompt(x_prompt, mod4, norm_g[0:1], w)
    hh, C_p, n_p, m_p = _mlstm_prompt(q, k, v, igf, w["ml_ln_g"])
    x1 = _ml_back_prompt(hh, xc, z, x_prompt, mod4, w)
    y_prompt, cconv_p = _cf_prompt(x1, mod4, norm_g[1:2], w, final_g2)

    xs = x_sample.reshape(R, D)
    m_pad = jnp.pad(state_mlstm_m[0], ((0, 0), (0, LANES - HEADS)))
    (q_s, kw_s, v_s, xc_s, z_s, n_s, a1, a2, dec, m_s, mconv_s) = _ml_front_sample(
        xs, mod_s[0], norm_g[0:1], w, state_mlstm_conv[0].reshape(R, (MCONV - 1) * INNER),
        state_mlstm_n[0].reshape(R, INNER), m_pad)
    C_s, qc = _mlstm_step(dec[:, :HEADS].reshape(R * HEADS), state_mlstm_C[0], q_s, kw_s, v_s)
    x1_s = _ml_back_sample(a1, a2, v_s, qc, xc_s, z_s, xs, mod_s[0], w)
    y_sample, cconv_s = _cf_sample(x1_s, mod_s[1], norm_g[1:2], w, final_g2,
                                   state_conf_conv[0].reshape(R, (CCONV - 1) * INNER))

    return (
        y_prompt,
        y_sample.reshape(R, 1, D),
        C_p[None],
        C_s[None],
        n_p[None],
        n_s.reshape(1, R, HEADS, DH),
        m_p[:, 0, :HEADS][None],
        m_s[:, :HEADS][None],
        mconv_p[None],
        mconv_s.reshape(1, R, MCONV - 1, INNER),
        cconv_p[None],
        cconv_s.reshape(1, R, CCONV - 1, INNER),
    )
```

```python
import functools

import jax
import jax.numpy as jnp
from jax import lax
from jax.experimental import pallas as pl
from jax.experimental.pallas import tpu as pltpu

D_MODEL = 1024
INNER = 2 * D_MODEL
HEADS = 4
DH = INNER // HEADS
MCONV = 4
CCONV = 31
EPS = 1e-6
NEG = -1e30
K_SCALE = DH ** -0.5

LANES = 128
SUBLANES = 8
VMEM_LIMIT = 56 * 2 ** 20

ROW_TILE = 256
CHUNK = 256
STEP_BATCH = 8
CONV_ROWS = 64
CONV_COLS = 256
CCONV_HIST = 32
SAMPLE_TILE = 16

f32 = jnp.float32
bf16 = jnp.bfloat16


def _bf(x):
    return x.astype(bf16)


def _dot(a, b):
    return jnp.dot(a, b, preferred_element_type=f32)


def _silu(x):
    return x * jax.nn.sigmoid(x)


def _log_sigmoid(x):
    return jnp.minimum(x, 0.0) - jnp.log1p(jnp.exp(-jnp.abs(x)))


def _rms(x, g):
    return x * lax.rsqrt(jnp.mean(x * x, axis=-1, keepdims=True) + EPS) * g


def _rms_mod(x, g, mod):
    return _rms(x, g) * (1.0 + mod[:, D_MODEL:2 * D_MODEL]) + mod[:, :D_MODEL]


def _layernorm(x, g, b=None):
    mu = jnp.mean(x, axis=-1, keepdims=True)
    xc = x - mu
    var = jnp.mean(xc * xc, axis=-1, keepdims=True)
    y = xc * lax.rsqrt(var + EPS) * g
    return y if b is None else y + b


def _const_spec(shape):
    n = len(shape)
    return pl.BlockSpec(shape, lambda *_: (0,) * n, pipeline_mode=pl.Buffered(1))


def _params(*sem):
    return pltpu.CompilerParams(dimension_semantics=sem, vmem_limit_bytes=VMEM_LIMIT)


def _ada_kernel(c_ref, w_ref, b_ref, o_ref):
    o_ref[...] = _dot(_bf(_silu(c_ref[...])), _bf(w_ref[...])) + b_ref[...]


def _ada(c_all, w_ada, b_ada):
    depth, d, d3 = w_ada.shape
    rows = c_all.shape[0]
    tn = D_MODEL
    return pl.pallas_call(
        _ada_kernel,
        grid=(depth, d3 // tn),
        in_specs=[
            pl.BlockSpec((rows, d), lambda i, j: (0, 0)),
            pl.BlockSpec((None, d, tn), lambda i, j: (i, 0, j)),
            pl.BlockSpec((None, 1, tn), lambda i, j: (i, 0, j)),
        ],
        out_specs=pl.BlockSpec((None, rows, tn), lambda i, j: (i, 0, j)),
        out_shape=jax.ShapeDtypeStruct((depth, rows, d3), f32),
        compiler_params=_params("arbitrary", "arbitrary"),
        name="ada",
    )(c_all, w_ada, b_ada.reshape(depth, 1, d3))


def _ml_qkv(xm, xc, wq_ref, wk_ref, wv_ref, wigf_ref, bigf_ref):
    xcb, xmb = _bf(xc), _bf(xm)
    qs, ks, vs = [], [], []
    for h in range(HEADS):
        seg = slice(h * DH, (h + 1) * DH)
        qs.append(_dot(xcb[:, seg], wq_ref[h]))
        ks.append(_dot(xcb[:, seg], wk_ref[h]))
        vs.append(_dot(xmb[:, seg], wv_ref[h]))
    q = jnp.concatenate(qs, axis=1)
    k = jnp.concatenate(ks, axis=1)
    v = jnp.concatenate(vs, axis=1)
    pre = (_dot(_bf(q), wigf_ref[0:INNER, :]) + _dot(_bf(k), wigf_ref[INNER:2 * INNER, :])
           + _dot(_bf(v), wigf_ref[2 * INNER:3 * INNER, :]) + bigf_ref[...])
    lane = lax.broadcasted_iota(jnp.int32, pre.shape, 1)
    gates = jnp.where(lane >= HEADS, _log_sigmoid(pre), pre)
    return q, k, v, gates


def _ml_front_prompt_kernel(x_ref, mod_ref, g_ref, win_ref, wconv_ref, bconv_ref, wq_ref, wk_ref, wv_ref,
                            wigf_ref, bigf_ref,
                            q_ref, k_ref, v_ref, xc_ref, z_ref, igf_ref, mconv_ref, xbuf_ref, *, tm):
    @pl.when(pl.program_id(1) == 0)
    def _():
        xbuf_ref[0:SUBLANES, :] = jnp.zeros((SUBLANES, INNER), f32)

    h = _rms_mod(x_ref[...], g_ref[...], mod_ref[...])
    xmz = _dot(_bf(h), win_ref[...])
    xm = xmz[:, :INNER]
    z_ref[...] = xmz[:, INNER:]
    xbuf_ref[SUBLANES:SUBLANES + tm, :] = xm
    acc = bconv_ref[...] + wconv_ref[MCONV - 1:MCONV, :] * xm
    for j in range(MCONV - 1):
        start = SUBLANES - (MCONV - 1) + j
        acc = acc + wconv_ref[j:j + 1, :] * xbuf_ref[start:start + tm, :]
    xc = _silu(acc)
    xc_ref[...] = xc
    mconv_ref[...] = xbuf_ref[SUBLANES + tm - (MCONV - 1):SUBLANES + tm, :]
    xbuf_ref[0:SUBLANES, :] = xbuf_ref[tm:tm + SUBLANES, :]

    q, k, v, gates = _ml_qkv(xm, xc, wq_ref, wk_ref, wv_ref, wigf_ref, bigf_ref)
    q_ref[...] = _bf(q)
    k_ref[...] = _bf(k * K_SCALE)
    v_ref[...] = _bf(v)
    igf_ref[...] = gates


def _ml_front_prompt(x, mod4, g, w):
    B, T, D = x.shape
    tm = ROW_TILE
    row = lambda b, t: (b, t, 0)
    act = lambda dt: jax.ShapeDtypeStruct((B, T, INNER), dt)
    return pl.pallas_call(
        functools.partial(_ml_front_prompt_kernel, tm=tm),
        grid=(B, T // tm),
        in_specs=[
            pl.BlockSpec((None, tm, D), row),
            pl.BlockSpec((None, None, 1, 3 * D), lambda b, t: (0, b, 0, 0)),
            _const_spec((1, D)),
            _const_spec((D, 2 * INNER)),
            _const_spec((MCONV, INNER)),
            _const_spec((1, INNER)),
            _const_spec((HEADS, DH, DH)),
            _const_spec((HEADS, DH, DH)),
            _const_spec((HEADS, DH, DH)),
            _const_spec((3 * INNER, LANES)),
            _const_spec((1, LANES)),
        ],
        out_specs=[
            pl.BlockSpec((None, tm, INNER), row),
            pl.BlockSpec((None, tm, INNER), row),
            pl.BlockSpec((None, tm, INNER), row),
            pl.BlockSpec((None, tm, INNER), row),
            pl.BlockSpec((None, tm, INNER), row),
            pl.BlockSpec((None, tm, LANES), row),
            pl.BlockSpec((None, MCONV - 1, INNER), lambda b, t: (b, 0, 0)),
        ],
        out_shape=[act(bf16), act(bf16), act(bf16), act(f32), act(f32),
                   jax.ShapeDtypeStruct((B, T, LANES), f32),
                   jax.ShapeDtypeStruct((B, MCONV - 1, INNER), f32)],
        scratch_shapes=[pltpu.VMEM((SUBLANES + tm, INNER), f32)],
        compiler_params=_params("arbitrary", "arbitrary"),
        name="ml_front_prompt",
    )(x, mod4, g, w["ml_w_in"], w["ml_w_conv"], w["ml_b_conv"], w["ml_w_q"], w["ml_w_k"], w["ml_w_v"],
      w["ml_w_igf"], w["ml_b_igf"])


def _ml_front_sample_kernel(x_ref, mod_ref, g_ref, win_ref, wconv_ref, bconv_ref, wq_ref, wk_ref, wv_ref,
                            wigf_ref, bigf_ref, st_ref, n_ref, m_ref,
                            q_ref, kw_ref, v_ref, xc_ref, z_ref, nnew_ref, a1_ref, a2_ref, dec_ref, mnew_ref,
                            mconv_ref):
    h = _rms_mod(x_ref[...], g_ref[...], mod_ref[...])
    xmz = _dot(_bf(h), win_ref[...])
    xm = xmz[:, :INNER]
    z_ref[...] = xmz[:, INNER:]
    acc = bconv_ref[...] + wconv_ref[MCONV - 1:MCONV, :] * xm
    for j in range(MCONV - 1):
        acc = acc + wconv_ref[j:j + 1, :] * st_ref[:, j, :]
    xc = _silu(acc)
    xc_ref[...] = xc
    mconv_ref[:, 0:MCONV - 2, :] = st_ref[:, 1:MCONV - 1, :]
    mconv_ref[:, MCONV - 2, :] = xm

    q, k, v, gates = _ml_qkv(xm, xc, wq_ref, wk_ref, wv_ref, wigf_ref, bigf_ref)
    ks = k * K_SCALE
    q_ref[...] = q
    v_ref[...] = v

    ig = gates
    lf = pltpu.roll(gates, LANES - HEADS, axis=1)
    inter = lf + m_ref[...]
    m_new = jnp.maximum(inter, ig)
    w_intra = jnp.exp(ig - m_new)
    w_inter = jnp.exp(inter - m_new)
    lane = lax.broadcasted_iota(jnp.int32, ig.shape, 1)
    qk = jnp.zeros_like(ig)
    qn = jnp.zeros_like(ig)
    for h in range(HEADS):
        seg = slice(h * DH, (h + 1) * DH)
        qk = jnp.where(lane == h, jnp.sum(q[:, seg] * ks[:, seg], axis=1, keepdims=True), qk)
        qn = jnp.where(lane == h, jnp.sum(q[:, seg] * n_ref[:, seg], axis=1, keepdims=True), qn)
        kw = w_intra[:, h:h + 1] * ks[:, seg]
        kw_ref[:, seg] = kw
        nnew_ref[:, seg] = w_inter[:, h:h + 1] * n_ref[:, seg] + kw
    s = qk * w_intra
    den = s + w_inter * qn
    dn = jnp.maximum(jnp.abs(den), jnp.exp(-m_new))
    valid = lane < HEADS
    a1_ref[...] = jnp.where(valid, s / dn, 0.0)
    a2_ref[...] = jnp.where(valid, w_inter / dn, 0.0)
    dec_ref[...] = jnp.where(valid, w_inter, 0.0)
    mnew_ref[...] = jnp.where(valid, m_new, 0.0)


def _ml_front_sample(x, mod, g, w, st, n, m):
    R, D = x.shape
    full = lambda shape: pl.BlockSpec(shape, lambda i: (0,) * len(shape))
    act = jax.ShapeDtypeStruct((R, INNER), f32)
    small = jax.ShapeDtypeStruct((R, LANES), f32)
    return pl.pallas_call(
        _ml_front_sample_kernel,
        grid=(1,),
        in_specs=[
            full((R, D)), full((R, 3 * D)), full((1, D)), full((D, 2 * INNER)), full((MCONV, INNER)),
            full((1, INNER)), full((HEADS, DH, DH)), full((HEADS, DH, DH)), full((HEADS, DH, DH)),
            full((3 * INNER, LANES)), full((1, LANES)),
            full((R, MCONV - 1, INNER)), full((R, INNER)), full((R, LANES)),
        ],
        out_specs=[full((R, INNER))] * 6 + [full((R, LANES))] * 4 + [full((R, MCONV - 1, INNER))],
        out_shape=[act] * 6 + [small] * 4 + [jax.ShapeDtypeStruct((R, MCONV - 1, INNER), f32)],
        compiler_params=_params("arbitrary"),
        name="ml_front_sample",
    )(x, mod, g, w["ml_w_in"], w["ml_w_conv"], w["ml_b_conv"], w["ml_w_q"], w["ml_w_k"], w["ml_w_v"],
      w["ml_w_igf"], w["ml_b_igf"], st, n, m)


def _mlstm_prompt_kernel(q_ref, k_ref, v_ref, igf_ref, lng_ref, hh_ref, C_ref, n_ref, m_ref, *, L):
    @pl.when(pl.program_id(1) == 0)
    def _():
        C_ref[...] = jnp.zeros_like(C_ref)
        n_ref[...] = jnp.zeros_like(n_ref)
        m_ref[...] = jnp.zeros_like(m_ref)

    igf = igf_ref[...]
    row = lax.broadcasted_iota(jnp.int32, (L, L), 0)
    col = lax.broadcasted_iota(jnp.int32, (L, L), 1)
    causal = row >= col
    tri = jnp.where(causal, 1.0, 0.0).astype(bf16)
    hi = _bf(igf)
    lo = _bf(igf - hi.astype(f32))
    bcum = _dot(tri, hi) + _dot(tri, lo)
    igf_t = igf.T
    bcum_t = bcum.T

    for h in range(HEADS):
        seg = slice(h * DH, (h + 1) * DH)
        ig_col = igf[:, h:h + 1]
        b_col = bcum[:, HEADS + h:HEADS + h + 1]
        ig_row = igf_t[h:h + 1, :]
        b_row = bcum_t[HEADS + h:HEADS + h + 1, :]
        m_prev = m_ref[:, h:h + 1]
        qh, kh, vh = q_ref[:, seg], k_ref[:, seg], v_ref[:, seg]

        dmat = jnp.where(causal, b_col - b_row + ig_row, NEG)
        inter = b_col + m_prev
        m_t = jnp.maximum(inter, jnp.max(dmat, axis=1, keepdims=True))
        w_intra = jnp.exp(dmat - m_t)
        w_inter = jnp.exp(inter - m_t)
        s = lax.dot_general(qh, kh, (((1,), (1,)), ((), ())), preferred_element_type=f32) * w_intra
        num = _dot(_bf(s), vh) + w_inter * _dot(qh, _bf(C_ref[h]))
        qn = jnp.sum(qh.astype(f32) * n_ref[h:h + 1, :], axis=1, keepdims=True)
        den = jnp.sum(s, axis=1, keepdims=True) + w_inter * qn
        hc = num / jnp.maximum(jnp.abs(den), jnp.exp(-m_t))
        hh_ref[:, seg] = _layernorm(hc, lng_ref[:, seg])

        m_new = m_t[L - 1:L, :]
        b_last = b_col[L - 1:L, :]
        decay = jnp.exp(b_last + m_prev - m_new)
        kw = kh.astype(f32) * jnp.exp(b_last - b_col + ig_col - m_new)
        C_ref[h] = decay * C_ref[h] + lax.dot_general(_bf(kw), vh, (((0,), (0,)), ((), ())),
                                                      preferred_element_type=f32)
        n_ref[h:h + 1, :] = decay * n_ref[h:h + 1, :] + jnp.sum(kw, axis=0, keepdims=True)
        m_ref[:, h:h + 1] = m_new


def _mlstm_prompt(q, k, v, igf, ln_g):
    B, T, _ = q.shape
    L = CHUNK
    row = lambda b, c: (b, c, 0)
    return pl.pallas_call(
        functools.partial(_mlstm_prompt_kernel, L=L),
        grid=(B, T // L),
        in_specs=[
            pl.BlockSpec((None, L, INNER), row),
            pl.BlockSpec((None, L, INNER), row),
            pl.BlockSpec((None, L, INNER), row),
            pl.BlockSpec((None, L, LANES), row),
            _const_spec((1, INNER)),
        ],
        out_specs=[
            pl.BlockSpec((None, L, INNER), row),
            pl.BlockSpec((None, HEADS, DH, DH), lambda b, c: (b, 0, 0, 0)),
            pl.BlockSpec((None, HEADS, DH), lambda b, c: (b, 0, 0)),
            pl.BlockSpec((None, 1, LANES), lambda b, c: (b, 0, 0)),
        ],
        out_shape=[
            jax.ShapeDtypeStruct((B, T, INNER), f32),
            jax.ShapeDtypeStruct((B, HEADS, DH, DH), f32),
            jax.ShapeDtypeStruct((B, HEADS, DH), f32),
            jax.ShapeDtypeStruct((B, 1, LANES), f32),
        ],
        compiler_params=_params("arbitrary", "arbitrary"),
        name="mlstm_prompt",
    )(q, k, v, igf, ln_g)


def _mlstm_step_kernel(dec_ref, c_ref, q_ref, kw_ref, v_ref, cout_ref, qc_ref, *, bb):
    g = pl.program_id(0)
    h = pl.program_id(1)
    qt = q_ref[...].T
    kwt = kw_ref[...].T
    for i in range(bb):
        decay = dec_ref[(g * bb + i) * HEADS + h]
        c = c_ref[i]
        qc_ref[i:i + 1, :] = jnp.sum(c * qt[:, i:i + 1], axis=0, keepdims=True)
        cout_ref[i] = decay * c + kwt[:, i:i + 1] * v_ref[i:i + 1, :]


def _mlstm_step(dec_flat, C, q, kw, v):
    R = C.shape[0]
    bb = STEP_BATCH
    gh = lambda g, h: (g, h, 0, 0)
    vec = pl.BlockSpec((bb, DH), lambda g, h: (g, h))
    return pl.pallas_call(
        functools.partial(_mlstm_step_kernel, bb=bb),
        grid=(R // bb, HEADS),
        in_specs=[
            pl.BlockSpec(memory_space=pltpu.SMEM),
            pl.BlockSpec((bb, None, DH, DH), gh),
            vec, vec, vec,
        ],
        out_specs=[pl.BlockSpec((bb, None, DH, DH), gh), vec],
        out_shape=[jax.ShapeDtypeStruct(C.shape, f32), jax.ShapeDtypeStruct((R, INNER), f32)],
        compiler_params=_params("arbitrary", "arbitrary"),
        name="mlstm_step",
    )(dec_flat, C, q, kw, v)


def _ml_back(hh, xc, z, x, mod, skip, wout_ref):
    act = (hh + skip * xc) * _silu(z)
    return x + mod[:, 2 * D_MODEL:] * _dot(_bf(act), wout_ref[...])


def _ml_back_prompt_kernel(hh_ref, xc_ref, z_ref, x_ref, mod_ref, skip_ref, wout_ref, o_ref):
    o_ref[...] = _ml_back(hh_ref[...], xc_ref[...], z_ref[...], x_ref[...], mod_ref[...], skip_ref[...], wout_ref)


def _ml_back_prompt(hh, xc, z, x, mod4, w):
    B, T, D = x.shape
    tm = ROW_TILE
    row = lambda b, t: (b, t, 0)
    return pl.pallas_call(
        _ml_back_prompt_kernel,
        grid=(B, T // tm),
        in_specs=[
            pl.BlockSpec((None, tm, INNER), row),
            pl.BlockSpec((None, tm, INNER), row),
            pl.BlockSpec((None, tm, INNER), row),
            pl.BlockSpec((None, tm, D), row),
            pl.BlockSpec((None, None, 1, 3 * D), lambda b, t: (0, b, 0, 0)),
            _const_spec((1, INNER)),
            _const_spec((INNER, D)),
        ],
        out_specs=pl.BlockSpec((None, tm, D), row),
        out_shape=jax.ShapeDtypeStruct((B, T, D), f32),
        compiler_params=_params("arbitrary", "arbitrary"),
        name="ml_back_prompt",
    )(hh, xc, z, x, mod4, w["ml_skip"], w["ml_w_out"])


def _ml_back_sample_kernel(a1_ref, a2_ref, v_ref, qc_ref, lng_ref, xc_ref, z_ref, x_ref, mod_ref, skip_ref,
                           wout_ref, o_ref):
    hs = []
    for h in range(HEADS):
        seg = slice(h * DH, (h + 1) * DH)
        hc = a1_ref[:, h:h + 1] * v_ref[:, seg] + a2_ref[:, h:h + 1] * qc_ref[:, seg]
        hs.append(_layernorm(hc, lng_ref[:, seg]))
    hh = jnp.concatenate(hs, axis=1)
    o_ref[...] = _ml_back(hh, xc_ref[...], z_ref[...], x_ref[...], mod_ref[...], skip_ref[...], wout_ref)


def _ml_back_sample(a1, a2, v, qc, xc, z, x, mod, w):
    R, D = x.shape
    full = lambda shape: pl.BlockSpec(shape, lambda i: (0,) * len(shape))
    return pl.pallas_call(
        _ml_back_sample_kernel,
        grid=(1,),
        in_specs=[full((R, LANES)), full((R, LANES)), full((R, INNER)), full((R, INNER)), full((1, INNER)),
                  full((R, INNER)), full((R, INNER)), full((R, D)), full((R, 3 * D)), full((1, INNER)),
                  full((INNER, D))],
        out_specs=full((R, D)),
        out_shape=jax.ShapeDtypeStruct((R, D), f32),
        compiler_params=_params("arbitrary"),
        name="ml_back_sample",
    )(a1, a2, v, qc, w["ml_ln_g"], xc, z, x, mod, w["ml_skip"], w["ml_w_out"])


def _cf_in(x, mod, g_ref, win_ref, bin_ref):
    h = _rms_mod(x, g_ref[...], mod)
    agz = _dot(_bf(h), win_ref[...]) + bin_ref[...]
    u = agz[:, :INNER] * jax.nn.sigmoid(agz[:, INNER:2 * INNER])
    return u, agz[:, 2 * INNER:]


def _cf_out(y, z, x, mod, lng_ref, lnb_ref, wout_ref, fg_ref):
    yn = _layernorm(y, lng_ref[...], lnb_ref[...])
    out = _dot(_bf(_silu(yn) * _silu(z)), wout_ref[...])
    return _rms(x + mod[:, 2 * D_MODEL:] * out, fg_ref[...])


def _dwconv_tile(ubuf_ref, sh_ref, wb_ref, bdw_ref, y_ref, tm):
    base = CCONV_HIST - (CCONV - 1)
    max_row_off = (base + CCONV - 2) // SUBLANES * SUBLANES

    def col_body(ci, carry):
        cols = pl.ds(pl.multiple_of(ci * CONV_COLS, CONV_COLS), CONV_COLS)
        for r in range(1, SUBLANES):
            sh_ref[r - 1] = ubuf_ref[r:r + tm + max_row_off, cols]
        b = bdw_ref[:, cols]
        for rc in range(tm // CONV_ROWS):
            acc = jnp.broadcast_to(b, (CONV_ROWS, CONV_COLS))
            for j in range(CCONV):
                a, r = divmod(base + j, SUBLANES)
                rows = pl.ds(SUBLANES * a + CONV_ROWS * rc, CONV_ROWS)
                src = ubuf_ref[rows, cols] if r == 0 else sh_ref[r - 1, rows, :]
                wj = wb_ref[SUBLANES * j:SUBLANES * (j + 1), cols]
                src = src.reshape(CONV_ROWS // SUBLANES, SUBLANES, CONV_COLS)
                acc = acc + (src * wj[None]).reshape(CONV_ROWS, CONV_COLS)
            y_ref[pl.ds(CONV_ROWS * rc, CONV_ROWS), cols] = acc
        return carry

    lax.fori_loop(0, INNER // CONV_COLS, col_body, 0)


def _cf_prompt_kernel(x_ref, mod_ref, g_ref, win_ref, bin_ref, wdw_ref, bdw_ref, lng_ref, lnb_ref, wout_ref,
                      fg_ref, o_ref, cconv_ref, ubuf_ref, y_ref, sh_ref, wb_ref, *, tm):
    @pl.when((pl.program_id(0) == 0) & (pl.program_id(1) == 0))
    def _():
        for j in range(CCONV):
            wb_ref[SUBLANES * j:SUBLANES * (j + 1), :] = jnp.broadcast_to(wdw_ref[j:j + 1, :], (SUBLANES, INNER))

    @pl.when(pl.program_id(1) == 0)
    def _():
        ubuf_ref[0:CCONV_HIST, :] = jnp.zeros((CCONV_HIST, INNER), f32)

    x = x_ref[...]
    mod = mod_ref[...]
    u, z = _cf_in(x, mod, g_ref, win_ref, bin_ref)
    ubuf_ref[CCONV_HIST:CCONV_HIST + tm, :] = u
    _dwconv_tile(ubuf_ref, sh_ref, wb_ref, bdw_ref, y_ref, tm)
    cconv_ref[...] = ubuf_ref[CCONV_HIST + tm - (CCONV - 1):CCONV_HIST + tm, :]
    ubuf_ref[0:CCONV_HIST, :] = ubuf_ref[tm:tm + CCONV_HIST, :]
    o_ref[...] = _cf_out(y_ref[...], z, x, mod, lng_ref, lnb_ref, wout_ref, fg_ref)


def _cf_prompt(x, mod4, g, w, final_g):
    B, T, D = x.shape
    tm = ROW_TILE
    row = lambda b, t: (b, t, 0)
    return pl.pallas_call(
        functools.partial(_cf_prompt_kernel, tm=tm),
        grid=(B, T // tm),
        in_specs=[
            pl.BlockSpec((None, tm, D), row),
            pl.BlockSpec((None, None, 1, 3 * D), lambda b, t: (1, b, 0, 0)),
            _const_spec((1, D)),
            _const_spec((D, 3 * INNER)),
            _const_spec((1, 3 * INNER)),
            _const_spec((CCONV, INNER)),
            _const_spec((1, INNER)),
            _const_spec((1, INNER)),
            _const_spec((1, INNER)),
            _const_spec((INNER, D)),
            _const_spec((1, D)),
        ],
        out_specs=[
            pl.BlockSpec((None, tm, D), row),
            pl.BlockSpec((None, CCONV - 1, INNER), lambda b, t: (b, 0, 0)),
        ],
        out_shape=[
            jax.ShapeDtypeStruct((B, T, D), f32),
            jax.ShapeDtypeStruct((B, CCONV - 1, INNER), f32),
        ],
        scratch_shapes=[
            pltpu.VMEM((CCONV_HIST + tm, INNER), f32),
            pltpu.VMEM((tm, INNER), f32),
            pltpu.VMEM((SUBLANES - 1, tm + CCONV_HIST - SUBLANES, CONV_COLS), f32),
            pltpu.VMEM((SUBLANES * CCONV, INNER), f32),
        ],
        compiler_params=_params("arbitrary", "arbitrary"),
        name="cf_prompt",
    )(x, mod4, g, w["cf_w_in"], w["cf_b_in"], w["cf_w_dw"], w["cf_b_dw"], w["cf_ln_g"], w["cf_ln_b"],
      w["cf_w_out"], final_g)


def _cf_sample_kernel(x_ref, mod_ref, g_ref, win_ref, bin_ref, wdw_ref, bdw_ref, lng_ref, lnb_ref, wout_ref,
                      fg_ref, st_ref, o_ref, stout_ref, u_ref, z_ref, y_ref, *, tb):
    i = pl.program_id(0)

    @pl.when(i == 0)
    def _():
        u, z = _cf_in(x_ref[...], mod_ref[...], g_ref, win_ref, bin_ref)
        u_ref[...] = u
        z_ref[...] = z

    rows = pl.ds(pl.multiple_of(i * tb, tb), tb)
    u = u_ref[rows, :]
    acc = bdw_ref[...] + wdw_ref[CCONV - 1:CCONV, :] * u
    for j in range(CCONV - 1):
        acc = acc + wdw_ref[j:j + 1, :] * st_ref[:, j, :]
    y_ref[rows, :] = acc
    stout_ref[:, 0:CCONV - 2, :] = st_ref[:, 1:CCONV - 1, :]
    stout_ref[:, CCONV - 2, :] = u

    @pl.when(i == pl.num_programs(0) - 1)
    def _():
        o_ref[...] = _cf_out(y_ref[...], z_ref[...], x_ref[...], mod_ref[...], lng_ref, lnb_ref, wout_ref, fg_ref)


def _cf_sample(x, mod, g, w, final_g, st):
    R, D = x.shape
    tb = SAMPLE_TILE
    state = pl.BlockSpec((tb, CCONV - 1, INNER), lambda i: (i, 0, 0))
    return pl.pallas_call(
        functools.partial(_cf_sample_kernel, tb=tb),
        grid=(R // tb,),
        in_specs=[
            _const_spec((R, D)),
            _const_spec((R, 3 * D)),
            _const_spec((1, D)),
            _const_spec((D, 3 * INNER)),
            _const_spec((1, 3 * INNER)),
            _const_spec((CCONV, INNER)),
            _const_spec((1, INNER)),
            _const_spec((1, INNER)),
            _const_spec((1, INNER)),
            _const_spec((INNER, D)),
            _const_spec((1, D)),
            state,
        ],
        out_specs=[pl.BlockSpec((R, D), lambda i: (0, 0)), state],
        out_shape=[jax.ShapeDtypeStruct((R, D), f32), jax.ShapeDtypeStruct((R, CCONV - 1, INNER), f32)],
        scratch_shapes=[pltpu.VMEM((R, INNER), f32)] * 3,
        compiler_params=_params("arbitrary"),
        name="cf_sample",
    )(x, mod, g, w["cf_w_in"], w["cf_b_in"], w["cf_w_dw"], w["cf_b_dw"], w["cf_ln_g"], w["cf_ln_b"],
      w["cf_w_out"], final_g, st)


def kernel(x_prompt, x_sample, c_prompt, c_sample, state_mlstm_C, state_mlstm_n, state_mlstm_m, state_mlstm_conv, state_conf_conv, norm_g, w_ada, b_ada, ml_w_in, ml_w_conv, ml_b_conv, ml_w_q, ml_w_k, ml_w_v, ml_w_ig, ml_b_ig, ml_w_fg, ml_b_fg, ml_ln_g, ml_skip, ml_w_out, cf_w_in, cf_b_in, cf_w_dw, cf_b_dw, cf_ln_g, cf_ln_b, cf_w_out, final_g):
    B = x_prompt.shape[0]
    R = x_sample.shape[0]
    D = D_MODEL

    gate_pad = LANES - 2 * HEADS
    w = {
        "ml_w_in": _bf(ml_w_in[0]),
        "ml_w_conv": ml_w_conv[0],
        "ml_b_conv": ml_b_conv,
        "ml_w_q": _bf(ml_w_q[0]),
        "ml_w_k": _bf(ml_w_k[0]),
        "ml_w_v": _bf(ml_w_v[0]),
        "ml_w_igf": _bf(jnp.pad(jnp.concatenate([ml_w_ig[0], ml_w_fg[0]], axis=1), ((0, 0), (0, gate_pad)))),
        "ml_b_igf": jnp.pad(jnp.concatenate([ml_b_ig, ml_b_fg], axis=1), ((0, 0), (0, gate_pad))),
        "ml_ln_g": ml_ln_g.reshape(1, INNER),
        "ml_skip": ml_skip,
        "ml_w_out": _bf(ml_w_out[0]),
        "cf_w_in": _bf(cf_w_in[0]),
        "cf_b_in": cf_b_in,
        "cf_w_dw": cf_w_dw[0],
        "cf_b_dw": cf_b_dw,
        "cf_ln_g": cf_ln_g,
        "cf_ln_b": cf_ln_b,
        "cf_w_out": _bf(cf_w_out[0]),
    }
    final_g2 = final_g.reshape(1, D)

    mod = _ada(jnp.concatenate([c_prompt, c_sample], axis=0), w_ada, b_ada)
    mod4 = mod.reshape(mod.shape[0], B + R, 1, 3 * D)
    mod_s = mod[:, B:]

    q, k, v, xc, z, igf, mconv_p = _ml_front_prompt(x_prompt, mod4, norm_g[0:1], w)
    hh, C_p, n_p, m_p = _mlstm_prompt(q, k, v, igf, w["ml_ln_g"])
    x1 = _ml_back_prompt(hh, xc, z, x_prompt, mod4, w)
    y_prompt, cconv_p = _cf_prompt(x1, mod4, norm_g[1:2], w, final_g2)

    xs = x_sample.reshape(R, D)
    m_pad = jnp.pad(state_mlstm_m[0], ((0, 0), (0, LANES - HEADS)))
    (q_s, kw_s, v_s, xc_s, z_s, n_s, a1, a2, dec, m_s, mconv_s) = _ml_front_sample(
        xs, mod_s[0], norm_g[0:1], w, state_mlstm_conv[0],
        state_mlstm_n[0].reshape(R, INNER), m_pad)
    C_s, qc = _mlstm_step(dec[:, :HEADS].reshape(R * HEADS), state_mlstm_C[0], q_s, kw_s, v_s)
    x1_s = _ml_back_sample(a1, a2, v_s, qc, xc_s, z_s, xs, mod_s[0], w)
    y_sample, cconv_s = _cf_sample(x1_s, mod_s[1], norm_g[1:2], w, final_g2,
                                   state_conf_conv[0])

    return (
        y_prompt,
        y_sample.reshape(R, 1, D),
        C_p[None],
        C_s[None],
        n_p[None],
        n_s.reshape(1, R, HEADS, DH),
        m_p[:, 0, :HEADS][None],
        m_s[:, :HEADS][None],
        mconv_p[None],
        mconv_s[None],
        cconv_p[None],
        cconv_s[None],
    )
```

```python
import functools

import jax
import jax.numpy as jnp
from jax import lax
from jax.experimental import pallas as pl
from jax.experimental.pallas import tpu as pltpu

D_MODEL = 1024
INNER = 2 * D_MODEL
HEADS = 4
DH = INNER // HEADS
MCONV = 4
CCONV = 31
EPS = 1e-6
NEG = -1e30
K_SCALE = DH ** -0.5

LANES = 128
SUBLANES = 8
VMEM_LIMIT = 56 * 2 ** 20

ROW_TILE = 256
CHUNK = 256
STEP_BATCH = 8
CONV_ROWS = 64
CONV_COLS = 256
CCONV_HIST = 32
SAMPLE_TILE = 16

f32 = jnp.float32
bf16 = jnp.bfloat16


def _bf(x):
    return x.astype(bf16)


def _dot(a, b):
    return jnp.dot(a, b, preferred_element_type=f32)


def _silu(x):
    return x * jax.nn.sigmoid(x)


def _log_sigmoid(x):
    return jnp.minimum(x, 0.0) - jnp.log1p(jnp.exp(-jnp.abs(x)))


def _rms(x, g):
    return x * lax.rsqrt(jnp.mean(x * x, axis=-1, keepdims=True) + EPS) * g


def _rms_mod(x, g, mod):
    return _rms(x, g) * (1.0 + mod[:, D_MODEL:2 * D_MODEL]) + mod[:, :D_MODEL]


def _layernorm(x, g, b=None):
    mu = jnp.mean(x, axis=-1, keepdims=True)
    xc = x - mu
    var = jnp.mean(xc * xc, axis=-1, keepdims=True)
    y = xc * lax.rsqrt(var + EPS) * g
    return y if b is None else y + b


def _const_spec(shape):
    n = len(shape)
    return pl.BlockSpec(shape, lambda *_: (0,) * n, pipeline_mode=pl.Buffered(1))


def _params(*sem):
    return pltpu.CompilerParams(dimension_semantics=sem, vmem_limit_bytes=VMEM_LIMIT)


def _ada_kernel(c_ref, w_ref, b_ref, o_ref):
    o_ref[...] = _dot(_bf(_silu(c_ref[...])), _bf(w_ref[...])) + b_ref[...]


def _ada(c_all, w_ada, b_ada):
    depth, d, d3 = w_ada.shape
    rows = c_all.shape[0]
    tn = D_MODEL
    return pl.pallas_call(
        _ada_kernel,
        grid=(depth, d3 // tn),
        in_specs=[
            pl.BlockSpec((rows, d), lambda i, j: (0, 0)),
            pl.BlockSpec((None, d, tn), lambda i, j: (i, 0, j)),
            pl.BlockSpec((None, 1, tn), lambda i, j: (i, 0, j)),
        ],
        out_specs=pl.BlockSpec((None, rows, tn), lambda i, j: (i, 0, j)),
        out_shape=jax.ShapeDtypeStruct((depth, rows, d3), f32),
        compiler_params=_params("arbitrary", "arbitrary"),
        name="ada",
    )(c_all, w_ada, b_ada.reshape(depth, 1, d3))


def _ml_qkv(xm, xc, wq_ref, wk_ref, wv_ref, wigf_ref, bigf_ref):
    xcb, xmb = _bf(xc), _bf(xm)
    qs, ks, vs = [], [], []
    for h in range(HEADS):
        seg = slice(h * DH, (h + 1) * DH)
        qs.append(_dot(xcb[:, seg], wq_ref[h]))
        ks.append(_dot(xcb[:, seg], wk_ref[h]))
        vs.append(_dot(xmb[:, seg], wv_ref[h]))
    q = jnp.concatenate(qs, axis=1)
    k = jnp.concatenate(ks, axis=1)
    v = jnp.concatenate(vs, axis=1)
    pre = (_dot(_bf(q), wigf_ref[0:INNER, :]) + _dot(_bf(k), wigf_ref[INNER:2 * INNER, :])
           + _dot(_bf(v), wigf_ref[2 * INNER:3 * INNER, :]) + bigf_ref[...])
    lane = lax.broadcasted_iota(jnp.int32, pre.shape, 1)
    gates = jnp.where(lane >= HEADS, _log_sigmoid(pre), pre)
    return q, k, v, gates


def _ml_back(hh, xc, z, x, mod, skip, wout_ref):
    act = (hh + skip * xc) * _silu(z)
    return x + mod[:, 2 * D_MODEL:] * _dot(_bf(act), wout_ref[...])


def _ml_conv_prompt(xm, wconv_ref, bconv_ref, xbuf_ref, tm):
    xbuf_ref[SUBLANES:SUBLANES + tm, :] = xm
    acc = bconv_ref[...] + wconv_ref[MCONV - 1:MCONV, :] * xm
    for j in range(MCONV - 1):
        start = SUBLANES - (MCONV - 1) + j
        acc = acc + wconv_ref[j:j + 1, :] * xbuf_ref[start:start + tm, :]
    return acc


def _mlstm_chunk(q, ks, v, gates, lng_ref, C_ref, n_ref, m_ref):
    L = q.shape[0]
    row = lax.broadcasted_iota(jnp.int32, (L, L), 0)
    col = lax.broadcasted_iota(jnp.int32, (L, L), 1)
    causal = row >= col
    tri = jnp.where(causal, 1.0, 0.0).astype(bf16)
    hi = _bf(gates)
    lo = _bf(gates - hi.astype(f32))
    bcum = _dot(tri, hi) + _dot(tri, lo)
    gates_t = gates.T
    bcum_t = bcum.T

    hs = []
    for h in range(HEADS):
        seg = slice(h * DH, (h + 1) * DH)
        ig_col = gates[:, h:h + 1]
        b_col = bcum[:, HEADS + h:HEADS + h + 1]
        ig_row = gates_t[h:h + 1, :]
        b_row = bcum_t[HEADS + h:HEADS + h + 1, :]
        m_prev = m_ref[:, h:h + 1]
        qh, kh, vh = q[:, seg], ks[:, seg], v[:, seg]

        dmat = jnp.where(causal, b_col - b_row + ig_row, NEG)
        inter = b_col + m_prev
        m_t = jnp.maximum(inter, jnp.max(dmat, axis=1, keepdims=True))
        w_intra = jnp.exp(dmat - m_t)
        w_inter = jnp.exp(inter - m_t)
        s = lax.dot_general(qh, kh, (((1,), (1,)), ((), ())), preferred_element_type=f32) * w_intra
        num = _dot(_bf(s), vh) + w_inter * _dot(qh, _bf(C_ref[h]))
        qn = jnp.sum(qh.astype(f32) * n_ref[h:h + 1, :], axis=1, keepdims=True)
        den = jnp.sum(s, axis=1, keepdims=True) + w_inter * qn
        hc = num / jnp.maximum(jnp.abs(den), jnp.exp(-m_t))
        hs.append(_layernorm(hc, lng_ref[:, seg]))

        m_new = m_t[L - 1:L, :]
        b_last = b_col[L - 1:L, :]
        decay = jnp.exp(b_last + m_prev - m_new)
        kw = kh.astype(f32) * jnp.exp(b_last - b_col + ig_col - m_new)
        C_ref[h] = decay * C_ref[h] + lax.dot_general(_bf(kw), vh, (((0,), (0,)), ((), ())),
                                                      preferred_element_type=f32)
        n_ref[h:h + 1, :] = decay * n_ref[h:h + 1, :] + jnp.sum(kw, axis=0, keepdims=True)
        m_ref[:, h:h + 1] = m_new
    return jnp.concatenate(hs, axis=1)


def _ml_prompt_kernel(x_ref, mod_ref, g_ref, win_ref, wconv_ref, bconv_ref, wq_ref, wk_ref, wv_ref, wigf_ref,
                      bigf_ref, lng_ref, skip_ref, wout_ref,
                      o_ref, mconv_ref, C_ref, n_ref, m_ref, xbuf_ref, *, tm):
    @pl.when(pl.program_id(1) == 0)
    def _():
        xbuf_ref[0:SUBLANES, :] = jnp.zeros((SUBLANES, INNER), f32)
        C_ref[...] = jnp.zeros_like(C_ref)
        n_ref[...] = jnp.zeros_like(n_ref)
        m_ref[...] = jnp.zeros_like(m_ref)

    x = x_ref[...]
    mod = mod_ref[...]
    xmz = _dot(_bf(_rms_mod(x, g_ref[...], mod)), win_ref[...])
    xm = xmz[:, :INNER]
    z = xmz[:, INNER:]
    xc = _silu(_ml_conv_prompt(xm, wconv_ref, bconv_ref, xbuf_ref, tm))
    mconv_ref[...] = xbuf_ref[SUBLANES + tm - (MCONV - 1):SUBLANES + tm, :]
    xbuf_ref[0:SUBLANES, :] = xbuf_ref[tm:tm + SUBLANES, :]

    q, k, v, gates = _ml_qkv(xm, xc, wq_ref, wk_ref, wv_ref, wigf_ref, bigf_ref)
    hh = _mlstm_chunk(_bf(q), _bf(k * K_SCALE), _bf(v), gates, lng_ref, C_ref, n_ref, m_ref)
    o_ref[...] = _ml_back(hh, xc, z, x, mod, skip_ref[...], wout_ref)


def _ml_prompt(x, mod4, g, w):
    B, T, D = x.shape
    tm = CHUNK
    row = lambda b, t: (b, t, 0)
    return pl.pallas_call(
        functools.partial(_ml_prompt_kernel, tm=tm),
        grid=(B, T // tm),
        in_specs=[
            pl.BlockSpec((None, tm, D), row),
            pl.BlockSpec((None, None, 1, 3 * D), lambda b, t: (0, b, 0, 0)),
            _const_spec((1, D)),
            _const_spec((D, 2 * INNER)),
            _const_spec((MCONV, INNER)),
            _const_spec((1, INNER)),
            _const_spec((HEADS, DH, DH)),
            _const_spec((HEADS, DH, DH)),
            _const_spec((HEADS, DH, DH)),
            _const_spec((3 * INNER, LANES)),
            _const_spec((1, LANES)),
            _const_spec((1, INNER)),
            _const_spec((1, INNER)),
            _const_spec((INNER, D)),
        ],
        out_specs=[
            pl.BlockSpec((None, tm, D), row),
            pl.BlockSpec((None, MCONV - 1, INNER), lambda b, t: (b, 0, 0)),
            pl.BlockSpec((None, HEADS, DH, DH), lambda b, t: (b, 0, 0, 0)),
            pl.BlockSpec((None, HEADS, DH), lambda b, t: (b, 0, 0)),
            pl.BlockSpec((None, 1, LANES), lambda b, t: (b, 0, 0)),
        ],
        out_shape=[
            jax.ShapeDtypeStruct((B, T, D), f32),
            jax.ShapeDtypeStruct((B, MCONV - 1, INNER), f32),
            jax.ShapeDtypeStruct((B, HEADS, DH, DH), f32),
            jax.ShapeDtypeStruct((B, HEADS, DH), f32),
            jax.ShapeDtypeStruct((B, 1, LANES), f32),
        ],
        scratch_shapes=[pltpu.VMEM((SUBLANES + tm, INNER), f32)],
        compiler_params=_params("arbitrary", "arbitrary"),
        name="ml_prompt",
    )(x, mod4, g, w["ml_w_in"], w["ml_w_conv"], w["ml_b_conv"], w["ml_w_q"], w["ml_w_k"], w["ml_w_v"],
      w["ml_w_igf"], w["ml_b_igf"], w["ml_ln_g"], w["ml_skip"], w["ml_w_out"])


def _ml_front_sample_kernel(x_ref, mod_ref, g_ref, win_ref, wconv_ref, bconv_ref, wq_ref, wk_ref, wv_ref,
                            wigf_ref, bigf_ref, st_ref, n_ref, m_ref,
                            q_ref, kw_ref, v_ref, xc_ref, z_ref, nnew_ref, a1_ref, a2_ref, dec_ref, mnew_ref,
                            mconv_ref):
    h = _rms_mod(x_ref[...], g_ref[...], mod_ref[...])
    xmz = _dot(_bf(h), win_ref[...])
    xm = xmz[:, :INNER]
    z_ref[...] = xmz[:, INNER:]
    acc = bconv_ref[...] + wconv_ref[MCONV - 1:MCONV, :] * xm
    for j in range(MCONV - 1):
        acc = acc + wconv_ref[j:j + 1, :] * st_ref[j]
    xc = _silu(acc)
    xc_ref[...] = xc
    mconv_ref[0:MCONV - 2] = st_ref[1:MCONV - 1]
    mconv_ref[MCONV - 2] = xm

    q, k, v, gates = _ml_qkv(xm, xc, wq_ref, wk_ref, wv_ref, wigf_ref, bigf_ref)
    ks = k * K_SCALE
    q_ref[...] = q
    v_ref[...] = v

    ig = gates
    lf = pltpu.roll(gates, LANES - HEADS, axis=1)
    inter = lf + m_ref[...]
    m_new = jnp.maximum(inter, ig)
    w_intra = jnp.exp(ig - m_new)
    w_inter = jnp.exp(inter - m_new)
    lane = lax.broadcasted_iota(jnp.int32, ig.shape, 1)
    qk = jnp.zeros_like(ig)
    qn = jnp.zeros_like(ig)
    for h in range(HEADS):
        seg = slice(h * DH, (h + 1) * DH)
        qk = jnp.where(lane == h, jnp.sum(q[:, seg] * ks[:, seg], axis=1, keepdims=True), qk)
        qn = jnp.where(lane == h, jnp.sum(q[:, seg] * n_ref[:, seg], axis=1, keepdims=True), qn)
        kw = w_intra[:, h:h + 1] * ks[:, seg]
        kw_ref[:, seg] = kw
        nnew_ref[:, seg] = w_inter[:, h:h + 1] * n_ref[:, seg] + kw
    s = qk * w_intra
    den = s + w_inter * qn
    dn = jnp.maximum(jnp.abs(den), jnp.exp(-m_new))
    valid = lane < HEADS
    a1_ref[...] = jnp.where(valid, s / dn, 0.0)
    a2_ref[...] = jnp.where(valid, w_inter / dn, 0.0)
    dec_ref[...] = jnp.where(valid, w_inter, 0.0)
    mnew_ref[...] = jnp.where(valid, m_new, 0.0)


def _ml_front_sample(x, mod, g, w, st, n, m):
    R, D = x.shape
    full = lambda shape: pl.BlockSpec(shape, lambda i: (0,) * len(shape))
    act = jax.ShapeDtypeStruct((R, INNER), f32)
    small = jax.ShapeDtypeStruct((R, LANES), f32)
    return pl.pallas_call(
        _ml_front_sample_kernel,
        grid=(1,),
        in_specs=[
            full((R, D)), full((R, 3 * D)), full((1, D)), full((D, 2 * INNER)), full((MCONV, INNER)),
            full((1, INNER)), full((HEADS, DH, DH)), full((HEADS, DH, DH)), full((HEADS, DH, DH)),
            full((3 * INNER, LANES)), full((1, LANES)),
            full((MCONV - 1, R, INNER)), full((R, INNER)), full((R, LANES)),
        ],
        out_specs=[full((R, INNER))] * 6 + [full((R, LANES))] * 4 + [full((MCONV - 1, R, INNER))],
        out_shape=[act] * 6 + [small] * 4 + [jax.ShapeDtypeStruct((MCONV - 1, R, INNER), f32)],
        compiler_params=_params("arbitrary"),
        name="ml_front_sample",
    )(x, mod, g, w["ml_w_in"], w["ml_w_conv"], w["ml_b_conv"], w["ml_w_q"], w["ml_w_k"], w["ml_w_v"],
      w["ml_w_igf"], w["ml_b_igf"], st, n, m)


def _mlstm_step_kernel(dec_ref, c_ref, q_ref, kw_ref, v_ref, cout_ref, qc_ref, *, bb):
    g = pl.program_id(0)
    h = pl.program_id(1)
    qt = q_ref[...].T
    kwt = kw_ref[...].T
    for i in range(bb):
        decay = dec_ref[(g * bb + i) * HEADS + h]
        c = c_ref[i]
        qc_ref[i:i + 1, :] = jnp.sum(c * qt[:, i:i + 1], axis=0, keepdims=True)
        cout_ref[i] = decay * c + kwt[:, i:i + 1] * v_ref[i:i + 1, :]


def _mlstm_step(dec_flat, C, q, kw, v):
    R = C.shape[0]
    bb = STEP_BATCH
    gh = lambda g, h: (g, h, 0, 0)
    vec = pl.BlockSpec((bb, DH), lambda g, h: (g, h))
    return pl.pallas_call(
        functools.partial(_mlstm_step_kernel, bb=bb),
        grid=(R // bb, HEADS),
        in_specs=[
            pl.BlockSpec(memory_space=pltpu.SMEM),
            pl.BlockSpec((bb, None, DH, DH), gh),
            vec, vec, vec,
        ],
        out_specs=[pl.BlockSpec((bb, None, DH, DH), gh), vec],
        out_shape=[jax.ShapeDtypeStruct(C.shape, f32), jax.ShapeDtypeStruct((R, INNER), f32)],
        compiler_params=_params("arbitrary", "arbitrary"),
        name="mlstm_step",
    )(dec_flat, C, q, kw, v)


def _ml_back_sample_kernel(a1_ref, a2_ref, v_ref, qc_ref, lng_ref, xc_ref, z_ref, x_ref, mod_ref, skip_ref,
                           wout_ref, o_ref):
    hs = []
    for h in range(HEADS):
        seg = slice(h * DH, (h + 1) * DH)
        hc = a1_ref[:, h:h + 1] * v_ref[:, seg] + a2_ref[:, h:h + 1] * qc_ref[:, seg]
        hs.append(_layernorm(hc, lng_ref[:, seg]))
    hh = jnp.concatenate(hs, axis=1)
    o_ref[...] = _ml_back(hh, xc_ref[...], z_ref[...], x_ref[...], mod_ref[...], skip_ref[...], wout_ref)


def _ml_back_sample(a1, a2, v, qc, xc, z, x, mod, w):
    R, D = x.shape
    full = lambda shape: pl.BlockSpec(shape, lambda i: (0,) * len(shape))
    return pl.pallas_call(
        _ml_back_sample_kernel,
        grid=(1,),
        in_specs=[full((R, LANES)), full((R, LANES)), full((R, INNER)), full((R, INNER)), full((1, INNER)),
                  full((R, INNER)), full((R, INNER)), full((R, D)), full((R, 3 * D)), full((1, INNER)),
                  full((INNER, D))],
        out_specs=full((R, D)),
        out_shape=jax.ShapeDtypeStruct((R, D), f32),
        compiler_params=_params("arbitrary"),
        name="ml_back_sample",
    )(a1, a2, v, qc, w["ml_ln_g"], xc, z, x, mod, w["ml_skip"], w["ml_w_out"])


def _cf_in(x, mod, g_ref, win_ref, bin_ref):
    h = _rms_mod(x, g_ref[...], mod)
    agz = _dot(_bf(h), win_ref[...]) + bin_ref[...]
    u = agz[:, :INNER] * jax.nn.sigmoid(agz[:, INNER:2 * INNER])
    return u, agz[:, 2 * INNER:]


def _cf_out(y, z, x, mod, lng_ref, lnb_ref, wout_ref, fg_ref):
    yn = _layernorm(y, lng_ref[...], lnb_ref[...])
    out = _dot(_bf(_silu(yn) * _silu(z)), wout_ref[...])
    return _rms(x + mod[:, 2 * D_MODEL:] * out, fg_ref[...])


def _dwconv_tile(ubuf_ref, sh_ref, wb_ref, bdw_ref, y_ref, tm):
    base = CCONV_HIST - (CCONV - 1)
    max_row_off = (base + CCONV - 2) // SUBLANES * SUBLANES

    def col_body(ci, carry):
        cols = pl.ds(pl.multiple_of(ci * CONV_COLS, CONV_COLS), CONV_COLS)
        for r in range(1, SUBLANES):
            sh_ref[r - 1] = ubuf_ref[r:r + tm + max_row_off, cols]
        b = bdw_ref[:, cols]
        for rc in range(tm // CONV_ROWS):
            acc = jnp.broadcast_to(b, (CONV_ROWS, CONV_COLS))
            for j in range(CCONV):
                a, r = divmod(base + j, SUBLANES)
                rows = pl.ds(SUBLANES * a + CONV_ROWS * rc, CONV_ROWS)
                src = ubuf_ref[rows, cols] if r == 0 else sh_ref[r - 1, rows, :]
                wj = wb_ref[SUBLANES * j:SUBLANES * (j + 1), cols]
                src = src.reshape(CONV_ROWS // SUBLANES, SUBLANES, CONV_COLS)
                acc = acc + (src * wj[None]).reshape(CONV_ROWS, CONV_COLS)
            y_ref[pl.ds(CONV_ROWS * rc, CONV_ROWS), cols] = acc
        return carry

    lax.fori_loop(0, INNER // CONV_COLS, col_body, 0)


def _cf_prompt_kernel(x_ref, mod_ref, g_ref, win_ref, bin_ref, wdw_ref, bdw_ref, lng_ref, lnb_ref, wout_ref,
                      fg_ref, o_ref, cconv_ref, ubuf_ref, y_ref, sh_ref, wb_ref, *, tm):
    @pl.when((pl.program_id(0) == 0) & (pl.program_id(1) == 0))
    def _():
        for j in range(CCONV):
            wb_ref[SUBLANES * j:SUBLANES * (j + 1), :] = jnp.broadcast_to(wdw_ref[j:j + 1, :], (SUBLANES, INNER))

    @pl.when(pl.program_id(1) == 0)
    def _():
        ubuf_ref[0:CCONV_HIST, :] = jnp.zeros((CCONV_HIST, INNER), f32)

    x = x_ref[...]
    mod = mod_ref[...]
    u, z = _cf_in(x, mod, g_ref, win_ref, bin_ref)
    ubuf_ref[CCONV_HIST:CCONV_HIST + tm, :] = u
    _dwconv_tile(ubuf_ref, sh_ref, wb_ref, bdw_ref, y_ref, tm)
    cconv_ref[...] = ubuf_ref[CCONV_HIST + tm - (CCONV - 1):CCONV_HIST + tm, :]
    ubuf_ref[0:CCONV_HIST, :] = ubuf_ref[tm:tm + CCONV_HIST, :]
    o_ref[...] = _cf_out(y_ref[...], z, x, mod, lng_ref, lnb_ref, wout_ref, fg_ref)


def _cf_prompt(x, mod4, g, w, final_g):
    B, T, D = x.shape
    tm = ROW_TILE
    row = lambda b, t: (b, t, 0)
    return pl.pallas_call(
        functools.partial(_cf_prompt_kernel, tm=tm),
        grid=(B, T // tm),
        in_specs=[
            pl.BlockSpec((None, tm, D), row),
            pl.BlockSpec((None, None, 1, 3 * D), lambda b, t: (1, b, 0, 0)),
            _const_spec((1, D)),
            _const_spec((D, 3 * INNER)),
            _const_spec((1, 3 * INNER)),
            _const_spec((CCONV, INNER)),
            _const_spec((1, INNER)),
            _const_spec((1, INNER)),
            _const_spec((1, INNER)),
            _const_spec((INNER, D)),
            _const_spec((1, D)),
        ],
        out_specs=[
            pl.BlockSpec((None, tm, D), row),
            pl.BlockSpec((None, CCONV - 1, INNER), lambda b, t: (b, 0, 0)),
        ],
        out_shape=[
            jax.ShapeDtypeStruct((B, T, D), f32),
            jax.ShapeDtypeStruct((B, CCONV - 1, INNER), f32),
        ],
        scratch_shapes=[
            pltpu.VMEM((CCONV_HIST + tm, INNER), f32),
            pltpu.VMEM((tm, INNER), f32),
            pltpu.VMEM((SUBLANES - 1, tm + CCONV_HIST - SUBLANES, CONV_COLS), f32),
            pltpu.VMEM((SUBLANES * CCONV, INNER), f32),
        ],
        compiler_params=_params("arbitrary", "arbitrary"),
        name="cf_prompt",
    )(x, mod4, g, w["cf_w_in"], w["cf_b_in"], w["cf_w_dw"], w["cf_b_dw"], w["cf_ln_g"], w["cf_ln_b"],
      w["cf_w_out"], final_g)


def _cf_sample_kernel(x_ref, mod_ref, g_ref, win_ref, bin_ref, wdw_ref, bdw_ref, lng_ref, lnb_ref, wout_ref,
                      fg_ref, st_ref, o_ref, stout_ref, u_ref, z_ref, y_ref, *, tb):
    i = pl.program_id(0)

    @pl.when(i == 0)
    def _():
        u, z = _cf_in(x_ref[...], mod_ref[...], g_ref, win_ref, bin_ref)
        u_ref[...] = u
        z_ref[...] = z

    rows = pl.ds(pl.multiple_of(i * tb, tb), tb)
    u = u_ref[rows, :]
    acc = bdw_ref[...] + wdw_ref[CCONV - 1:CCONV, :] * u
    for j in range(CCONV - 1):
        acc = acc + wdw_ref[j:j + 1, :] * st_ref[j]
    y_ref[rows, :] = acc
    stout_ref[0:CCONV - 2] = st_ref[1:CCONV - 1]
    stout_ref[CCONV - 2] = u

    @pl.when(i == pl.num_programs(0) - 1)
    def _():
        o_ref[...] = _cf_out(y_ref[...], z_ref[...], x_ref[...], mod_ref[...], lng_ref, lnb_ref, wout_ref, fg_ref)


def _cf_sample(x, mod, g, w, final_g, st):
    R, D = x.shape
    tb = SAMPLE_TILE
    state = pl.BlockSpec((CCONV - 1, tb, INNER), lambda i: (0, i, 0))
    return pl.pallas_call(
        functools.partial(_cf_sample_kernel, tb=tb),
        grid=(R // tb,),
        in_specs=[
            _const_spec((R, D)),
            _const_spec((R, 3 * D)),
            _const_spec((1, D)),
            _const_spec((D, 3 * INNER)),
            _const_spec((1, 3 * INNER)),
            _const_spec((CCONV, INNER)),
            _const_spec((1, INNER)),
            _const_spec((1, INNER)),
            _const_spec((1, INNER)),
            _const_spec((INNER, D)),
            _const_spec((1, D)),
            state,
        ],
        out_specs=[pl.BlockSpec((R, D), lambda i: (0, 0)), state],
        out_shape=[jax.ShapeDtypeStruct((R, D), f32), jax.ShapeDtypeStruct((CCONV - 1, R, INNER), f32)],
        scratch_shapes=[pltpu.VMEM((R, INNER), f32)] * 3,
        compiler_params=_params("arbitrary"),
        name="cf_sample",
    )(x, mod, g, w["cf_w_in"], w["cf_b_in"], w["cf_w_dw"], w["cf_b_dw"], w["cf_ln_g"], w["cf_ln_b"],
      w["cf_w_out"], final_g, st)


def kernel(x_prompt, x_sample, c_prompt, c_sample, state_mlstm_C, state_mlstm_n, state_mlstm_m, state_mlstm_conv, state_conf_conv, norm_g, w_ada, b_ada, ml_w_in, ml_w_conv, ml_b_conv, ml_w_q, ml_w_k, ml_w_v, ml_w_ig, ml_b_ig, ml_w_fg, ml_b_fg, ml_ln_g, ml_skip, ml_w_out, cf_w_in, cf_b_in, cf_w_dw, cf_b_dw, cf_ln_g, cf_ln_b, cf_w_out, final_g):
    B = x_prompt.shape[0]
    R = x_sample.shape[0]
    D = D_MODEL

    gate_pad = LANES - 2 * HEADS
    w = {
        "ml_w_in": _bf(ml_w_in[0]),
        "ml_w_conv": ml_w_conv[0],
        "ml_b_conv": ml_b_conv,
        "ml_w_q": _bf(ml_w_q[0]),
        "ml_w_k": _bf(ml_w_k[0]),
        "ml_w_v": _bf(ml_w_v[0]),
        "ml_w_igf": _bf(jnp.pad(jnp.concatenate([ml_w_ig[0], ml_w_fg[0]], axis=1), ((0, 0), (0, gate_pad)))),
        "ml_b_igf": jnp.pad(jnp.concatenate([ml_b_ig, ml_b_fg], axis=1), ((0, 0), (0, gate_pad))),
        "ml_ln_g": ml_ln_g.reshape(1, INNER),
        "ml_skip": ml_skip,
        "ml_w_out": _bf(ml_w_out[0]),
        "cf_w_in": _bf(cf_w_in[0]),
        "cf_b_in": cf_b_in,
        "cf_w_dw": cf_w_dw[0],
        "cf_b_dw": cf_b_dw,
        "cf_ln_g": cf_ln_g,
        "cf_ln_b": cf_ln_b,
        "cf_w_out": _bf(cf_w_out[0]),
    }
    final_g2 = final_g.reshape(1, D)

    mod = _ada(jnp.concatenate([c_prompt, c_sample], axis=0), w_ada, b_ada)
    mod4 = mod.reshape(mod.shape[0], B + R, 1, 3 * D)
    mod_s = mod[:, B:]

    x1, mconv_p, C_p, n_p, m_p = _ml_prompt(x_prompt, mod4, norm_g[0:1], w)
    y_prompt, cconv_p = _cf_prompt(x1, mod4, norm_g[1:2], w, final_g2)

    tap_major = lambda a: jnp.transpose(a, (1, 0, 2))
    xs = x_sample.reshape(R, D)
    m_pad = jnp.pad(state_mlstm_m[0], ((0, 0), (0, LANES - HEADS)))
    (q_s, kw_s, v_s, xc_s, z_s, n_s, a1, a2, dec, m_s, mconv_s) = _ml_front_sample(
        xs, mod_s[0], norm_g[0:1], w, tap_major(state_mlstm_conv[0]), state_mlstm_n[0].reshape(R, INNER), m_pad)
    C_s, qc = _mlstm_step(dec[:, :HEADS].reshape(R * HEADS), state_mlstm_C[0], q_s, kw_s, v_s)
    x1_s = _ml_back_sample(a1, a2, v_s, qc, xc_s, z_s, xs, mod_s[0], w)
    y_sample, cconv_s = _cf_sample(x1_s, mod_s[1], norm_g[1:2], w, final_g2, tap_major(state_conf_conv[0]))

    return (
        y_prompt,
        y_sample.reshape(R, 1, D),
        C_p[None],
        C_s[None],
        n_p[None],
        n_s.reshape(1, R, HEADS, DH),
        m_p[:, 0, :HEADS][None],
        m_s[:, :HEADS][None],
        mconv_p[None],
        tap_major(mconv_s)[None],
        cconv_p[None],
        tap_major(cconv_s)[None],
    )
```

```python
import functools

import jax
import jax.numpy as jnp
from jax import lax
from jax.experimental import pallas as pl
from jax.experimental.pallas import tpu as pltpu

D_MODEL = 1024
INNER = 2 * D_MODEL
HEADS = 4
DH = INNER // HEADS
MCONV = 4
CCONV = 31
EPS = 1e-6
NEG = -1e30
K_SCALE = DH ** -0.5

LANES = 128
SUBLANES = 8
VMEM_LIMIT = 56 * 2 ** 20

ROW_TILE = 256
CHUNK = 256
STEP_BATCH = 8
CONV_ROWS = 64
CONV_COLS = 256
CCONV_HIST = 32
SAMPLE_TILE = 16

f32 = jnp.float32
bf16 = jnp.bfloat16


def _bf(x):
    return x.astype(bf16)


def _dot(a, b):
    return jnp.dot(a, b, preferred_element_type=f32)


def _silu(x):
    return x * jax.nn.sigmoid(x)


def _log_sigmoid(x):
    return jnp.minimum(x, 0.0) - jnp.log1p(jnp.exp(-jnp.abs(x)))


def _rms(x, g):
    return x * lax.rsqrt(jnp.mean(x * x, axis=-1, keepdims=True) + EPS) * g


def _rms_mod(x, g, mod):
    return _rms(x, g) * (1.0 + mod[:, D_MODEL:2 * D_MODEL]) + mod[:, :D_MODEL]


def _layernorm(x, g, b=None):
    mu = jnp.mean(x, axis=-1, keepdims=True)
    xc = x - mu
    var = jnp.mean(xc * xc, axis=-1, keepdims=True)
    y = xc * lax.rsqrt(var + EPS) * g
    return y if b is None else y + b


def _const_spec(shape):
    n = len(shape)
    return pl.BlockSpec(shape, lambda *_: (0,) * n, pipeline_mode=pl.Buffered(1))


def _params(*sem):
    return pltpu.CompilerParams(dimension_semantics=sem, vmem_limit_bytes=VMEM_LIMIT)


def _ada_kernel(c_ref, w_ref, b_ref, o_ref):
    o_ref[...] = _dot(_bf(_silu(c_ref[...])), _bf(w_ref[...])) + b_ref[...]


def _ada(c_all, w_ada, b_ada):
    depth, d, d3 = w_ada.shape
    rows = c_all.shape[0]
    tn = D_MODEL
    return pl.pallas_call(
        _ada_kernel,
        grid=(depth, d3 // tn),
        in_specs=[
            pl.BlockSpec((rows, d), lambda i, j: (0, 0)),
            pl.BlockSpec((None, d, tn), lambda i, j: (i, 0, j)),
            pl.BlockSpec((None, 1, tn), lambda i, j: (i, 0, j)),
        ],
        out_specs=pl.BlockSpec((None, rows, tn), lambda i, j: (i, 0, j)),
        out_shape=jax.ShapeDtypeStruct((depth, rows, d3), f32),
        compiler_params=_params("arbitrary", "arbitrary"),
        name="ada",
    )(c_all, w_ada, b_ada.reshape(depth, 1, d3))


def _ml_qkv(xm, xc, wq_ref, wk_ref, wv_ref, wigf_ref, bigf_ref):
    xcb, xmb = _bf(xc), _bf(xm)
    qs, ks, vs = [], [], []
    for h in range(HEADS):
        seg = slice(h * DH, (h + 1) * DH)
        qs.append(_dot(xcb[:, seg], wq_ref[h]))
        ks.append(_dot(xcb[:, seg], wk_ref[h]))
        vs.append(_dot(xmb[:, seg], wv_ref[h]))
    q = jnp.concatenate(qs, axis=1)
    k = jnp.concatenate(ks, axis=1)
    v = jnp.concatenate(vs, axis=1)
    pre = (_dot(_bf(q), wigf_ref[0:INNER, :]) + _dot(_bf(k), wigf_ref[INNER:2 * INNER, :])
           + _dot(_bf(v), wigf_ref[2 * INNER:3 * INNER, :]) + bigf_ref[...])
    lane = lax.broadcasted_iota(jnp.int32, pre.shape, 1)
    gates = jnp.where(lane >= HEADS, _log_sigmoid(pre), pre)
    return q, k, v, gates


def _ml_back(hh, xc, z, x, mod, skip, wout_ref):
    act = (hh + skip * xc) * _silu(z)
    return x + mod[:, 2 * D_MODEL:] * _dot(_bf(act), wout_ref[...])


def _ml_conv_prompt(xm, wconv_ref, bconv_ref, xbuf_ref, tm):
    xbuf_ref[SUBLANES:SUBLANES + tm, :] = xm
    acc = bconv_ref[...] + wconv_ref[MCONV - 1:MCONV, :] * xm
    for j in range(MCONV - 1):
        start = SUBLANES - (MCONV - 1) + j
        acc = acc + wconv_ref[j:j + 1, :] * xbuf_ref[start:start + tm, :]
    return acc


def _mlstm_chunk(q, ks, v, gates, lng_ref, C_ref, n_ref, m_ref):
    L = q.shape[0]
    row = lax.broadcasted_iota(jnp.int32, (L, L), 0)
    col = lax.broadcasted_iota(jnp.int32, (L, L), 1)
    causal = row >= col
    tri = jnp.where(causal, 1.0, 0.0).astype(bf16)
    hi = _bf(gates)
    lo = _bf(gates - hi.astype(f32))
    bcum = _dot(tri, hi) + _dot(tri, lo)
    gates_t = gates.T
    bcum_t = bcum.T

    hs = []
    for h in range(HEADS):
        seg = slice(h * DH, (h + 1) * DH)
        ig_col = gates[:, h:h + 1]
        b_col = bcum[:, HEADS + h:HEADS + h + 1]
        ig_row = gates_t[h:h + 1, :]
        b_row = bcum_t[HEADS + h:HEADS + h + 1, :]
        m_prev = m_ref[:, h:h + 1]
        qh, kh, vh = q[:, seg], ks[:, seg], v[:, seg]

        dmat = jnp.where(causal, b_col - b_row + ig_row, NEG)
        inter = b_col + m_prev
        m_t = jnp.maximum(inter, jnp.max(dmat, axis=1, keepdims=True))
        w_intra = jnp.exp(dmat - m_t)
        w_inter = jnp.exp(inter - m_t)
        s = lax.dot_general(qh, kh, (((1,), (1,)), ((), ())), preferred_element_type=f32) * w_intra
        num = _dot(_bf(s), vh) + w_inter * _dot(qh, _bf(C_ref[h]))
        qn = jnp.sum(qh.astype(f32) * n_ref[h:h + 1, :], axis=1, keepdims=True)
        den = jnp.sum(s, axis=1, keepdims=True) + w_inter * qn
        hc = num / jnp.maximum(jnp.abs(den), jnp.exp(-m_t))
        hs.append(_layernorm(hc, lng_ref[:, seg]))

        m_new = m_t[L - 1:L, :]
        b_last = b_col[L - 1:L, :]
        decay = jnp.exp(b_last + m_prev - m_new)
        kw = kh.astype(f32) * jnp.exp(b_last - b_col + ig_col - m_new)
        C_ref[h] = decay * C_ref[h] + lax.dot_general(_bf(kw), vh, (((0,), (0,)), ((), ())),
                                                      preferred_element_type=f32)
        n_ref[h:h + 1, :] = decay * n_ref[h:h + 1, :] + jnp.sum(kw, axis=0, keepdims=True)
        m_ref[:, h:h + 1] = m_new
    return jnp.concatenate(hs, axis=1)


class _MemoryStream:
    def __init__(self, step, cin_hbm, cout_hbm, inbuf, outbuf, insem, outsem):
        self.head = step % HEADS
        self.first = (step // HEADS) * STEP_BATCH
        self.cin_hbm, self.cout_hbm = cin_hbm, cout_hbm
        self.inbuf, self.outbuf, self.insem, self.outsem = inbuf, outbuf, insem, outsem

    def in_copy(self, i):
        return pltpu.make_async_copy(self.cin_hbm.at[self.first + i, self.head], self.inbuf.at[i % 2],
                                     self.insem.at[i % 2])

    def out_copy(self, i):
        return pltpu.make_async_copy(self.outbuf.at[i % 2], self.cout_hbm.at[self.first + i, self.head],
                                     self.outsem.at[i % 2])

    def update(self, dec_ref, q_ref, kw_ref, v_ref, qc_ref):
        qt = q_ref[...].T
        kwt = kw_ref[...].T
        for i in range(STEP_BATCH):
            if i + 1 < STEP_BATCH:
                self.in_copy(i + 1).start()
            self.in_copy(i).wait()
            if i >= 2:
                self.out_copy(i - 2).wait()
            c = self.inbuf[i % 2]
            decay = dec_ref[(self.first + i) * HEADS + self.head]
            qc_ref[i:i + 1, :] = jnp.sum(c * qt[:, i:i + 1], axis=0, keepdims=True)
            self.outbuf[i % 2] = decay * c + kwt[:, i:i + 1] * v_ref[i:i + 1, :]
            self.out_copy(i).start()
        self.out_copy(STEP_BATCH - 2).wait()
        self.out_copy(STEP_BATCH - 1).wait()


def _ml_prompt_kernel(x_ref, mod_ref, g_ref, win_ref, wconv_ref, bconv_ref, wq_ref, wk_ref, wv_ref, wigf_ref,
                      bigf_ref, lng_ref, skip_ref, wout_ref, dec_ref, qs_ref, kws_ref, vs_ref, cin_hbm,
                      o_ref, mconv_ref, C_ref, n_ref, m_ref, qc_ref, cout_hbm,
                      xbuf_ref, inbuf, outbuf, insem, outsem, *, tm):
    step = pl.program_id(0) * pl.num_programs(1) + pl.program_id(1)
    stream = _MemoryStream(step, cin_hbm, cout_hbm, inbuf, outbuf, insem, outsem)
    stream.in_copy(0).start()

    @pl.when(pl.program_id(1) == 0)
    def _():
        xbuf_ref[0:SUBLANES, :] = jnp.zeros((SUBLANES, INNER), f32)
        C_ref[...] = jnp.zeros_like(C_ref)
        n_ref[...] = jnp.zeros_like(n_ref)
        m_ref[...] = jnp.zeros_like(m_ref)

    x = x_ref[...]
    mod = mod_ref[...]
    xmz = _dot(_bf(_rms_mod(x, g_ref[...], mod)), win_ref[...])
    xm = xmz[:, :INNER]
    z = xmz[:, INNER:]
    xc = _silu(_ml_conv_prompt(xm, wconv_ref, bconv_ref, xbuf_ref, tm))
    mconv_ref[...] = xbuf_ref[SUBLANES + tm - (MCONV - 1):SUBLANES + tm, :]
    xbuf_ref[0:SUBLANES, :] = xbuf_ref[tm:tm + SUBLANES, :]

    q, k, v, gates = _ml_qkv(xm, xc, wq_ref, wk_ref, wv_ref, wigf_ref, bigf_ref)
    hh = _mlstm_chunk(_bf(q), _bf(k * K_SCALE), _bf(v), gates, lng_ref, C_ref, n_ref, m_ref)
    o_ref[...] = _ml_back(hh, xc, z, x, mod, skip_ref[...], wout_ref)

    stream.update(dec_ref, qs_ref, kws_ref, vs_ref, qc_ref)


def _ml_prompt(x, mod4, g, w, dec_flat, C_s, q_s, kw_s, v_s):
    B, T, D = x.shape
    R = C_s.shape[0]
    tm = CHUNK
    steps_t = T // tm
    assert B * steps_t * STEP_BATCH == R * HEADS, "one (sequence group, head) memory block set per grid step"
    row = lambda b, t: (b, t, 0)
    vec = pl.BlockSpec((STEP_BATCH, DH), lambda b, t: ((b * steps_t + t) // HEADS, (b * steps_t + t) % HEADS))
    return pl.pallas_call(
        functools.partial(_ml_prompt_kernel, tm=tm),
        grid=(B, steps_t),
        in_specs=[
            pl.BlockSpec((None, tm, D), row),
            pl.BlockSpec((None, None, 1, 3 * D), lambda b, t: (0, b, 0, 0)),
            _const_spec((1, D)),
            _const_spec((D, 2 * INNER)),
            _const_spec((MCONV, INNER)),
            _const_spec((1, INNER)),
            _const_spec((HEADS, DH, DH)),
            _const_spec((HEADS, DH, DH)),
            _const_spec((HEADS, DH, DH)),
            _const_spec((3 * INNER, LANES)),
            _const_spec((1, LANES)),
            _const_spec((1, INNER)),
            _const_spec((1, INNER)),
            _const_spec((INNER, D)),
            pl.BlockSpec(memory_space=pltpu.SMEM),
            vec, vec, vec,
            pl.BlockSpec(memory_space=pl.ANY),
        ],
        out_specs=[
            pl.BlockSpec((None, tm, D), row),
            pl.BlockSpec((None, MCONV - 1, INNER), lambda b, t: (b, 0, 0)),
            pl.BlockSpec((None, HEADS, DH, DH), lambda b, t: (b, 0, 0, 0)),
            pl.BlockSpec((None, HEADS, DH), lambda b, t: (b, 0, 0)),
            pl.BlockSpec((None, 1, LANES), lambda b, t: (b, 0, 0)),
            vec,
            pl.BlockSpec(memory_space=pl.ANY),
        ],
        out_shape=[
            jax.ShapeDtypeStruct((B, T, D), f32),
            jax.ShapeDtypeStruct((B, MCONV - 1, INNER), f32),
            jax.ShapeDtypeStruct((B, HEADS, DH, DH), f32),
            jax.ShapeDtypeStruct((B, HEADS, DH), f32),
            jax.ShapeDtypeStruct((B, 1, LANES), f32),
            jax.ShapeDtypeStruct((R, INNER), f32),
            jax.ShapeDtypeStruct(C_s.shape, f32),
        ],
        scratch_shapes=[
            pltpu.VMEM((SUBLANES + tm, INNER), f32),
            pltpu.VMEM((2, DH, DH), f32),
            pltpu.VMEM((2, DH, DH), f32),
            pltpu.SemaphoreType.DMA((2,)),
            pltpu.SemaphoreType.DMA((2,)),
        ],
        compiler_params=_params("arbitrary", "arbitrary"),
        name="ml_prompt",
    )(x, mod4, g, w["ml_w_in"], w["ml_w_conv"], w["ml_b_conv"], w["ml_w_q"], w["ml_w_k"], w["ml_w_v"],
      w["ml_w_igf"], w["ml_b_igf"], w["ml_ln_g"], w["ml_skip"], w["ml_w_out"], dec_flat, q_s, kw_s, v_s, C_s)


def _ml_front_sample_kernel(x_ref, mod_ref, g_ref, win_ref, wconv_ref, bconv_ref, wq_ref, wk_ref, wv_ref,
                            wigf_ref, bigf_ref, st_ref, n_ref, m_ref,
                            q_ref, kw_ref, v_ref, xc_ref, z_ref, nnew_ref, a1_ref, a2_ref, dec_ref, mnew_ref,
                            mconv_ref):
    h = _rms_mod(x_ref[...], g_ref[...], mod_ref[...])
    xmz = _dot(_bf(h), win_ref[...])
    xm = xmz[:, :INNER]
    z_ref[...] = xmz[:, INNER:]
    acc = bconv_ref[...] + wconv_ref[MCONV - 1:MCONV, :] * xm
    for j in range(MCONV - 1):
        acc = acc + wconv_ref[j:j + 1, :] * st_ref[j]
    xc = _silu(acc)
    xc_ref[...] = xc
    mconv_ref[0:MCONV - 2] = st_ref[1:MCONV - 1]
    mconv_ref[MCONV - 2] = xm

    q, k, v, gates = _ml_qkv(xm, xc, wq_ref, wk_ref, wv_ref, wigf_ref, bigf_ref)
    ks = k * K_SCALE
    q_ref[...] = q
    v_ref[...] = v

    ig = gates
    lf = pltpu.roll(gates, LANES - HEADS, axis=1)
    inter = lf + m_ref[...]
    m_new = jnp.maximum(inter, ig)
    w_intra = jnp.exp(ig - m_new)
    w_inter = jnp.exp(inter - m_new)
    lane = lax.broadcasted_iota(jnp.int32, ig.shape, 1)
    qk = jnp.zeros_like(ig)
    qn = jnp.zeros_like(ig)
    for h in range(HEADS):
        seg = slice(h * DH, (h + 1) * DH)
        qk = jnp.where(lane == h, jnp.sum(q[:, seg] * ks[:, seg], axis=1, keepdims=True), qk)
        qn = jnp.where(lane == h, jnp.sum(q[:, seg] * n_ref[:, seg], axis=1, keepdims=True), qn)
        kw = w_intra[:, h:h + 1] * ks[:, seg]
        kw_ref[:, seg] = kw
        nnew_ref[:, seg] = w_inter[:, h:h + 1] * n_ref[:, seg] + kw
    s = qk * w_intra
    den = s + w_inter * qn
    dn = jnp.maximum(jnp.abs(den), jnp.exp(-m_new))
    valid = lane < HEADS
    a1_ref[...] = jnp.where(valid, s / dn, 0.0)
    a2_ref[...] = jnp.where(valid, w_inter / dn, 0.0)
    dec_ref[...] = jnp.where(valid, w_inter, 0.0)
    mnew_ref[...] = jnp.where(valid, m_new, 0.0)


def _ml_front_sample(x, mod, g, w, st, n, m):
    R, D = x.shape
    full = lambda shape: pl.BlockSpec(shape, lambda i: (0,) * len(shape))
    act = jax.ShapeDtypeStruct((R, INNER), f32)
    small = jax.ShapeDtypeStruct((R, LANES), f32)
    return pl.pallas_call(
        _ml_front_sample_kernel,
        grid=(1,),
        in_specs=[
            full((R, D)), full((R, 3 * D)), full((1, D)), full((D, 2 * INNER)), full((MCONV, INNER)),
            full((1, INNER)), full((HEADS, DH, DH)), full((HEADS, DH, DH)), full((HEADS, DH, DH)),
            full((3 * INNER, LANES)), full((1, LANES)),
            full((MCONV - 1, R, INNER)), full((R, INNER)), full((R, LANES)),
        ],
        out_specs=[full((R, INNER))] * 6 + [full((R, LANES))] * 4 + [full((MCONV - 1, R, INNER))],
        out_shape=[act] * 6 + [small] * 4 + [jax.ShapeDtypeStruct((MCONV - 1, R, INNER), f32)],
        compiler_params=_params("arbitrary"),
        name="ml_front_sample",
    )(x, mod, g, w["ml_w_in"], w["ml_w_conv"], w["ml_b_conv"], w["ml_w_q"], w["ml_w_k"], w["ml_w_v"],
      w["ml_w_igf"], w["ml_b_igf"], st, n, m)


def _ml_back_sample_kernel(a1_ref, a2_ref, v_ref, qc_ref, lng_ref, xc_ref, z_ref, x_ref, mod_ref, skip_ref,
                           wout_ref, o_ref):
    hs = []
    for h in range(HEADS):
        seg = slice(h * DH, (h + 1) * DH)
        hc = a1_ref[:, h:h + 1] * v_ref[:, seg] + a2_ref[:, h:h + 1] * qc_ref[:, seg]
        hs.append(_layernorm(hc, lng_ref[:, seg]))
    hh = jnp.concatenate(hs, axis=1)
    o_ref[...] = _ml_back(hh, xc_ref[...], z_ref[...], x_ref[...], mod_ref[...], skip_ref[...], wout_ref)


def _ml_back_sample(a1, a2, v, qc, xc, z, x, mod, w):
    R, D = x.shape
    full = lambda shape: pl.BlockSpec(shape, lambda i: (0,) * len(shape))
    return pl.pallas_call(
        _ml_back_sample_kernel,
        grid=(1,),
        in_specs=[full((R, LANES)), full((R, LANES)), full((R, INNER)), full((R, INNER)), full((1, INNER)),
                  full((R, INNER)), full((R, INNER)), full((R, D)), full((R, 3 * D)), full((1, INNER)),
                  full((INNER, D))],
        out_specs=full((R, D)),
        out_shape=jax.ShapeDtypeStruct((R, D), f32),
        compiler_params=_params("arbitrary"),
        name="ml_back_sample",
    )(a1, a2, v, qc, w["ml_ln_g"], xc, z, x, mod, w["ml_skip"], w["ml_w_out"])


def _cf_in(x, mod, g_ref, win_ref, bin_ref):
    h = _rms_mod(x, g_ref[...], mod)
    agz = _dot(_bf(h), win_ref[...]) + bin_ref[...]
    u = agz[:, :INNER] * jax.nn.sigmoid(agz[:, INNER:2 * INNER])
    return u, agz[:, 2 * INNER:]


def _cf_out(y, z, x, mod, lng_ref, lnb_ref, wout_ref, fg_ref):
    yn = _layernorm(y, lng_ref[...], lnb_ref[...])
    out = _dot(_bf(_silu(yn) * _silu(z)), wout_ref[...])
    return _rms(x + mod[:, 2 * D_MODEL:] * out, fg_ref[...])


def _dwconv_tile(ubuf_ref, sh_ref, wb_ref, bdw_ref, y_ref, tm):
    base = CCONV_HIST - (CCONV - 1)
    max_row_off = (base + CCONV - 2) // SUBLANES * SUBLANES

    def col_body(ci, carry):
        cols = pl.ds(pl.multiple_of(ci * CONV_COLS, CONV_COLS), CONV_COLS)
        for r in range(1, SUBLANES):
            sh_ref[r - 1] = ubuf_ref[r:r + tm + max_row_off, cols]
        b = bdw_ref[:, cols]
        for rc in range(tm // CONV_ROWS):
            acc = jnp.broadcast_to(b, (CONV_ROWS, CONV_COLS))
            for j in range(CCONV):
                a, r = divmod(base + j, SUBLANES)
                rows = pl.ds(SUBLANES * a + CONV_ROWS * rc, CONV_ROWS)
                src = ubuf_ref[rows, cols] if r == 0 else sh_ref[r - 1, rows, :]
                wj = wb_ref[SUBLANES * j:SUBLANES * (j + 1), cols]
                src = src.reshape(CONV_ROWS // SUBLANES, SUBLANES, CONV_COLS)
                acc = acc + (src * wj[None]).reshape(CONV_ROWS, CONV_COLS)
            y_ref[pl.ds(CONV_ROWS * rc, CONV_ROWS), cols] = acc
        return carry

    lax.fori_loop(0, INNER // CONV_COLS, col_body, 0)


def _cf_prompt_kernel(x_ref, mod_ref, g_ref, win_ref, bin_ref, wdw_ref, bdw_ref, lng_ref, lnb_ref, wout_ref,
                      fg_ref, o_ref, cconv_ref, ubuf_ref, y_ref, sh_ref, wb_ref, *, tm):
    @pl.when((pl.program_id(0) == 0) & (pl.program_id(1) == 0))
    def _():
        for j in range(CCONV):
            wb_ref[SUBLANES * j:SUBLANES * (j + 1), :] = jnp.broadcast_to(wdw_ref[j:j + 1, :], (SUBLANES, INNER))

    @pl.when(pl.program_id(1) == 0)
    def _():
        ubuf_ref[0:CCONV_HIST, :] = jnp.zeros((CCONV_HIST, INNER), f32)

    x = x_ref[...]
    mod = mod_ref[...]
    u, z = _cf_in(x, mod, g_ref, win_ref, bin_ref)
    ubuf_ref[CCONV_HIST:CCONV_HIST + tm, :] = u
    _dwconv_tile(ubuf_ref, sh_ref, wb_ref, bdw_ref, y_ref, tm)
    cconv_ref[...] = ubuf_ref[CCONV_HIST + tm - (CCONV - 1):CCONV_HIST + tm, :]
    ubuf_ref[0:CCONV_HIST, :] = ubuf_ref[tm:tm + CCONV_HIST, :]
    o_ref[...] = _cf_out(y_ref[...], z, x, mod, lng_ref, lnb_ref, wout_ref, fg_ref)


def _cf_prompt(x, mod4, g, w, final_g):
    B, T, D = x.shape
    tm = ROW_TILE
    row = lambda b, t: (b, t, 0)
    return pl.pallas_call(
        functools.partial(_cf_prompt_kernel, tm=tm),
        grid=(B, T // tm),
        in_specs=[
            pl.BlockSpec((None, tm, D), row),
            pl.BlockSpec((None, None, 1, 3 * D), lambda b, t: (1, b, 0, 0)),
            _const_spec((1, D)),
            _const_spec((D, 3 * INNER)),
            _const_spec((1, 3 * INNER)),
            _const_spec((CCONV, INNER)),
            _const_spec((1, INNER)),
            _const_spec((1, INNER)),
            _const_spec((1, INNER)),
            _const_spec((INNER, D)),
            _const_spec((1, D)),
        ],
        out_specs=[
            pl.BlockSpec((None, tm, D), row),
            pl.BlockSpec((None, CCONV - 1, INNER), lambda b, t: (b, 0, 0)),
        ],
        out_shape=[
            jax.ShapeDtypeStruct((B, T, D), f32),
            jax.ShapeDtypeStruct((B, CCONV - 1, INNER), f32),
        ],
        scratch_shapes=[
            pltpu.VMEM((CCONV_HIST + tm, INNER), f32),
            pltpu.VMEM((tm, INNER), f32),
            pltpu.VMEM((SUBLANES - 1, tm + CCONV_HIST - SUBLANES, CONV_COLS), f32),
            pltpu.VMEM((SUBLANES * CCONV, INNER), f32),
        ],
        compiler_params=_params("arbitrary", "arbitrary"),
        name="cf_prompt",
    )(x, mod4, g, w["cf_w_in"], w["cf_b_in"], w["cf_w_dw"], w["cf_b_dw"], w["cf_ln_g"], w["cf_ln_b"],
      w["cf_w_out"], final_g)


def _cf_sample_kernel(x_ref, mod_ref, g_ref, win_ref, bin_ref, wdw_ref, bdw_ref, lng_ref, lnb_ref, wout_ref,
                      fg_ref, st_ref, o_ref, stout_ref, u_ref, z_ref, y_ref, *, tb):
    i = pl.program_id(0)

    @pl.when(i == 0)
    def _():
        u, z = _cf_in(x_ref[...], mod_ref[...], g_ref, win_ref, bin_ref)
        u_ref[...] = u
        z_ref[...] = z

    rows = pl.ds(pl.multiple_of(i * tb, tb), tb)
    u = u_ref[rows, :]
    acc = bdw_ref[...] + wdw_ref[CCONV - 1:CCONV, :] * u
    for j in range(CCONV - 1):
        acc = acc + wdw_ref[j:j + 1, :] * st_ref[j]
    y_ref[rows, :] = acc
    stout_ref[0:CCONV - 2] = st_ref[1:CCONV - 1]
    stout_ref[CCONV - 2] = u

    @pl.when(i == pl.num_programs(0) - 1)
    def _():
        o_ref[...] = _cf_out(y_ref[...], z_ref[...], x_ref[...], mod_ref[...], lng_ref, lnb_ref, wout_ref, fg_ref)


def _cf_sample(x, mod, g, w, final_g, st):
    R, D = x.shape
    tb = SAMPLE_TILE
    state = pl.BlockSpec((CCONV - 1, tb, INNER), lambda i: (0, i, 0))
    return pl.pallas_call(
        functools.partial(_cf_sample_kernel, tb=tb),
        grid=(R // tb,),
        in_specs=[
            _const_spec((R, D)),
            _const_spec((R, 3 * D)),
            _const_spec((1, D)),
            _const_spec((D, 3 * INNER)),
            _const_spec((1, 3 * INNER)),
            _const_spec((CCONV, INNER)),
            _const_spec((1, INNER)),
            _const_spec((1, INNER)),
            _const_spec((1, INNER)),
            _const_spec((INNER, D)),
            _const_spec((1, D)),
            state,
        ],
        out_specs=[pl.BlockSpec((R, D), lambda i: (0, 0)), state],
        out_shape=[jax.ShapeDtypeStruct((R, D), f32), jax.ShapeDtypeStruct((CCONV - 1, R, INNER), f32)],
        scratch_shapes=[pltpu.VMEM((R, INNER), f32)] * 3,
        compiler_params=_params("arbitrary"),
        name="cf_sample",
    )(x, mod, g, w["cf_w_in"], w["cf_b_in"], w["cf_w_dw"], w["cf_b_dw"], w["cf_ln_g"], w["cf_ln_b"],
      w["cf_w_out"], final_g, st)


def kernel(x_prompt, x_sample, c_prompt, c_sample, state_mlstm_C, state_mlstm_n, state_mlstm_m, state_mlstm_conv, state_conf_conv, norm_g, w_ada, b_ada, ml_w_in, ml_w_conv, ml_b_conv, ml_w_q, ml_w_k, ml_w_v, ml_w_ig, ml_b_ig, ml_w_fg, ml_b_fg, ml_ln_g, ml_skip, ml_w_out, cf_w_in, cf_b_in, cf_w_dw, cf_b_dw, cf_ln_g, cf_ln_b, cf_w_out, final_g):
    B = x_prompt.shape[0]
    R = x_sample.shape[0]
    D = D_MODEL

    gate_pad = LANES - 2 * HEADS
    w = {
        "ml_w_in": _bf(ml_w_in[0]),
        "ml_w_conv": ml_w_conv[0],
        "ml_b_conv": ml_b_conv,
        "ml_w_q": _bf(ml_w_q[0]),
        "ml_w_k": _bf(ml_w_k[0]),
        "ml_w_v": _bf(ml_w_v[0]),
        "ml_w_igf": _bf(jnp.pad(jnp.concatenate([ml_w_ig[0], ml_w_fg[0]], axis=1), ((0, 0), (0, gate_pad)))),
        "ml_b_igf": jnp.pad(jnp.concatenate([ml_b_ig, ml_b_fg], axis=1), ((0, 0), (0, gate_pad))),
        "ml_ln_g": ml_ln_g.reshape(1, INNER),
        "ml_skip": ml_skip,
        "ml_w_out": _bf(ml_w_out[0]),
        "cf_w_in": _bf(cf_w_in[0]),
        "cf_b_in": cf_b_in,
        "cf_w_dw": cf_w_dw[0],
        "cf_b_dw": cf_b_dw,
        "cf_ln_g": cf_ln_g,
        "cf_ln_b": cf_ln_b,
        "cf_w_out": _bf(cf_w_out[0]),
    }
    final_g2 = final_g.reshape(1, D)

    mod = _ada(jnp.concatenate([c_prompt, c_sample], axis=0), w_ada, b_ada)
    mod4 = mod.reshape(mod.shape[0], B + R, 1, 3 * D)
    mod_s = mod[:, B:]

    tap_major = lambda a: jnp.transpose(a, (1, 0, 2))
    xs = x_sample.reshape(R, D)
    m_pad = jnp.pad(state_mlstm_m[0], ((0, 0), (0, LANES - HEADS)))
    (q_s, kw_s, v_s, xc_s, z_s, n_s, a1, a2, dec, m_s, mconv_s) = _ml_front_sample(
        xs, mod_s[0], norm_g[0:1], w, tap_major(state_mlstm_conv[0]), state_mlstm_n[0].reshape(R, INNER), m_pad)

    x1, mconv_p, C_p, n_p, m_p, qc, C_s = _ml_prompt(
        x_prompt, mod4, norm_g[0:1], w, dec[:, :HEADS].reshape(R * HEADS), state_mlstm_C[0], q_s, kw_s, v_s)
    y_prompt, cconv_p = _cf_prompt(x1, mod4, norm_g[1:2], w, final_g2)

    x1_s = _ml_back_sample(a1, a2, v_s, qc, xc_s, z_s, xs, mod_s[0], w)
    y_sample, cconv_s = _cf_sample(x1_s, mod_s[1], norm_g[1:2], w, final_g2, tap_major(state_conf_conv[0]))

    return (
        y_prompt,
        y_sample.reshape(R, 1, D),
        C_p[None],
        C_s[None],
        n_p[None],
        n_s.reshape(1, R, HEADS, DH),
        m_p[:, 0, :HEADS][None],
        m_s[:, :HEADS][None],
        mconv_p[None],
        tap_major(mconv_s)[None],
        cconv_p[None],
        tap_major(cconv_s)[None],
    )
```

```python
import functools

import jax
import jax.numpy as jnp
from jax import lax
from jax.experimental import pallas as pl
from jax.experimental.pallas import tpu as pltpu

D_MODEL = 1024
INNER = 2 * D_MODEL
HEADS = 4
DH = INNER // HEADS
MCONV = 4
CCONV = 31
EPS = 1e-6
NEG = -1e30
K_SCALE = DH ** -0.5

LANES = 128
SUBLANES = 8
VMEM_LIMIT = 56 * 2 ** 20

ROW_TILE = 128
CHUNK = 256
STEP_BATCH = 8
CONV_ROWS = 64
CONV_COLS = 256
CCONV_HIST = 32
SAMPLE_TILE = 16

f32 = jnp.float32
bf16 = jnp.bfloat16


def _bf(x):
    return x.astype(bf16)


def _dot(a, b):
    return jnp.dot(a, b, preferred_element_type=f32)


def _silu(x):
    return x * jax.nn.sigmoid(x)


def _log_sigmoid(x):
    return jnp.minimum(x, 0.0) - jnp.log1p(jnp.exp(-jnp.abs(x)))


def _rms(x, g):
    return x * lax.rsqrt(jnp.mean(x * x, axis=-1, keepdims=True) + EPS) * g


def _rms_mod(x, g, mod):
    return _rms(x, g) * (1.0 + mod[:, D_MODEL:2 * D_MODEL]) + mod[:, :D_MODEL]


def _layernorm(x, g, b=None):
    mu = jnp.mean(x, axis=-1, keepdims=True)
    xc = x - mu
    var = jnp.mean(xc * xc, axis=-1, keepdims=True)
    y = xc * lax.rsqrt(var + EPS) * g
    return y if b is None else y + b


def _const_spec(shape):
    n = len(shape)
    return pl.BlockSpec(shape, lambda *_: (0,) * n, pipeline_mode=pl.Buffered(1))


def _params(*sem):
    return pltpu.CompilerParams(dimension_semantics=sem, vmem_limit_bytes=VMEM_LIMIT)


def _ada_kernel(c_ref, w_ref, b_ref, o_ref):
    o_ref[...] = _dot(_bf(_silu(c_ref[...])), _bf(w_ref[...])) + b_ref[...]


def _ada(c_all, w_ada, b_ada):
    depth, d, d3 = w_ada.shape
    rows = c_all.shape[0]
    tn = D_MODEL
    return pl.pallas_call(
        _ada_kernel,
        grid=(depth, d3 // tn),
        in_specs=[
            pl.BlockSpec((rows, d), lambda i, j: (0, 0)),
            pl.BlockSpec((None, d, tn), lambda i, j: (i, 0, j)),
            pl.BlockSpec((None, 1, tn), lambda i, j: (i, 0, j)),
        ],
        out_specs=pl.BlockSpec((None, rows, tn), lambda i, j: (i, 0, j)),
        out_shape=jax.ShapeDtypeStruct((depth, rows, d3), f32),
        compiler_params=_params("arbitrary", "arbitrary"),
        name="ada",
    )(c_all, w_ada, b_ada.reshape(depth, 1, d3))


def _ml_qkv(xm, xc, wq_ref, wk_ref, wv_ref, wigf_ref, bigf_ref):
    xcb, xmb = _bf(xc), _bf(xm)
    qs, ks, vs = [], [], []
    for h in range(HEADS):
        seg = slice(h * DH, (h + 1) * DH)
        qs.append(_dot(xcb[:, seg], wq_ref[h]))
        ks.append(_dot(xcb[:, seg], wk_ref[h]))
        vs.append(_dot(xmb[:, seg], wv_ref[h]))
    q = jnp.concatenate(qs, axis=1)
    k = jnp.concatenate(ks, axis=1)
    v = jnp.concatenate(vs, axis=1)
    pre = (_dot(_bf(q), wigf_ref[0:INNER, :]) + _dot(_bf(k), wigf_ref[INNER:2 * INNER, :])
           + _dot(_bf(v), wigf_ref[2 * INNER:3 * INNER, :]) + bigf_ref[...])
    lane = lax.broadcasted_iota(jnp.int32, pre.shape, 1)
    gates = jnp.where(lane >= HEADS, _log_sigmoid(pre), pre)
    return q, k, v, gates


def _ml_back(hh, xc, z, x, mod, skip, wout_ref):
    act = (hh + skip * xc) * _silu(z)
    return x + mod[:, 2 * D_MODEL:] * _dot(_bf(act), wout_ref[...])


def _ml_conv_prompt(xm, wconv_ref, bconv_ref, xbuf_ref, tm):
    xbuf_ref[SUBLANES:SUBLANES + tm, :] = xm
    acc = bconv_ref[...] + wconv_ref[MCONV - 1:MCONV, :] * xm
    for j in range(MCONV - 1):
        start = SUBLANES - (MCONV - 1) + j
        acc = acc + wconv_ref[j:j + 1, :] * xbuf_ref[start:start + tm, :]
    return acc


def _mlstm_chunk(q, ks, v, gates, lng_ref, C_ref, n_ref, m_ref):
    L = q.shape[0]
    row = lax.broadcasted_iota(jnp.int32, (L, L), 0)
    col = lax.broadcasted_iota(jnp.int32, (L, L), 1)
    causal = row >= col
    tri = jnp.where(causal, 1.0, 0.0).astype(bf16)
    hi = _bf(gates)
    lo = _bf(gates - hi.astype(f32))
    bcum = _dot(tri, hi) + _dot(tri, lo)
    gates_t = gates.T
    bcum_t = bcum.T

    hs = []
    for h in range(HEADS):
        seg = slice(h * DH, (h + 1) * DH)
        ig_col = gates[:, h:h + 1]
        b_col = bcum[:, HEADS + h:HEADS + h + 1]
        ig_row = gates_t[h:h + 1, :]
        b_row = bcum_t[HEADS + h:HEADS + h + 1, :]
        m_prev = m_ref[:, h:h + 1]
        qh, kh, vh = q[:, seg], ks[:, seg], v[:, seg]

        dmat = jnp.where(causal, b_col - b_row + ig_row, NEG)
        inter = b_col + m_prev
        m_t = jnp.maximum(inter, jnp.max(dmat, axis=1, keepdims=True))
        w_intra = jnp.exp(dmat - m_t)
        w_inter = jnp.exp(inter - m_t)
        s = lax.dot_general(qh, kh, (((1,), (1,)), ((), ())), preferred_element_type=f32) * w_intra
        num = _dot(_bf(s), vh) + w_inter * _dot(qh, _bf(C_ref[h]))
        qn = jnp.sum(qh.astype(f32) * n_ref[h:h + 1, :], axis=1, keepdims=True)
        den = jnp.sum(s, axis=1, keepdims=True) + w_inter * qn
        hc = num / jnp.maximum(jnp.abs(den), jnp.exp(-m_t))
        hs.append(_layernorm(hc, lng_ref[:, seg]))

        m_new = m_t[L - 1:L, :]
        b_last = b_col[L - 1:L, :]
        decay = jnp.exp(b_last + m_prev - m_new)
        kw = kh.astype(f32) * jnp.exp(b_last - b_col + ig_col - m_new)
        C_ref[h] = decay * C_ref[h] + lax.dot_general(_bf(kw), vh, (((0,), (0,)), ((), ())),
                                                      preferred_element_type=f32)
        n_ref[h:h + 1, :] = decay * n_ref[h:h + 1, :] + jnp.sum(kw, axis=0, keepdims=True)
        m_ref[:, h:h + 1] = m_new
    return jnp.concatenate(hs, axis=1)


def _ml_prompt_kernel(x_ref, mod_ref, g_ref, win_ref, wconv_ref, bconv_ref, wq_ref, wk_ref, wv_ref, wigf_ref,
                      bigf_ref, lng_ref, skip_ref, wout_ref,
                      o_ref, mconv_ref, C_ref, n_ref, m_ref, xbuf_ref, *, tm):
    @pl.when(pl.program_id(1) == 0)
    def _():
        xbuf_ref[0:SUBLANES, :] = jnp.zeros((SUBLANES, INNER), f32)
        C_ref[...] = jnp.zeros_like(C_ref)
        n_ref[...] = jnp.zeros_like(n_ref)
        m_ref[...] = jnp.zeros_like(m_ref)

    x = x_ref[...]
    mod = mod_ref[...]
    xmz = _dot(_bf(_rms_mod(x, g_ref[...], mod)), win_ref[...])
    xm = xmz[:, :INNER]
    z = xmz[:, INNER:]
    xc = _silu(_ml_conv_prompt(xm, wconv_ref, bconv_ref, xbuf_ref, tm))
    mconv_ref[...] = xbuf_ref[SUBLANES + tm - (MCONV - 1):SUBLANES + tm, :]
    xbuf_ref[0:SUBLANES, :] = xbuf_ref[tm:tm + SUBLANES, :]

    q, k, v, gates = _ml_qkv(xm, xc, wq_ref, wk_ref, wv_ref, wigf_ref, bigf_ref)
    hh = _mlstm_chunk(_bf(q), _bf(k * K_SCALE), _bf(v), gates, lng_ref, C_ref, n_ref, m_ref)
    o_ref[...] = _ml_back(hh, xc, z, x, mod, skip_ref[...], wout_ref)


def _ml_prompt(x, mod4, g, w):
    B, T, D = x.shape
    tm = CHUNK
    row = lambda b, t: (b, t, 0)
    return pl.pallas_call(
        functools.partial(_ml_prompt_kernel, tm=tm),
        grid=(B, T // tm),
        in_specs=[
            pl.BlockSpec((None, tm, D), row),
            pl.BlockSpec((None, None, 1, 3 * D), lambda b, t: (0, b, 0, 0)),
            _const_spec((1, D)),
            _const_spec((D, 2 * INNER)),
            _const_spec((MCONV, INNER)),
            _const_spec((1, INNER)),
            _const_spec((HEADS, DH, DH)),
            _const_spec((HEADS, DH, DH)),
            _const_spec((HEADS, DH, DH)),
            _const_spec((3 * INNER, LANES)),
            _const_spec((1, LANES)),
            _const_spec((1, INNER)),
            _const_spec((1, INNER)),
            _const_spec((INNER, D)),
        ],
        out_specs=[
            pl.BlockSpec((None, tm, D), row),
            pl.BlockSpec((None, MCONV - 1, INNER), lambda b, t: (b, 0, 0)),
            pl.BlockSpec((None, HEADS, DH, DH), lambda b, t: (b, 0, 0, 0)),
            pl.BlockSpec((None, HEADS, DH), lambda b, t: (b, 0, 0)),
            pl.BlockSpec((None, 1, LANES), lambda b, t: (b, 0, 0)),
        ],
        out_shape=[
            jax.ShapeDtypeStruct((B, T, D), f32),
            jax.ShapeDtypeStruct((B, MCONV - 1, INNER), f32),
            jax.ShapeDtypeStruct((B, HEADS, DH, DH), f32),
            jax.ShapeDtypeStruct((B, HEADS, DH), f32),
            jax.ShapeDtypeStruct((B, 1, LANES), f32),
        ],
        scratch_shapes=[pltpu.VMEM((SUBLANES + tm, INNER), f32)],
        compiler_params=_params("arbitrary", "arbitrary"),
        name="ml_prompt",
    )(x, mod4, g, w["ml_w_in"], w["ml_w_conv"], w["ml_b_conv"], w["ml_w_q"], w["ml_w_k"], w["ml_w_v"],
      w["ml_w_igf"], w["ml_b_igf"], w["ml_ln_g"], w["ml_skip"], w["ml_w_out"])


def _ml_front_sample_kernel(x_ref, mod_ref, g_ref, win_ref, wconv_ref, bconv_ref, wq_ref, wk_ref, wv_ref,
                            wigf_ref, bigf_ref, st_ref, n_ref, m_ref,
                            q_ref, kw_ref, v_ref, xc_ref, z_ref, nnew_ref, a1_ref, a2_ref, dec_ref, mnew_ref,
                            mconv_ref):
    h = _rms_mod(x_ref[...], g_ref[...], mod_ref[...])
    xmz = _dot(_bf(h), win_ref[...])
    xm = xmz[:, :INNER]
    z_ref[...] = xmz[:, INNER:]
    acc = bconv_ref[...] + wconv_ref[MCONV - 1:MCONV, :] * xm
    for j in range(MCONV - 1):
        acc = acc + wconv_ref[j:j + 1, :] * st_ref[j]
    xc = _silu(acc)
    xc_ref[...] = xc
    mconv_ref[0:MCONV - 2] = st_ref[1:MCONV - 1]
    mconv_ref[MCONV - 2] = xm

    q, k, v, gates = _ml_qkv(xm, xc, wq_ref, wk_ref, wv_ref, wigf_ref, bigf_ref)
    ks = k * K_SCALE
    q_ref[...] = q
    v_ref[...] = v

    ig = gates
    lf = pltpu.roll(gates, LANES - HEADS, axis=1)
    inter = lf + m_ref[...]
    m_new = jnp.maximum(inter, ig)
    w_intra = jnp.exp(ig - m_new)
    w_inter = jnp.exp(inter - m_new)
    lane = lax.broadcasted_iota(jnp.int32, ig.shape, 1)
    qk = jnp.zeros_like(ig)
    qn = jnp.zeros_like(ig)
    for h in range(HEADS):
        seg = slice(h * DH, (h + 1) * DH)
        qk = jnp.where(lane == h, jnp.sum(q[:, seg] * ks[:, seg], axis=1, keepdims=True), qk)
        qn = jnp.where(lane == h, jnp.sum(q[:, seg] * n_ref[:, seg], axis=1, keepdims=True), qn)
        kw = w_intra[:, h:h + 1] * ks[:, seg]
        kw_ref[:, seg] = kw
        nnew_ref[:, seg] = w_inter[:, h:h + 1] * n_ref[:, seg] + kw
    s = qk * w_intra
    den = s + w_inter * qn
    dn = jnp.maximum(jnp.abs(den), jnp.exp(-m_new))
    valid = lane < HEADS
    a1_ref[...] = jnp.where(valid, s / dn, 0.0)
    a2_ref[...] = jnp.where(valid, w_inter / dn, 0.0)
    dec_ref[...] = jnp.where(valid, w_inter, 0.0)
    mnew_ref[...] = jnp.where(valid, m_new, 0.0)


def _ml_front_sample(x, mod, g, w, st, n, m):
    R, D = x.shape
    full = lambda shape: pl.BlockSpec(shape, lambda i: (0,) * len(shape))
    act = jax.ShapeDtypeStruct((R, INNER), f32)
    small = jax.ShapeDtypeStruct((R, LANES), f32)
    return pl.pallas_call(
        _ml_front_sample_kernel,
        grid=(1,),
        in_specs=[
            full((R, D)), full((R, 3 * D)), full((1, D)), full((D, 2 * INNER)), full((MCONV, INNER)),
            full((1, INNER)), full((HEADS, DH, DH)), full((HEADS, DH, DH)), full((HEADS, DH, DH)),
            full((3 * INNER, LANES)), full((1, LANES)),
            full((MCONV - 1, R, INNER)), full((R, INNER)), full((R, LANES)),
        ],
        out_specs=[full((R, INNER))] * 6 + [full((R, LANES))] * 4 + [full((MCONV - 1, R, INNER))],
        out_shape=[act] * 6 + [small] * 4 + [jax.ShapeDtypeStruct((MCONV - 1, R, INNER), f32)],
        compiler_params=_params("arbitrary"),
        name="ml_front_sample",
    )(x, mod, g, w["ml_w_in"], w["ml_w_conv"], w["ml_b_conv"], w["ml_w_q"], w["ml_w_k"], w["ml_w_v"],
      w["ml_w_igf"], w["ml_b_igf"], st, n, m)


def _ml_back_sample_kernel(a1_ref, a2_ref, v_ref, qc_ref, lng_ref, xc_ref, z_ref, x_ref, mod_ref, skip_ref,
                           wout_ref, o_ref):
    hs = []
    for h in range(HEADS):
        seg = slice(h * DH, (h + 1) * DH)
        hc = a1_ref[:, h:h + 1] * v_ref[:, seg] + a2_ref[:, h:h + 1] * qc_ref[:, seg]
        hs.append(_layernorm(hc, lng_ref[:, seg]))
    hh = jnp.concatenate(hs, axis=1)
    o_ref[...] = _ml_back(hh, xc_ref[...], z_ref[...], x_ref[...], mod_ref[...], skip_ref[...], wout_ref)


def _ml_back_sample(a1, a2, v, qc, xc, z, x, mod, w):
    R, D = x.shape
    full = lambda shape: pl.BlockSpec(shape, lambda i: (0,) * len(shape))
    return pl.pallas_call(
        _ml_back_sample_kernel,
        grid=(1,),
        in_specs=[full((R, LANES)), full((R, LANES)), full((R, INNER)), full((R, INNER)), full((1, INNER)),
                  full((R, INNER)), full((R, INNER)), full((R, D)), full((R, 3 * D)), full((1, INNER)),
                  full((INNER, D))],
        out_specs=full((R, D)),
        out_shape=jax.ShapeDtypeStruct((R, D), f32),
        compiler_params=_params("arbitrary"),
        name="ml_back_sample",
    )(a1, a2, v, qc, w["ml_ln_g"], xc, z, x, mod, w["ml_skip"], w["ml_w_out"])


def _cf_in(x, mod, g_ref, win_ref, bin_ref):
    h = _rms_mod(x, g_ref[...], mod)
    agz = _dot(_bf(h), win_ref[...]) + bin_ref[...]
    u = agz[:, :INNER] * jax.nn.sigmoid(agz[:, INNER:2 * INNER])
    return u, agz[:, 2 * INNER:]


def _cf_out(y, z, x, mod, lng_ref, lnb_ref, wout_ref, fg_ref):
    yn = _layernorm(y, lng_ref[...], lnb_ref[...])
    out = _dot(_bf(_silu(yn) * _silu(z)), wout_ref[...])
    return _rms(x + mod[:, 2 * D_MODEL:] * out, fg_ref[...])


def _dwconv_tile(ubuf_ref, sh_ref, wb_ref, bdw_ref, y_ref, tm):
    base = CCONV_HIST - (CCONV - 1)
    max_row_off = (base + CCONV - 2) // SUBLANES * SUBLANES

    def col_body(ci, carry):
        cols = pl.ds(pl.multiple_of(ci * CONV_COLS, CONV_COLS), CONV_COLS)
        for r in range(1, SUBLANES):
            sh_ref[r - 1] = ubuf_ref[r:r + tm + max_row_off, cols]
        b = bdw_ref[:, cols]
        for rc in range(tm // CONV_ROWS):
            acc = jnp.broadcast_to(b, (CONV_ROWS, CONV_COLS))
            for j in range(CCONV):
                a, r = divmod(base + j, SUBLANES)
                rows = pl.ds(SUBLANES * a + CONV_ROWS * rc, CONV_ROWS)
                src = ubuf_ref[rows, cols] if r == 0 else sh_ref[r - 1, rows, :]
                wj = wb_ref[SUBLANES * j:SUBLANES * (j + 1), cols]
                src = src.reshape(CONV_ROWS // SUBLANES, SUBLANES, CONV_COLS)
                acc = acc + (src * wj[None]).reshape(CONV_ROWS, CONV_COLS)
            y_ref[pl.ds(CONV_ROWS * rc, CONV_ROWS), cols] = acc
        return carry

    lax.fori_loop(0, INNER // CONV_COLS, col_body, 0)


def _memory_step(step, dec_ref, c_ref, q_ref, kw_ref, v_ref, cout_ref, qc_ref):
    half = step % 2
    head = (step // 2) % HEADS
    first = (step // (2 * HEADS)) * STEP_BATCH
    qt = q_ref[...].T
    kwt = kw_ref[...].T
    parts = []
    for i in range(STEP_BATCH):
        decay = dec_ref[(first + i) * HEADS + head]
        c = c_ref[i]
        parts.append(jnp.sum(c * qt[:, i:i + 1], axis=0, keepdims=True))
        cout_ref[i] = decay * c + kwt[:, i:i + 1] * v_ref[i:i + 1, :]
    qc = jnp.concatenate(parts, axis=0)

    @pl.when(half == 0)
    def _():
        qc_ref[...] = qc

    @pl.when(half == 1)
    def _():
        qc_ref[...] += qc


def _cf_prompt_kernel(x_ref, mod_ref, g_ref, win_ref, bin_ref, wdw_ref, bdw_ref, lng_ref, lnb_ref, wout_ref,
                      fg_ref, dec_ref, cs_ref, qs_ref, kws_ref, vs_ref,
                      o_ref, cconv_ref, csout_ref, qc_ref, ubuf_ref, y_ref, sh_ref, wb_ref, *, tm):
    step = pl.program_id(0) * pl.num_programs(1) + pl.program_id(1)
    _memory_step(step, dec_ref, cs_ref, qs_ref, kws_ref, vs_ref, csout_ref, qc_ref)

    @pl.when(step == 0)
    def _():
        for j in range(CCONV):
            wb_ref[SUBLANES * j:SUBLANES * (j + 1), :] = jnp.broadcast_to(wdw_ref[j:j + 1, :], (SUBLANES, INNER))

    @pl.when(pl.program_id(1) == 0)
    def _():
        ubuf_ref[0:CCONV_HIST, :] = jnp.zeros((CCONV_HIST, INNER), f32)

    x = x_ref[...]
    mod = mod_ref[...]
    u, z = _cf_in(x, mod, g_ref, win_ref, bin_ref)
    ubuf_ref[CCONV_HIST:CCONV_HIST + tm, :] = u
    _dwconv_tile(ubuf_ref, sh_ref, wb_ref, bdw_ref, y_ref, tm)
    cconv_ref[...] = ubuf_ref[CCONV_HIST + tm - (CCONV - 1):CCONV_HIST + tm, :]
    ubuf_ref[0:CCONV_HIST, :] = ubuf_ref[tm:tm + CCONV_HIST, :]
    o_ref[...] = _cf_out(y_ref[...], z, x, mod, lng_ref, lnb_ref, wout_ref, fg_ref)


def _cf_prompt(x, mod4, g, w, final_g, dec_flat, C_s, q_s, kw_s, v_s):
    B, T, D = x.shape
    R = C_s.shape[0]
    tm = ROW_TILE
    steps_t = T // tm
    assert B * steps_t * STEP_BATCH == R * HEADS * 2, "one (sequence group, head, key half) per grid step"
    row = lambda b, t: (b, t, 0)

    def ghk(b, t):
        s = b * steps_t + t
        return s // (2 * HEADS), (s // 2) % HEADS, s % 2

    c_spec = pl.BlockSpec((STEP_BATCH, None, DH // 2, DH), lambda b, t: (*ghk(b, t), 0))
    half_spec = pl.BlockSpec((STEP_BATCH, DH // 2), lambda b, t: (ghk(b, t)[0], 2 * ghk(b, t)[1] + ghk(b, t)[2]))
    head_spec = pl.BlockSpec((STEP_BATCH, DH), lambda b, t: ghk(b, t)[:2])
    return pl.pallas_call(
        functools.partial(_cf_prompt_kernel, tm=tm),
        grid=(B, steps_t),
        in_specs=[
            pl.BlockSpec((None, tm, D), row),
            pl.BlockSpec((None, None, 1, 3 * D), lambda b, t: (1, b, 0, 0)),
            _const_spec((1, D)),
            _const_spec((D, 3 * INNER)),
            _const_spec((1, 3 * INNER)),
            _const_spec((CCONV, INNER)),
            _const_spec((1, INNER)),
            _const_spec((1, INNER)),
            _const_spec((1, INNER)),
            _const_spec((INNER, D)),
            _const_spec((1, D)),
            pl.BlockSpec(memory_space=pltpu.SMEM),
            c_spec, half_spec, half_spec, head_spec,
        ],
        out_specs=[
            pl.BlockSpec((None, tm, D), row),
            pl.BlockSpec((None, CCONV - 1, INNER), lambda b, t: (b, 0, 0)),
            c_spec,
            head_spec,
        ],
        out_shape=[
            jax.ShapeDtypeStruct((B, T, D), f32),
            jax.ShapeDtypeStruct((B, CCONV - 1, INNER), f32),
            jax.ShapeDtypeStruct(C_s.shape, f32),
            jax.ShapeDtypeStruct((R, INNER), f32),
        ],
        scratch_shapes=[
            pltpu.VMEM((CCONV_HIST + tm, INNER), f32),
            pltpu.VMEM((tm, INNER), f32),
            pltpu.VMEM((SUBLANES - 1, tm + CCONV_HIST - SUBLANES, CONV_COLS), f32),
            pltpu.VMEM((SUBLANES * CCONV, INNER), f32),
        ],
        compiler_params=_params("arbitrary", "arbitrary"),
        name="cf_prompt",
    )(x, mod4, g, w["cf_w_in"], w["cf_b_in"], w["cf_w_dw"], w["cf_b_dw"], w["cf_ln_g"], w["cf_ln_b"],
      w["cf_w_out"], final_g, dec_flat, C_s, q_s, kw_s, v_s)


def _cf_sample_kernel(x_ref, mod_ref, g_ref, win_ref, bin_ref, wdw_ref, bdw_ref, lng_ref, lnb_ref, wout_ref,
                      fg_ref, st_ref, o_ref, stout_ref, u_ref, z_ref, y_ref, *, tb):
    i = pl.program_id(0)

    @pl.when(i == 0)
    def _():
        u, z = _cf_in(x_ref[...], mod_ref[...], g_ref, win_ref, bin_ref)
        u_ref[...] = u
        z_ref[...] = z

    rows = pl.ds(pl.multiple_of(i * tb, tb), tb)
    u = u_ref[rows, :]
    acc = bdw_ref[...] + wdw_ref[CCONV - 1:CCONV, :] * u
    for j in range(CCONV - 1):
        acc = acc + wdw_ref[j:j + 1, :] * st_ref[j]
    y_ref[rows, :] = acc
    stout_ref[0:CCONV - 2] = st_ref[1:CCONV - 1]
    stout_ref[CCONV - 2] = u

    @pl.when(i == pl.num_programs(0) - 1)
    def _():
        o_ref[...] = _cf_out(y_ref[...], z_ref[...], x_ref[...], mod_ref[...], lng_ref, lnb_ref, wout_ref, fg_ref)


def _cf_sample(x, mod, g, w, final_g, st):
    R, D = x.shape
    tb = SAMPLE_TILE
    state = pl.BlockSpec((CCONV - 1, tb, INNER), lambda i: (0, i, 0))
    return pl.pallas_call(
        functools.partial(_cf_sample_kernel, tb=tb),
        grid=(R // tb,),
        in_specs=[
            _const_spec((R, D)),
            _const_spec((R, 3 * D)),
            _const_spec((1, D)),
            _const_spec((D, 3 * INNER)),
            _const_spec((1, 3 * INNER)),
            _const_spec((CCONV, INNER)),
            _const_spec((1, INNER)),
            _const_spec((1, INNER)),
            _const_spec((1, INNER)),
            _const_spec((INNER, D)),
            _const_spec((1, D)),
            state,
        ],
        out_specs=[pl.BlockSpec((R, D), lambda i: (0, 0)), state],
        out_shape=[jax.ShapeDtypeStruct((R, D), f32), jax.ShapeDtypeStruct((CCONV - 1, R, INNER), f32)],
        scratch_shapes=[pltpu.VMEM((R, INNER), f32)] * 3,
        compiler_params=_params("arbitrary"),
        name="cf_sample",
    )(x, mod, g, w["cf_w_in"], w["cf_b_in"], w["cf_w_dw"], w["cf_b_dw"], w["cf_ln_g"], w["cf_ln_b"],
      w["cf_w_out"], final_g, st)


def kernel(x_prompt, x_sample, c_prompt, c_sample, state_mlstm_C, state_mlstm_n, state_mlstm_m, state_mlstm_conv, state_conf_conv, norm_g, w_ada, b_ada, ml_w_in, ml_w_conv, ml_b_conv, ml_w_q, ml_w_k, ml_w_v, ml_w_ig, ml_b_ig, ml_w_fg, ml_b_fg, ml_ln_g, ml_skip, ml_w_out, cf_w_in, cf_b_in, cf_w_dw, cf_b_dw, cf_ln_g, cf_ln_b, cf_w_out, final_g):
    B = x_prompt.shape[0]
    R = x_sample.shape[0]
    D = D_MODEL

    gate_pad = LANES - 2 * HEADS
    w = {
        "ml_w_in": _bf(ml_w_in[0]),
        "ml_w_conv": ml_w_conv[0],
        "ml_b_conv": ml_b_conv,
        "ml_w_q": _bf(ml_w_q[0]),
        "ml_w_k": _bf(ml_w_k[0]),
        "ml_w_v": _bf(ml_w_v[0]),
        "ml_w_igf": _bf(jnp.pad(jnp.concatenate([ml_w_ig[0], ml_w_fg[0]], axis=1), ((0, 0), (0, gate_pad)))),
        "ml_b_igf": jnp.pad(jnp.concatenate([ml_b_ig, ml_b_fg], axis=1), ((0, 0), (0, gate_pad))),
        "ml_ln_g": ml_ln_g.reshape(1, INNER),
        "ml_skip": ml_skip,
        "ml_w_out": _bf(ml_w_out[0]),
        "cf_w_in": _bf(cf_w_in[0]),
        "cf_b_in": cf_b_in,
        "cf_w_dw": cf_w_dw[0],
        "cf_b_dw": cf_b_dw,
        "cf_ln_g": cf_ln_g,
        "cf_ln_b": cf_ln_b,
        "cf_w_out": _bf(cf_w_out[0]),
    }
    final_g2 = final_g.reshape(1, D)

    mod = _ada(jnp.concatenate([c_prompt, c_sample], axis=0), w_ada, b_ada)
    mod4 = mod.reshape(mod.shape[0], B + R, 1, 3 * D)
    mod_s = mod[:, B:]

    tap_major = lambda a: jnp.transpose(a, (1, 0, 2))
    xs = x_sample.reshape(R, D)
    m_pad = jnp.pad(state_mlstm_m[0], ((0, 0), (0, LANES - HEADS)))
    (q_s, kw_s, v_s, xc_s, z_s, n_s, a1, a2, dec, m_s, mconv_s) = _ml_front_sample(
        xs, mod_s[0], norm_g[0:1], w, tap_major(state_mlstm_conv[0]), state_mlstm_n[0].reshape(R, INNER), m_pad)

    x1, mconv_p, C_p, n_p, m_p = _ml_prompt(x_prompt, mod4, norm_g[0:1], w)
    y_prompt, cconv_p, C_s, qc = _cf_prompt(
        x1, mod4, norm_g[1:2], w, final_g2, dec[:, :HEADS].reshape(R * HEADS), state_mlstm_C[0], q_s, kw_s, v_s)

    x1_s = _ml_back_sample(a1, a2, v_s, qc, xc_s, z_s, xs, mod_s[0], w)
    y_sample, cconv_s = _cf_sample(x1_s, mod_s[1], norm_g[1:2], w, final_g2, tap_major(state_conf_conv[0]))

    return (
        y_prompt,
        y_sample.reshape(R, 1, D),
        C_p[None],
        C_s[None],
        n_p[None],
        n_s.reshape(1, R, HEADS, DH),
        m_p[:, 0, :HEADS][None],
        m_s[:, :HEADS][None],
        mconv_p[None],
        tap_major(mconv_s)[None],
        cconv_p[None],
        tap_major(cconv_s)[None],
    )
```

```python
import functools

import jax
import jax.numpy as jnp
from jax import lax
from jax.experimental import pallas as pl
from jax.experimental.pallas import tpu as pltpu

D_MODEL = 1024
INNER = 2 * D_MODEL
HEADS = 4
DH = INNER // HEADS
MCONV = 4
CCONV = 31
EPS = 1e-6
NEG = -1e30
K_SCALE = DH ** -0.5

LANES = 128
SUBLANES = 8
VMEM_LIMIT = 56 * 2 ** 20

ROW_TILE = 128
CHUNK = 256
STEP_BATCH = 8
CONV_ROWS = 64
CONV_COLS = 256
CCONV_HIST = 32
SAMPLE_TILE = 16

f32 = jnp.float32
bf16 = jnp.bfloat16


def _bf(x):
    return x.astype(bf16)


def _dot(a, b):
    return jnp.dot(a, b, preferred_element_type=f32)


def _silu(x):
    return x * jax.nn.sigmoid(x)


def _log_sigmoid(x):
    return jnp.minimum(x, 0.0) - jnp.log1p(jnp.exp(-jnp.abs(x)))


def _rms(x, g):
    return x * lax.rsqrt(jnp.mean(x * x, axis=-1, keepdims=True) + EPS) * g


def _rms_mod(x, g, mod):
    return _rms(x, g) * (1.0 + mod[:, D_MODEL:2 * D_MODEL]) + mod[:, :D_MODEL]


def _layernorm(x, g, b=None):
    mu = jnp.mean(x, axis=-1, keepdims=True)
    xc = x - mu
    var = jnp.mean(xc * xc, axis=-1, keepdims=True)
    y = xc * lax.rsqrt(var + EPS) * g
    return y if b is None else y + b


def _const_spec(shape):
    n = len(shape)
    return pl.BlockSpec(shape, lambda *_: (0,) * n, pipeline_mode=pl.Buffered(1))


def _params(*sem):
    return pltpu.CompilerParams(dimension_semantics=sem, vmem_limit_bytes=VMEM_LIMIT)


def _ada_kernel(c_ref, w_ref, b_ref, o_ref):
    o_ref[...] = _dot(_bf(_silu(c_ref[...])), _bf(w_ref[...])) + b_ref[...]


def _ada(c_all, w_ada, b_ada):
    depth, d, d3 = w_ada.shape
    rows = c_all.shape[0]
    tn = D_MODEL
    return pl.pallas_call(
        _ada_kernel,
        grid=(depth, d3 // tn),
        in_specs=[
            pl.BlockSpec((rows, d), lambda i, j: (0, 0)),
            pl.BlockSpec((None, d, tn), lambda i, j: (i, 0, j)),
            pl.BlockSpec((None, 1, tn), lambda i, j: (i, 0, j)),
        ],
        out_specs=pl.BlockSpec((None, rows, tn), lambda i, j: (i, 0, j)),
        out_shape=jax.ShapeDtypeStruct((depth, rows, d3), f32),
        compiler_params=_params("arbitrary", "arbitrary"),
        name="ada",
    )(c_all, w_ada, b_ada.reshape(depth, 1, d3))


def _ml_qkv(xm, xc, wq_ref, wk_ref, wv_ref, wigf_ref, bigf_ref):
    xcb, xmb = _bf(xc), _bf(xm)
    qs, ks, vs = [], [], []
    for h in range(HEADS):
        seg = slice(h * DH, (h + 1) * DH)
        qs.append(_dot(xcb[:, seg], wq_ref[h]))
        ks.append(_dot(xcb[:, seg], wk_ref[h]))
        vs.append(_dot(xmb[:, seg], wv_ref[h]))
    q = jnp.concatenate(qs, axis=1)
    k = jnp.concatenate(ks, axis=1)
    v = jnp.concatenate(vs, axis=1)
    pre = (_dot(_bf(q), wigf_ref[0:INNER, :]) + _dot(_bf(k), wigf_ref[INNER:2 * INNER, :])
           + _dot(_bf(v), wigf_ref[2 * INNER:3 * INNER, :]) + bigf_ref[...])
    lane = lax.broadcasted_iota(jnp.int32, pre.shape, 1)
    gates = jnp.where(lane >= HEADS, _log_sigmoid(pre), pre)
    return q, k, v, gates


def _ml_back(hh, xc, z, x, mod, skip, wout_ref):
    act = (hh + skip * xc) * _silu(z)
    return x + mod[:, 2 * D_MODEL:] * _dot(_bf(act), wout_ref[...])


def _ml_conv_prompt(xm, wconv_ref, bconv_ref, xbuf_ref, tm):
    xbuf_ref[SUBLANES:SUBLANES + tm, :] = xm
    acc = bconv_ref[...] + wconv_ref[MCONV - 1:MCONV, :] * xm
    for j in range(MCONV - 1):
        start = SUBLANES - (MCONV - 1) + j
        acc = acc + wconv_ref[j:j + 1, :] * xbuf_ref[start:start + tm, :]
    return acc


def _mlstm_chunk(q, ks, v, gates, lng_ref, C_ref, n_ref, m_ref):
    L = q.shape[0]
    row = lax.broadcasted_iota(jnp.int32, (L, L), 0)
    col = lax.broadcasted_iota(jnp.int32, (L, L), 1)
    causal = row >= col
    tri = jnp.where(causal, 1.0, 0.0).astype(bf16)
    hi = _bf(gates)
    lo = _bf(gates - hi.astype(f32))
    bcum = _dot(tri, hi) + _dot(tri, lo)
    gates_t = gates.T
    bcum_t = bcum.T

    hs = []
    for h in range(HEADS):
        seg = slice(h * DH, (h + 1) * DH)
        ig_col = gates[:, h:h + 1]
        b_col = bcum[:, HEADS + h:HEADS + h + 1]
        ig_row = gates_t[h:h + 1, :]
        b_row = bcum_t[HEADS + h:HEADS + h + 1, :]
        m_prev = m_ref[:, h:h + 1]
        qh, kh, vh = q[:, seg], ks[:, seg], v[:, seg]

        dmat = jnp.where(causal, b_col - b_row + ig_row, NEG)
        inter = b_col + m_prev
        m_t = jnp.maximum(inter, jnp.max(dmat, axis=1, keepdims=True))
        w_intra = jnp.exp(dmat - m_t)
        w_inter = jnp.exp(inter - m_t)
        s = lax.dot_general(qh, kh, (((1,), (1,)), ((), ())), preferred_element_type=f32) * w_intra
        num = _dot(_bf(s), vh) + w_inter * _dot(qh, _bf(C_ref[h]))
        qn = jnp.sum(qh.astype(f32) * n_ref[h:h + 1, :], axis=1, keepdims=True)
        den = jnp.sum(s, axis=1, keepdims=True) + w_inter * qn
        hc = num / jnp.maximum(jnp.abs(den), jnp.exp(-m_t))
        hs.append(_layernorm(hc, lng_ref[:, seg]))

        m_new = m_t[L - 1:L, :]
        b_last = b_col[L - 1:L, :]
        decay = jnp.exp(b_last + m_prev - m_new)
        kw = kh.astype(f32) * jnp.exp(b_last - b_col + ig_col - m_new)
        C_ref[h] = decay * C_ref[h] + lax.dot_general(_bf(kw), vh, (((0,), (0,)), ((), ())),
                                                      preferred_element_type=f32)
        n_ref[h:h + 1, :] = decay * n_ref[h:h + 1, :] + jnp.sum(kw, axis=0, keepdims=True)
        m_ref[:, h:h + 1] = m_new
    return jnp.concatenate(hs, axis=1)


def _ml_prompt_kernel(x_ref, mod_ref, g_ref, win_ref, wconv_ref, bconv_ref, wq_ref, wk_ref, wv_ref, wigf_ref,
                      bigf_ref, lng_ref, skip_ref, wout_ref,
                      o_ref, mconv_ref, C_ref, n_ref, m_ref, xbuf_ref, *, tm):
    @pl.when(pl.program_id(1) == 0)
    def _():
        xbuf_ref[0:SUBLANES, :] = jnp.zeros((SUBLANES, INNER), f32)
        C_ref[...] = jnp.zeros_like(C_ref)
        n_ref[...] = jnp.zeros_like(n_ref)
        m_ref[...] = jnp.zeros_like(m_ref)

    x = x_ref[...]
    mod = mod_ref[...]
    xmz = _dot(_bf(_rms_mod(x, g_ref[...], mod)), win_ref[...])
    xm = xmz[:, :INNER]
    z = xmz[:, INNER:]
    xc = _silu(_ml_conv_prompt(xm, wconv_ref, bconv_ref, xbuf_ref, tm))
    mconv_ref[...] = xbuf_ref[SUBLANES + tm - (MCONV - 1):SUBLANES + tm, :]
    xbuf_ref[0:SUBLANES, :] = xbuf_ref[tm:tm + SUBLANES, :]

    q, k, v, gates = _ml_qkv(xm, xc, wq_ref, wk_ref, wv_ref, wigf_ref, bigf_ref)
    hh = _mlstm_chunk(_bf(q), _bf(k * K_SCALE), _bf(v), gates, lng_ref, C_ref, n_ref, m_ref)
    o_ref[...] = _ml_back(hh, xc, z, x, mod, skip_ref[...], wout_ref)


def _ml_prompt(x, mod4, g, w):
    B, T, D = x.shape
    tm = CHUNK
    row = lambda b, t: (b, t, 0)
    return pl.pallas_call(
        functools.partial(_ml_prompt_kernel, tm=tm),
        grid=(B, T // tm),
        in_specs=[
            pl.BlockSpec((None, tm, D), row),
            pl.BlockSpec((None, None, 1, 3 * D), lambda b, t: (0, b, 0, 0)),
            _const_spec((1, D)),
            _const_spec((D, 2 * INNER)),
            _const_spec((MCONV, INNER)),
            _const_spec((1, INNER)),
            _const_spec((HEADS, DH, DH)),
            _const_spec((HEADS, DH, DH)),
            _const_spec((HEADS, DH, DH)),
            _const_spec((3 * INNER, LANES)),
            _const_spec((1, LANES)),
            _const_spec((1, INNER)),
            _const_spec((1, INNER)),
            _const_spec((INNER, D)),
        ],
        out_specs=[
            pl.BlockSpec((None, tm, D), row),
            pl.BlockSpec((None, MCONV - 1, INNER), lambda b, t: (b, 0, 0)),
            pl.BlockSpec((None, HEADS, DH, DH), lambda b, t: (b, 0, 0, 0)),
            pl.BlockSpec((None, HEADS, DH), lambda b, t: (b, 0, 0)),
            pl.BlockSpec((None, 1, LANES), lambda b, t: (b, 0, 0)),
        ],
        out_shape=[
            jax.ShapeDtypeStruct((B, T, D), f32),
            jax.ShapeDtypeStruct((B, MCONV - 1, INNER), f32),
            jax.ShapeDtypeStruct((B, HEADS, DH, DH), f32),
            jax.ShapeDtypeStruct((B, HEADS, DH), f32),
            jax.ShapeDtypeStruct((B, 1, LANES), f32),
        ],
        scratch_shapes=[pltpu.VMEM((SUBLANES + tm, INNER), f32)],
        compiler_params=_params("arbitrary", "arbitrary"),
        name="ml_prompt",
    )(x, mod4, g, w["ml_w_in"], w["ml_w_conv"], w["ml_b_conv"], w["ml_w_q"], w["ml_w_k"], w["ml_w_v"],
      w["ml_w_igf"], w["ml_b_igf"], w["ml_ln_g"], w["ml_skip"], w["ml_w_out"])


def _ml_front_sample_kernel(x_ref, mod_ref, g_ref, win_ref, wconv_ref, bconv_ref, wq_ref, wk_ref, wv_ref,
                            wigf_ref, bigf_ref, st_ref, n_ref, m_ref,
                            q_ref, kw_ref, v_ref, xc_ref, z_ref, nnew_ref, a1_ref, a2_ref, dec_ref, mnew_ref,
                            mconv_ref):
    h = _rms_mod(x_ref[...], g_ref[...], mod_ref[...])
    xmz = _dot(_bf(h), win_ref[...])
    xm = xmz[:, :INNER]
    z_ref[...] = xmz[:, INNER:]
    acc = bconv_ref[...] + wconv_ref[MCONV - 1:MCONV, :] * xm
    for j in range(MCONV - 1):
        acc = acc + wconv_ref[j:j + 1, :] * st_ref[j]
    xc = _silu(acc)
    xc_ref[...] = xc
    mconv_ref[0:MCONV - 2] = st_ref[1:MCONV - 1]
    mconv_ref[MCONV - 2] = xm

    q, k, v, gates = _ml_qkv(xm, xc, wq_ref, wk_ref, wv_ref, wigf_ref, bigf_ref)
    ks = k * K_SCALE
    q_ref[...] = q
    v_ref[...] = v

    ig = gates
    lf = pltpu.roll(gates, LANES - HEADS, axis=1)
    inter = lf + m_ref[...]
    m_new = jnp.maximum(inter, ig)
    w_intra = jnp.exp(ig - m_new)
    w_inter = jnp.exp(inter - m_new)
    lane = lax.broadcasted_iota(jnp.int32, ig.shape, 1)
    qk = jnp.zeros_like(ig)
    qn = jnp.zeros_like(ig)
    for h in range(HEADS):
        seg = slice(h * DH, (h + 1) * DH)
        qk = jnp.where(lane == h, jnp.sum(q[:, seg] * ks[:, seg], axis=1, keepdims=True), qk)
        qn = jnp.where(lane == h, jnp.sum(q[:, seg] * n_ref[:, seg], axis=1, keepdims=True), qn)
        kw = w_intra[:, h:h + 1] * ks[:, seg]
        kw_ref[:, seg] = kw
        nnew_ref[:, seg] = w_inter[:, h:h + 1] * n_ref[:, seg] + kw
    s = qk * w_intra
    den = s + w_inter * qn
    dn = jnp.maximum(jnp.abs(den), jnp.exp(-m_new))
    valid = lane < HEADS
    a1_ref[...] = jnp.where(valid, s / dn, 0.0)
    a2_ref[...] = jnp.where(valid, w_inter / dn, 0.0)
    dec_ref[...] = jnp.where(valid, w_inter, 0.0)
    mnew_ref[...] = jnp.where(valid, m_new, 0.0)


def _ml_front_sample(x, mod, g, w, st, n, m):
    R, D = x.shape
    full = lambda shape: pl.BlockSpec(shape, lambda i: (0,) * len(shape))
    act = jax.ShapeDtypeStruct((R, INNER), f32)
    small = jax.ShapeDtypeStruct((R, LANES), f32)
    return pl.pallas_call(
        _ml_front_sample_kernel,
        grid=(1,),
        in_specs=[
            full((R, D)), full((R, 3 * D)), full((1, D)), full((D, 2 * INNER)), full((MCONV, INNER)),
            full((1, INNER)), full((HEADS, DH, DH)), full((HEADS, DH, DH)), full((HEADS, DH, DH)),
            full((3 * INNER, LANES)), full((1, LANES)),
            full((MCONV - 1, R, INNER)), full((R, INNER)), full((R, LANES)),
        ],
        out_specs=[full((R, INNER))] * 6 + [full((R, LANES))] * 4 + [full((MCONV - 1, R, INNER))],
        out_shape=[act] * 6 + [small] * 4 + [jax.ShapeDtypeStruct((MCONV - 1, R, INNER), f32)],
        compiler_params=_params("arbitrary"),
        name="ml_front_sample",
    )(x, mod, g, w["ml_w_in"], w["ml_w_conv"], w["ml_b_conv"], w["ml_w_q"], w["ml_w_k"], w["ml_w_v"],
      w["ml_w_igf"], w["ml_b_igf"], st, n, m)


def _ml_back_sample_kernel(a1_ref, a2_ref, v_ref, qc_ref, lng_ref, xc_ref, z_ref, x_ref, mod_ref, skip_ref,
                           wout_ref, o_ref):
    hs = []
    for h in range(HEADS):
        seg = slice(h * DH, (h + 1) * DH)
        hc = a1_ref[:, h:h + 1] * v_ref[:, seg] + a2_ref[:, h:h + 1] * qc_ref[:, seg]
        hs.append(_layernorm(hc, lng_ref[:, seg]))
    hh = jnp.concatenate(hs, axis=1)
    o_ref[...] = _ml_back(hh, xc_ref[...], z_ref[...], x_ref[...], mod_ref[...], skip_ref[...], wout_ref)


def _ml_back_sample(a1, a2, v, qc, xc, z, x, mod, w):
    R, D = x.shape
    full = lambda shape: pl.BlockSpec(shape, lambda i: (0,) * len(shape))
    return pl.pallas_call(
        _ml_back_sample_kernel,
        grid=(1,),
        in_specs=[full((R, LANES)), full((R, LANES)), full((R, INNER)), full((R, INNER)), full((1, INNER)),
                  full((R, INNER)), full((R, INNER)), full((R, D)), full((R, 3 * D)), full((1, INNER)),
                  full((INNER, D))],
        out_specs=full((R, D)),
        out_shape=jax.ShapeDtypeStruct((R, D), f32),
        compiler_params=_params("arbitrary"),
        name="ml_back_sample",
    )(a1, a2, v, qc, w["ml_ln_g"], xc, z, x, mod, w["ml_skip"], w["ml_w_out"])


def _cf_in(x, mod, g_ref, win_ref, bin_ref):
    h = _rms_mod(x, g_ref[...], mod)
    agz = _dot(_bf(h), win_ref[...]) + bin_ref[...]
    u = agz[:, :INNER] * jax.nn.sigmoid(agz[:, INNER:2 * INNER])
    return u, agz[:, 2 * INNER:]


def _cf_out(y, z, x, mod, lng_ref, lnb_ref, wout_ref, fg_ref):
    yn = _layernorm(y, lng_ref[...], lnb_ref[...])
    out = _dot(_bf(_silu(yn) * _silu(z)), wout_ref[...])
    return _rms(x + mod[:, 2 * D_MODEL:] * out, fg_ref[...])


def _dwconv_block(ubuf_ref, sh_ref, wb_ref, bdw_ref, y_ref, tm, cols):
    base = CCONV_HIST - (CCONV - 1)
    max_row_off = (base + CCONV - 2) // SUBLANES * SUBLANES
    for r in range(1, SUBLANES):
        sh_ref[r - 1] = ubuf_ref[r:r + tm + max_row_off, cols]
    b = bdw_ref[:, cols]
    for rc in range(tm // CONV_ROWS):
        acc = jnp.broadcast_to(b, (CONV_ROWS, CONV_COLS))
        for j in range(CCONV):
            a, r = divmod(base + j, SUBLANES)
            rows = pl.ds(SUBLANES * a + CONV_ROWS * rc, CONV_ROWS)
            src = ubuf_ref[rows, cols] if r == 0 else sh_ref[r - 1, rows, :]
            wj = wb_ref[SUBLANES * j:SUBLANES * (j + 1), cols]
            src = src.reshape(CONV_ROWS // SUBLANES, SUBLANES, CONV_COLS)
            acc = acc + (src * wj[None]).reshape(CONV_ROWS, CONV_COLS)
        y_ref[pl.ds(CONV_ROWS * rc, CONV_ROWS), cols] = acc


def _memory_step(step, dec_ref, c_ref, q_ref, kw_ref, v_ref, cout_ref):
    head = (step // 2) % HEADS
    first = (step // (2 * HEADS)) * STEP_BATCH
    qt = q_ref[...].T
    kwt = kw_ref[...].T
    parts = []
    for i in range(STEP_BATCH):
        decay = dec_ref[(first + i) * HEADS + head]
        c = c_ref[i]
        parts.append(jnp.sum(c * qt[:, i:i + 1], axis=0, keepdims=True))
        cout_ref[i] = decay * c + kwt[:, i:i + 1] * v_ref[i:i + 1, :]
    return jnp.concatenate(parts, axis=0)


def _cf_prompt_kernel(x_ref, mod_ref, g_ref, win_ref, bin_ref, wdw_ref, bdw_ref, lng_ref, lnb_ref, wout_ref,
                      fg_ref, dec_ref, cs_ref, qs_ref, kws_ref, vs_ref,
                      o_ref, cconv_ref, csout_ref, qc_ref, ubuf_ref, y_ref, sh_ref, wb_ref, *, tm):
    step = pl.program_id(0) * pl.num_programs(1) + pl.program_id(1)

    @pl.when(step == 0)
    def _():
        for j in range(CCONV):
            wb_ref[SUBLANES * j:SUBLANES * (j + 1), :] = jnp.broadcast_to(wdw_ref[j:j + 1, :], (SUBLANES, INNER))

    @pl.when(pl.program_id(1) == 0)
    def _():
        ubuf_ref[0:CCONV_HIST, :] = jnp.zeros((CCONV_HIST, INNER), f32)

    qc = _memory_step(step, dec_ref, cs_ref, qs_ref, kws_ref, vs_ref, csout_ref)

    x = x_ref[...]
    mod = mod_ref[...]
    hb = _bf(_rms_mod(x, g_ref[...], mod))
    for cb in range(INNER // CONV_COLS):
        a_cols = slice(cb * CONV_COLS, (cb + 1) * CONV_COLS)
        g_cols = slice(INNER + cb * CONV_COLS, INNER + (cb + 1) * CONV_COLS)
        a = _dot(hb, win_ref[:, a_cols]) + bin_ref[:, a_cols]
        gate = _dot(hb, win_ref[:, g_cols]) + bin_ref[:, g_cols]
        ubuf_ref[CCONV_HIST:CCONV_HIST + tm, a_cols] = a * jax.nn.sigmoid(gate)
        _dwconv_block(ubuf_ref, sh_ref, wb_ref, bdw_ref, y_ref, tm, a_cols)
    z = _dot(hb, win_ref[:, 2 * INNER:]) + bin_ref[:, 2 * INNER:]
    cconv_ref[...] = ubuf_ref[CCONV_HIST + tm - (CCONV - 1):CCONV_HIST + tm, :]
    ubuf_ref[0:CCONV_HIST, :] = ubuf_ref[tm:tm + CCONV_HIST, :]
    o_ref[...] = _cf_out(y_ref[...], z, x, mod, lng_ref, lnb_ref, wout_ref, fg_ref)

    @pl.when(step % 2 == 0)
    def _():
        qc_ref[...] = qc

    @pl.when(step % 2 == 1)
    def _():
        qc_ref[...] += qc


def _cf_prompt(x, mod4, g, w, final_g, dec_flat, C_s, q_s, kw_s, v_s):
    B, T, D = x.shape
    R = C_s.shape[0]
    tm = ROW_TILE
    steps_t = T // tm
    assert B * steps_t * STEP_BATCH == R * HEADS * 2, "one (sequence group, head, key half) per grid step"
    row = lambda b, t: (b, t, 0)

    def ghk(b, t):
        s = b * steps_t + t
        return s // (2 * HEADS), (s // 2) % HEADS, s % 2

    c_spec = pl.BlockSpec((STEP_BATCH, None, DH // 2, DH), lambda b, t: (*ghk(b, t), 0))
    half_spec = pl.BlockSpec((STEP_BATCH, DH // 2), lambda b, t: (ghk(b, t)[0], 2 * ghk(b, t)[1] + ghk(b, t)[2]))
    head_spec = pl.BlockSpec((STEP_BATCH, DH), lambda b, t: ghk(b, t)[:2])
    return pl.pallas_call(
        functools.partial(_cf_prompt_kernel, tm=tm),
        grid=(B, steps_t),
        in_specs=[
            pl.BlockSpec((None, tm, D), row),
            pl.BlockSpec((None, None, 1, 3 * D), lambda b, t: (1, b, 0, 0)),
            _const_spec((1, D)),
            _const_spec((D, 3 * INNER)),
            _const_spec((1, 3 * INNER)),
            _const_spec((CCONV, INNER)),
            _const_spec((1, INNER)),
            _const_spec((1, INNER)),
            _const_spec((1, INNER)),
            _const_spec((INNER, D)),
            _const_spec((1, D)),
            pl.BlockSpec(memory_space=pltpu.SMEM),
            c_spec, half_spec, half_spec, head_spec,
        ],
        out_specs=[
            pl.BlockSpec((None, tm, D), row),
            pl.BlockSpec((None, CCONV - 1, INNER), lambda b, t: (b, 0, 0)),
            c_spec,
            head_spec,
        ],
        out_shape=[
            jax.ShapeDtypeStruct((B, T, D), f32),
            jax.ShapeDtypeStruct((B, CCONV - 1, INNER), f32),
            jax.ShapeDtypeStruct(C_s.shape, f32),
            jax.ShapeDtypeStruct((R, INNER), f32),
        ],
        scratch_shapes=[
            pltpu.VMEM((CCONV_HIST + tm, INNER), f32),
            pltpu.VMEM((tm, INNER), f32),
            pltpu.VMEM((SUBLANES - 1, tm + CCONV_HIST - SUBLANES, CONV_COLS), f32),
            pltpu.VMEM((SUBLANES * CCONV, INNER), f32),
        ],
        compiler_params=_params("arbitrary", "arbitrary"),
        name="cf_prompt",
    )(x, mod4, g, w["cf_w_in"], w["cf_b_in"], w["cf_w_dw"], w["cf_b_dw"], w["cf_ln_g"], w["cf_ln_b"],
      w["cf_w_out"], final_g, dec_flat, C_s, q_s, kw_s, v_s)


def _cf_sample_kernel(x_ref, mod_ref, g_ref, win_ref, bin_ref, wdw_ref, bdw_ref, lng_ref, lnb_ref, wout_ref,
                      fg_ref, st_ref, o_ref, stout_ref, u_ref, z_ref, y_ref, *, tb):
    i = pl.program_id(0)

    @pl.when(i == 0)
    def _():
        u, z = _cf_in(x_ref[...], mod_ref[...], g_ref, win_ref, bin_ref)
        u_ref[...] = u
        z_ref[...] = z

    rows = pl.ds(pl.multiple_of(i * tb, tb), tb)
    u = u_ref[rows, :]
    acc = bdw_ref[...] + wdw_ref[CCONV - 1:CCONV, :] * u
    for j in range(CCONV - 1):
        acc = acc + wdw_ref[j:j + 1, :] * st_ref[j]
    y_ref[rows, :] = acc
    stout_ref[0:CCONV - 2] = st_ref[1:CCONV - 1]
    stout_ref[CCONV - 2] = u

    @pl.when(i == pl.num_programs(0) - 1)
    def _():
        o_ref[...] = _cf_out(y_ref[...], z_ref[...], x_ref[...], mod_ref[...], lng_ref, lnb_ref, wout_ref, fg_ref)


def _cf_sample(x, mod, g, w, final_g, st):
    R, D = x.shape
    tb = SAMPLE_TILE
    state = pl.BlockSpec((CCONV - 1, tb, INNER), lambda i: (0, i, 0))
    return pl.pallas_call(
        functools.partial(_cf_sample_kernel, tb=tb),
        grid=(R // tb,),
        in_specs=[
            _const_spec((R, D)),
            _const_spec((R, 3 * D)),
            _const_spec((1, D)),
            _const_spec((D, 3 * INNER)),
            _const_spec((1, 3 * INNER)),
            _const_spec((CCONV, INNER)),
            _const_spec((1, INNER)),
            _const_spec((1, INNER)),
            _const_spec((1, INNER)),
            _const_spec((INNER, D)),
            _const_spec((1, D)),
            state,
        ],
        out_specs=[pl.BlockSpec((R, D), lambda i: (0, 0)), state],
        out_shape=[jax.ShapeDtypeStruct((R, D), f32), jax.ShapeDtypeStruct((CCONV - 1, R, INNER), f32)],
        scratch_shapes=[pltpu.VMEM((R, INNER), f32)] * 3,
        compiler_params=_params("arbitrary"),
        name="cf_sample",
    )(x, mod, g, w["cf_w_in"], w["cf_b_in"], w["cf_w_dw"], w["cf_b_dw"], w["cf_ln_g"], w["cf_ln_b"],
      w["cf_w_out"], final_g, st)


def kernel(x_prompt, x_sample, c_prompt, c_sample, state_mlstm_C, state_mlstm_n, state_mlstm_m, state_mlstm_conv, state_conf_conv, norm_g, w_ada, b_ada, ml_w_in, ml_w_conv, ml_b_conv, ml_w_q, ml_w_k, ml_w_v, ml_w_ig, ml_b_ig, ml_w_fg, ml_b_fg, ml_ln_g, ml_skip, ml_w_out, cf_w_in, cf_b_in, cf_w_dw, cf_b_dw, cf_ln_g, cf_ln_b, cf_w_out, final_g):
    B = x_prompt.shape[0]
    R = x_sample.shape[0]
    D = D_MODEL

    gate_pad = LANES - 2 * HEADS
    w = {
        "ml_w_in": _bf(ml_w_in[0]),
        "ml_w_conv": ml_w_conv[0],
        "ml_b_conv": ml_b_conv,
        "ml_w_q": _bf(ml_w_q[0]),
        "ml_w_k": _bf(ml_w_k[0]),
        "ml_w_v": _bf(ml_w_v[0]),
        "ml_w_igf": _bf(jnp.pad(jnp.concatenate([ml_w_ig[0], ml_w_fg[0]], axis=1), ((0, 0), (0, gate_pad)))),
        "ml_b_igf": jnp.pad(jnp.concatenate([ml_b_ig, ml_b_fg], axis=1), ((0, 0), (0, gate_pad))),
        "ml_ln_g": ml_ln_g.reshape(1, INNER),
        "ml_skip": ml_skip,
        "ml_w_out": _bf(ml_w_out[0]),
        "cf_w_in": _bf(cf_w_in[0]),
        "cf_b_in": cf_b_in,
        "cf_w_dw": cf_w_dw[0],
        "cf_b_dw": cf_b_dw,
        "cf_ln_g": cf_ln_g,
        "cf_ln_b": cf_ln_b,
        "cf_w_out": _bf(cf_w_out[0]),
    }
    final_g2 = final_g.reshape(1, D)

    mod = _ada(jnp.concatenate([c_prompt, c_sample], axis=0), w_ada, b_ada)
    mod4 = mod.reshape(mod.shape[0], B + R, 1, 3 * D)
    mod_s = mod[:, B:]

    tap_major = lambda a: jnp.transpose(a, (1, 0, 2))
    xs = x_sample.reshape(R, D)
    m_pad = jnp.pad(state_mlstm_m[0], ((0, 0), (0, LANES - HEADS)))
    (q_s, kw_s, v_s, xc_s, z_s, n_s, a1, a2, dec, m_s, mconv_s) = _ml_front_sample(
        xs, mod_s[0], norm_g[0:1], w, tap_major(state_mlstm_conv[0]), state_mlstm_n[0].reshape(R, INNER), m_pad)

    x1, mconv_p, C_p, n_p, m_p = _ml_prompt(x_prompt, mod4, norm_g[0:1], w)
    y_prompt, cconv_p, C_s, qc = _cf_prompt(
        x1, mod4, norm_g[1:2], w, final_g2, dec[:, :HEADS].reshape(R * HEADS), state_mlstm_C[0], q_s, kw_s, v_s)

    x1_s = _ml_back_sample(a1, a2, v_s, qc, xc_s, z_s, xs, mod_s[0], w)
    y_sample, cconv_s = _cf_sample(x1_s, mod_s[1], norm_g[1:2], w, final_g2, tap_major(state_conf_conv[0]))

    return (
        y_prompt,
        y_sample.reshape(R, 1, D),
        C_p[None],
        C_s[None],
        n_p[None],
        n_s.reshape(1, R, HEADS, DH),
        m_p[:, 0, :HEADS][None],
        m_s[:, :HEADS][None],
        mconv_p[None],
        tap_major(mconv_s)[None],
        cconv_p[None],
        tap_major(cconv_s)[None],
    )
```

```python
import functools

import jax
import jax.numpy as jnp
from jax import lax
from jax.experimental import pallas as pl
from jax.experimental.pallas import tpu as pltpu

D_MODEL = 1024
INNER = 2 * D_MODEL
HEADS = 4
DH = INNER // HEADS
MCONV = 4
CCONV = 31
EPS = 1e-6
NEG = -1e30
K_SCALE = DH ** -0.5

LANES = 128
SUBLANES = 8
VMEM_LIMIT = 56 * 2 ** 20

ROW_TILE = 128
CHUNK = 256
STEP_BATCH = 8
CONV_ROWS = 64
CONV_COLS = 256
CCONV_HIST = 32
SAMPLE_TILE = 16
W_COLS = 512

f32 = jnp.float32
bf16 = jnp.bfloat16


def _bf(x):
    return x.astype(bf16)


def _dot(a, b):
    return jnp.dot(a, b, preferred_element_type=f32)


def _col_chunks(w):
    k, n = w.shape
    return _bf(w).reshape(k, n // W_COLS, W_COLS).transpose(1, 0, 2)


def _dot_cols(a, w_ref):
    return jnp.concatenate([_dot(a, w_ref[c]) for c in range(w_ref.shape[0])], axis=1)


def _chunk_shape(k, n):
    return (n // W_COLS, k, W_COLS)


def _sigmoid(x):
    return 0.5 * jnp.tanh(0.5 * x) + 0.5


def _silu(x):
    h = 0.5 * x
    return h * jnp.tanh(h) + h


def _log_sigmoid(x):
    return jnp.minimum(x, 0.0) - jnp.log1p(jnp.exp(-jnp.abs(x)))


def _rows_op(op, x, r):
    m, n = x.shape
    if r.shape[0] in (1, m):
        return op(x, r)
    return op(x.reshape(m // SUBLANES, SUBLANES, n), r[None]).reshape(m, n)


def _mul_rows(x, r):
    return _rows_op(jnp.multiply, x, r)


def _add_rows(x, r):
    return _rows_op(jnp.add, x, r)


def _rep(row):
    return jnp.broadcast_to(row.reshape(1, -1), (SUBLANES, row.size))


def _rms(x, g):
    return _mul_rows(x * lax.rsqrt(jnp.mean(x * x, axis=-1, keepdims=True) + EPS), g)


def _rms_mod(x, g, mod):
    return _add_rows(_mul_rows(_rms(x, g), 1.0 + mod[:, D_MODEL:2 * D_MODEL]), mod[:, :D_MODEL])


def _layernorm(x, g, b=None):
    mu = jnp.mean(x, axis=-1, keepdims=True)
    xc = x - mu
    var = jnp.mean(xc * xc, axis=-1, keepdims=True)
    y = _mul_rows(xc * lax.rsqrt(var + EPS), g)
    return y if b is None else _add_rows(y, b)


def _const_spec(shape):
    n = len(shape)
    return pl.BlockSpec(shape, lambda *_: (0,) * n, pipeline_mode=pl.Buffered(1))


def _params(*sem):
    return pltpu.CompilerParams(dimension_semantics=sem, vmem_limit_bytes=VMEM_LIMIT)


def _ada_kernel(c_ref, w_ref, b_ref, o_ref):
    o_ref[...] = _dot(_bf(_silu(c_ref[...])), _bf(w_ref[...])) + b_ref[...]


def _ada(c_all, w_ada, b_ada):
    depth, d, d3 = w_ada.shape
    rows = c_all.shape[0]
    tn = D_MODEL
    return pl.pallas_call(
        _ada_kernel,
        grid=(depth, d3 // tn),
        in_specs=[
            pl.BlockSpec((rows, d), lambda i, j: (0, 0)),
            pl.BlockSpec((None, d, tn), lambda i, j: (i, 0, j)),
            pl.BlockSpec((None, 1, tn), lambda i, j: (i, 0, j)),
        ],
        out_specs=pl.BlockSpec((None, rows, tn), lambda i, j: (i, 0, j)),
        out_shape=jax.ShapeDtypeStruct((depth, rows, d3), f32),
        compiler_params=_params("arbitrary", "arbitrary"),
        name="ada",
    )(c_all, w_ada, b_ada.reshape(depth, 1, d3))


def _ml_qkv(xm, xc, wq_ref, wk_ref, wv_ref, wigf_ref, bigf_ref):
    xcb, xmb = _bf(xc), _bf(xm)
    qs, ks, vs = [], [], []
    for h in range(HEADS):
        seg = slice(h * DH, (h + 1) * DH)
        qs.append(_dot(xcb[:, seg], wq_ref[h]))
        ks.append(_dot(xcb[:, seg], wk_ref[h]))
        vs.append(_dot(xmb[:, seg], wv_ref[h]))
    q = jnp.concatenate(qs, axis=1)
    k = jnp.concatenate(ks, axis=1)
    v = jnp.concatenate(vs, axis=1)
    pre = (_dot(_bf(q), wigf_ref[0:INNER, :]) + _dot(_bf(k), wigf_ref[INNER:2 * INNER, :])
           + _dot(_bf(v), wigf_ref[2 * INNER:3 * INNER, :]) + bigf_ref[...])
    lane = lax.broadcasted_iota(jnp.int32, pre.shape, 1)
    gates = jnp.where(lane >= HEADS, _log_sigmoid(pre), pre)
    return q, k, v, gates


def _ml_back(hh, xc, z, x, mod, skip, wout_ref):
    act = (hh + _mul_rows(xc, skip)) * _silu(z)
    return x + _mul_rows(_dot_cols(_bf(act), wout_ref), mod[:, 2 * D_MODEL:])


def _tap(wconv_ref, j):
    return wconv_ref[SUBLANES * j:SUBLANES * (j + 1), :]


def _ml_conv_prompt(xm, wconv_ref, bconv_ref, xbuf_ref, tm):
    xbuf_ref[SUBLANES:SUBLANES + tm, :] = xm
    acc = _add_rows(_mul_rows(xm, _tap(wconv_ref, MCONV - 1)), bconv_ref[...])
    for j in range(MCONV - 1):
        start = SUBLANES - (MCONV - 1) + j
        acc = acc + _mul_rows(xbuf_ref[start:start + tm, :], _tap(wconv_ref, j))
    return acc


def _mlstm_chunk(q, ks, v, gates, lng_ref, C_ref, n_ref, m_ref):
    L = q.shape[0]
    row = lax.broadcasted_iota(jnp.int32, (L, L), 0)
    col = lax.broadcasted_iota(jnp.int32, (L, L), 1)
    causal = row >= col
    tri = jnp.where(causal, 1.0, 0.0).astype(bf16)
    hi = _bf(gates)
    lo = _bf(gates - hi.astype(f32))
    bcum = _dot(tri, hi) + _dot(tri, lo)
    gates_t = gates.T
    bcum_t = bcum.T

    hs = []
    for h in range(HEADS):
        seg = slice(h * DH, (h + 1) * DH)
        ig_col = gates[:, h:h + 1]
        b_col = bcum[:, HEADS + h:HEADS + h + 1]
        ig_row = gates_t[h:h + 1, :]
        b_row = bcum_t[HEADS + h:HEADS + h + 1, :]
        m_prev = m_ref[:, h:h + 1]
        qh, kh, vh = q[:, seg], ks[:, seg], v[:, seg]

        dmat = jnp.where(causal, b_col - b_row + ig_row, NEG)
        inter = b_col + m_prev
        m_t = jnp.maximum(inter, jnp.max(dmat, axis=1, keepdims=True))
        w_intra = jnp.exp(dmat - m_t)
        w_inter = jnp.exp(inter - m_t)
        s = lax.dot_general(qh, kh, (((1,), (1,)), ((), ())), preferred_element_type=f32) * w_intra
        num = _dot(_bf(s), vh) + w_inter * _dot(qh, _bf(C_ref[h]))
        qn = jnp.sum(qh.astype(f32) * n_ref[h:h + 1, :], axis=1, keepdims=True)
        den = jnp.sum(s, axis=1, keepdims=True) + w_inter * qn
        hc = num / jnp.maximum(jnp.abs(den), jnp.exp(-m_t))
        hs.append(_layernorm(hc, lng_ref[:, seg]))

        m_new = m_t[L - 1:L, :]
        b_last = b_col[L - 1:L, :]
        decay = jnp.exp(b_last + m_prev - m_new)
        kw = kh.astype(f32) * jnp.exp(b_last - b_col + ig_col - m_new)
        C_ref[h] = decay * C_ref[h] + lax.dot_general(_bf(kw), vh, (((0,), (0,)), ((), ())),
                                                      preferred_element_type=f32)
        n_ref[h:h + 1, :] = decay * n_ref[h:h + 1, :] + jnp.sum(kw, axis=0, keepdims=True)
        m_ref[:, h:h + 1] = m_new
    return jnp.concatenate(hs, axis=1)


def _ml_prompt_kernel(x_ref, mod_ref, g_ref, win_ref, wconv_ref, bconv_ref, wq_ref, wk_ref, wv_ref, wigf_ref,
                      bigf_ref, lng_ref, skip_ref, wout_ref,
                      o_ref, mconv_ref, C_ref, n_ref, m_ref, xbuf_ref, *, tm):
    @pl.when(pl.program_id(1) == 0)
    def _():
        xbuf_ref[0:SUBLANES, :] = jnp.zeros((SUBLANES, INNER), f32)
        C_ref[...] = jnp.zeros_like(C_ref)
        n_ref[...] = jnp.zeros_like(n_ref)
        m_ref[...] = jnp.zeros_like(m_ref)

    x = x_ref[...]
    mod = mod_ref[...]
    xmz = _dot_cols(_bf(_rms_mod(x, g_ref[...], mod)), win_ref)
    xm = xmz[:, :INNER]
    z = xmz[:, INNER:]
    xc = _silu(_ml_conv_prompt(xm, wconv_ref, bconv_ref, xbuf_ref, tm))
    mconv_ref[...] = xbuf_ref[SUBLANES + tm - (MCONV - 1):SUBLANES + tm, :]
    xbuf_ref[0:SUBLANES, :] = xbuf_ref[tm:tm + SUBLANES, :]

    q, k, v, gates = _ml_qkv(xm, xc, wq_ref, wk_ref, wv_ref, wigf_ref, bigf_ref)
    hh = _mlstm_chunk(_bf(q), _bf(k * K_SCALE), _bf(v), gates, lng_ref, C_ref, n_ref, m_ref)
    o_ref[...] = _ml_back(hh, xc, z, x, mod, skip_ref[...], wout_ref)


def _ml_prompt(x, mod4, g, w):
    B, T, D = x.shape
    tm = CHUNK
    row = lambda b, t: (b, t, 0)
    return pl.pallas_call(
        functools.partial(_ml_prompt_kernel, tm=tm),
        grid=(B, T // tm),
        in_specs=[
            pl.BlockSpec((None, tm, D), row),
            pl.BlockSpec((None, None, SUBLANES, 3 * D), lambda b, t: (0, b, 0, 0)),
            _const_spec((SUBLANES, D)),
            _const_spec(_chunk_shape(D, 2 * INNER)),
            _const_spec((MCONV * SUBLANES, INNER)),
            _const_spec((SUBLANES, INNER)),
            _const_spec((HEADS, DH, DH)),
            _const_spec((HEADS, DH, DH)),
            _const_spec((HEADS, DH, DH)),
            _const_spec((3 * INNER, LANES)),
            _const_spec((1, LANES)),
            _const_spec((SUBLANES, INNER)),
            _const_spec((SUBLANES, INNER)),
            _const_spec(_chunk_shape(INNER, D)),
        ],
        out_specs=[
            pl.BlockSpec((None, tm, D), row),
            pl.BlockSpec((None, MCONV - 1, INNER), lambda b, t: (b, 0, 0)),
            pl.BlockSpec((None, HEADS, DH, DH), lambda b, t: (b, 0, 0, 0)),
            pl.BlockSpec((None, HEADS, DH), lambda b, t: (b, 0, 0)),
            pl.BlockSpec((None, 1, LANES), lambda b, t: (b, 0, 0)),
        ],
        out_shape=[
            jax.ShapeDtypeStruct((B, T, D), f32),
            jax.ShapeDtypeStruct((B, MCONV - 1, INNER), f32),
            jax.ShapeDtypeStruct((B, HEADS, DH, DH), f32),
            jax.ShapeDtypeStruct((B, HEADS, DH), f32),
            jax.ShapeDtypeStruct((B, 1, LANES), f32),
        ],
        scratch_shapes=[pltpu.VMEM((SUBLANES + tm, INNER), f32)],
        compiler_params=_params("arbitrary", "arbitrary"),
        name="ml_prompt",
    )(x, mod4, g, w["ml_w_in"], w["ml_w_conv"], w["ml_b_conv"], w["ml_w_q"], w["ml_w_k"], w["ml_w_v"],
      w["ml_w_igf"], w["ml_b_igf"], w["ml_ln_g"], w["ml_skip"], w["ml_w_out"])


def _ml_front_sample_kernel(x_ref, mod_ref, g_ref, win_ref, wconv_ref, bconv_ref, wq_ref, wk_ref, wv_ref,
                            wigf_ref, bigf_ref, st_ref, n_ref, m_ref,
                            q_ref, kw_ref, v_ref, xc_ref, z_ref, nnew_ref, a1_ref, a2_ref, dec_ref, mnew_ref,
                            mconv_ref):
    h = _rms_mod(x_ref[...], g_ref[...], mod_ref[...])
    xmz = _dot_cols(_bf(h), win_ref)
    xm = xmz[:, :INNER]
    z_ref[...] = xmz[:, INNER:]
    acc = _add_rows(_mul_rows(xm, _tap(wconv_ref, MCONV - 1)), bconv_ref[...])
    for j in range(MCONV - 1):
        acc = acc + _mul_rows(st_ref[j], _tap(wconv_ref, j))
    xc = _silu(acc)
    xc_ref[...] = xc
    mconv_ref[0:MCONV - 2] = st_ref[1:MCONV - 1]
    mconv_ref[MCONV - 2] = xm

    q, k, v, gates = _ml_qkv(xm, xc, wq_ref, wk_ref, wv_ref, wigf_ref, bigf_ref)
    ks = k * K_SCALE
    q_ref[...] = q
    v_ref[...] = v

    ig = gates
    lf = pltpu.roll(gates, LANES - HEADS, axis=1)
    inter = lf + m_ref[...]
    m_new = jnp.maximum(inter, ig)
    w_intra = jnp.exp(ig - m_new)
    w_inter = jnp.exp(inter - m_new)
    lane = lax.broadcasted_iota(jnp.int32, ig.shape, 1)
    qk = jnp.zeros_like(ig)
    qn = jnp.zeros_like(ig)
    for h in range(HEADS):
        seg = slice(h * DH, (h + 1) * DH)
        qk = jnp.where(lane == h, jnp.sum(q[:, seg] * ks[:, seg], axis=1, keepdims=True), qk)
        qn = jnp.where(lane == h, jnp.sum(q[:, seg] * n_ref[:, seg], axis=1, keepdims=True), qn)
        kw = w_intra[:, h:h + 1] * ks[:, seg]
        kw_ref[:, seg] = kw
        nnew_ref[:, seg] = w_inter[:, h:h + 1] * n_ref[:, seg] + kw
    s = qk * w_intra
    den = s + w_inter * qn
    dn = jnp.maximum(jnp.abs(den), jnp.exp(-m_new))
    valid = lane < HEADS
    a1_ref[...] = jnp.where(valid, s / dn, 0.0)
    a2_ref[...] = jnp.where(valid, w_inter / dn, 0.0)
    dec_ref[...] = jnp.where(valid, w_inter, 0.0)
    mnew_ref[...] = jnp.where(valid, m_new, 0.0)


def _ml_front_sample(x, mod, g, w, st, n, m):
    R, D = x.shape
    full = lambda shape: pl.BlockSpec(shape, lambda i: (0,) * len(shape))
    act = jax.ShapeDtypeStruct((R, INNER), f32)
    small = jax.ShapeDtypeStruct((R, LANES), f32)
    return pl.pallas_call(
        _ml_front_sample_kernel,
        grid=(1,),
        in_specs=[
            full((R, D)), full((R, 3 * D)), full((1, D)), full(_chunk_shape(D, 2 * INNER)),
            full((MCONV * SUBLANES, INNER)),
            full((1, INNER)), full((HEADS, DH, DH)), full((HEADS, DH, DH)), full((HEADS, DH, DH)),
            full((3 * INNER, LANES)), full((1, LANES)),
            full((MCONV - 1, R, INNER)), full((R, INNER)), full((R, LANES)),
        ],
        out_specs=[full((R, INNER))] * 6 + [full((R, LANES))] * 4 + [full((MCONV - 1, R, INNER))],
        out_shape=[act] * 6 + [small] * 4 + [jax.ShapeDtypeStruct((MCONV - 1, R, INNER), f32)],
        compiler_params=_params("arbitrary"),
        name="ml_front_sample",
    )(x, mod, g, w["ml_w_in"], w["ml_w_conv"], w["ml_b_conv"], w["ml_w_q"], w["ml_w_k"], w["ml_w_v"],
      w["ml_w_igf"], w["ml_b_igf"], st, n, m)


def _ml_back_sample_kernel(a1_ref, a2_ref, v_ref, qc_ref, lng_ref, xc_ref, z_ref, x_ref, mod_ref, skip_ref,
                           wout_ref, o_ref):
    hs = []
    for h in range(HEADS):
        seg = slice(h * DH, (h + 1) * DH)
        hc = a1_ref[:, h:h + 1] * v_ref[:, seg] + a2_ref[:, h:h + 1] * qc_ref[:, seg]
        hs.append(_layernorm(hc, lng_ref[:, seg]))
    hh = jnp.concatenate(hs, axis=1)
    o_ref[...] = _ml_back(hh, xc_ref[...], z_ref[...], x_ref[...], mod_ref[...], skip_ref[...], wout_ref)


def _ml_back_sample(a1, a2, v, qc, xc, z, x, mod, w):
    R, D = x.shape
    full = lambda shape: pl.BlockSpec(shape, lambda i: (0,) * len(shape))
    return pl.pallas_call(
        _ml_back_sample_kernel,
        grid=(1,),
        in_specs=[full((R, LANES)), full((R, LANES)), full((R, INNER)), full((R, INNER)), full((1, INNER)),
                  full((R, INNER)), full((R, INNER)), full((R, D)), full((R, 3 * D)), full((1, INNER)),
                  full(_chunk_shape(INNER, D))],
        out_specs=full((R, D)),
        out_shape=jax.ShapeDtypeStruct((R, D), f32),
        compiler_params=_params("arbitrary"),
        name="ml_back_sample",
    )(a1, a2, v, qc, w["ml_ln_g"], xc, z, x, mod, w["ml_skip"], w["ml_w_out"])


def _cf_in(x, mod, g_ref, win_ref, bin_ref):
    h = _rms_mod(x, g_ref[...], mod)
    agz = _add_rows(_dot_cols(_bf(h), win_ref), bin_ref[...])
    u = agz[:, :INNER] * _sigmoid(agz[:, INNER:2 * INNER])
    return u, agz[:, 2 * INNER:]


def _cf_out(y, z, x, mod, lng_ref, lnb_ref, wout_ref, fg_ref):
    yn = _layernorm(y, lng_ref[...], lnb_ref[...])
    out = _dot_cols(_bf(_silu(yn) * _silu(z)), wout_ref)
    return _rms(x + _mul_rows(out, mod[:, 2 * D_MODEL:]), fg_ref[...])


def _dwconv_tile(ubuf_ref, sh_ref, wb_ref, bdw_ref, y_ref, tm):
    base = CCONV_HIST - (CCONV - 1)
    max_row_off = (base + CCONV - 2) // SUBLANES * SUBLANES

    def col_body(ci, carry):
        cols = pl.ds(pl.multiple_of(ci * CONV_COLS, CONV_COLS), CONV_COLS)
        for r in range(1, SUBLANES):
            sh_ref[r - 1] = ubuf_ref[r:r + tm + max_row_off, cols]
        b = bdw_ref[:, cols]
        for rc in range(tm // CONV_ROWS):
            acc = jnp.broadcast_to(b, (CONV_ROWS, CONV_COLS))
            for j in range(CCONV):
                a, r = divmod(base + j, SUBLANES)
                rows = pl.ds(SUBLANES * a + CONV_ROWS * rc, CONV_ROWS)
                src = ubuf_ref[rows, cols] if r == 0 else sh_ref[r - 1, rows, :]
                acc = acc + _mul_rows(src, wb_ref[SUBLANES * j:SUBLANES * (j + 1), cols])
            y_ref[pl.ds(CONV_ROWS * rc, CONV_ROWS), cols] = acc
        return carry

    lax.fori_loop(0, INNER // CONV_COLS, col_body, 0)


def _memory_step(step, dec_ref, c_ref, q_ref, kw_ref, v_ref, cout_ref):
    head = (step // 2) % HEADS
    first = (step // (2 * HEADS)) * STEP_BATCH
    qt = q_ref[...].T
    kwt = kw_ref[...].T
    parts = []
    for i in range(STEP_BATCH):
        decay = dec_ref[(first + i) * HEADS + head]
        c = c_ref[i]
        parts.append(jnp.sum(c * qt[:, i:i + 1], axis=0, keepdims=True))
        cout_ref[i] = decay * c + kwt[:, i:i + 1] * v_ref[i:i + 1, :]
    return jnp.concatenate(parts, axis=0)


def _cf_prompt_kernel(x_ref, mod_ref, g_ref, win_ref, bin_ref, wdw_ref, bdw_ref, lng_ref, lnb_ref, wout_ref,
                      fg_ref, dec_ref, cs_ref, qs_ref, kws_ref, vs_ref,
                      o_ref, cconv_ref, csout_ref, qc_ref, ubuf_ref, y_ref, sh_ref, wb_ref, *, tm):
    step = pl.program_id(0) * pl.num_programs(1) + pl.program_id(1)

    @pl.when(step == 0)
    def _():
        for j in range(CCONV):
            wb_ref[SUBLANES * j:SUBLANES * (j + 1), :] = jnp.broadcast_to(wdw_ref[j:j + 1, :], (SUBLANES, INNER))

    @pl.when(pl.program_id(1) == 0)
    def _():
        ubuf_ref[0:CCONV_HIST, :] = jnp.zeros((CCONV_HIST, INNER), f32)

    qc = _memory_step(step, dec_ref, cs_ref, qs_ref, kws_ref, vs_ref, csout_ref)

    x = x_ref[...]
    mod = mod_ref[...]
    u, z = _cf_in(x, mod, g_ref, win_ref, bin_ref)
    ubuf_ref[CCONV_HIST:CCONV_HIST + tm, :] = u
    _dwconv_tile(ubuf_ref, sh_ref, wb_ref, bdw_ref, y_ref, tm)
    cconv_ref[...] = ubuf_ref[CCONV_HIST + tm - (CCONV - 1):CCONV_HIST + tm, :]
    ubuf_ref[0:CCONV_HIST, :] = ubuf_ref[tm:tm + CCONV_HIST, :]
    o_ref[...] = _cf_out(y_ref[...], z, x, mod, lng_ref, lnb_ref, wout_ref, fg_ref)

    @pl.when(step % 2 == 0)
    def _():
        qc_ref[...] = qc

    @pl.when(step % 2 == 1)
    def _():
        qc_ref[...] += qc


def _cf_prompt(x, mod4, g, w, final_g, dec_flat, C_s, q_s, kw_s, v_s):
    B, T, D = x.shape
    R = C_s.shape[0]
    tm = ROW_TILE
    steps_t = T // tm
    assert B * steps_t * STEP_BATCH == R * HEADS * 2, "one (sequence group, head, key half) per grid step"
    row = lambda b, t: (b, t, 0)

    def ghk(b, t):
        s = b * steps_t + t
        return s // (2 * HEADS), (s // 2) % HEADS, s % 2

    c_spec = pl.BlockSpec((STEP_BATCH, None, DH // 2, DH), lambda b, t: (*ghk(b, t), 0))
    half_spec = pl.BlockSpec((STEP_BATCH, DH // 2), lambda b, t: (ghk(b, t)[0], 2 * ghk(b, t)[1] + ghk(b, t)[2]))
    head_spec = pl.BlockSpec((STEP_BATCH, DH), lambda b, t: ghk(b, t)[:2])
    return pl.pallas_call(
        functools.partial(_cf_prompt_kernel, tm=tm),
        grid=(B, steps_t),
        in_specs=[
            pl.BlockSpec((None, tm, D), row),
            pl.BlockSpec((None, None, SUBLANES, 3 * D), lambda b, t: (1, b, 0, 0)),
            _const_spec((SUBLANES, D)),
            _const_spec(_chunk_shape(D, 3 * INNER)),
            _const_spec((SUBLANES, 3 * INNER)),
            _const_spec((CCONV, INNER)),
            _const_spec((1, INNER)),
            _const_spec((SUBLANES, INNER)),
            _const_spec((SUBLANES, INNER)),
            _const_spec(_chunk_shape(INNER, D)),
            _const_spec((SUBLANES, D)),
            pl.BlockSpec(memory_space=pltpu.SMEM),
            c_spec, half_spec, half_spec, head_spec,
        ],
        out_specs=[
            pl.BlockSpec((None, tm, D), row),
            pl.BlockSpec((None, CCONV - 1, INNER), lambda b, t: (b, 0, 0)),
            c_spec,
            head_spec,
        ],
        out_shape=[
            jax.ShapeDtypeStruct((B, T, D), f32),
            jax.ShapeDtypeStruct((B, CCONV - 1, INNER), f32),
            jax.ShapeDtypeStruct(C_s.shape, f32),
            jax.ShapeDtypeStruct((R, INNER), f32),
        ],
        scratch_shapes=[
            pltpu.VMEM((CCONV_HIST + tm, INNER), f32),
            pltpu.VMEM((tm, INNER), f32),
            pltpu.VMEM((SUBLANES - 1, tm + CCONV_HIST - SUBLANES, CONV_COLS), f32),
            pltpu.VMEM((SUBLANES * CCONV, INNER), f32),
        ],
        compiler_params=_params("arbitrary", "arbitrary"),
        name="cf_prompt",
    )(x, mod4, g, w["cf_w_in"], w["cf_b_in"], w["cf_w_dw"], w["cf_b_dw"], w["cf_ln_g"], w["cf_ln_b"],
      w["cf_w_out"], final_g, dec_flat, C_s, q_s, kw_s, v_s)


def _cf_sample_kernel(x_ref, mod_ref, g_ref, win_ref, bin_ref, wdw_ref, bdw_ref, lng_ref, lnb_ref, wout_ref,
                      fg_ref, st_ref, o_ref, stout_ref, u_ref, z_ref, y_ref, *, tb):
    i = pl.program_id(0)

    @pl.when(i == 0)
    def _():
        u, z = _cf_in(x_ref[...], mod_ref[...], g_ref, win_ref, bin_ref)
        u_ref[...] = u
        z_ref[...] = z

    rows = pl.ds(pl.multiple_of(i * tb, tb), tb)
    u = u_ref[rows, :]
    acc = bdw_ref[...] + wdw_ref[CCONV - 1:CCONV, :] * u
    for j in range(CCONV - 1):
        acc = acc + wdw_ref[j:j + 1, :] * st_ref[j]
    y_ref[rows, :] = acc
    stout_ref[0:CCONV - 2] = st_ref[1:CCONV - 1]
    stout_ref[CCONV - 2] = u

    @pl.when(i == pl.num_programs(0) - 1)
    def _():
        o_ref[...] = _cf_out(y_ref[...], z_ref[...], x_ref[...], mod_ref[...], lng_ref, lnb_ref, wout_ref, fg_ref)


def _cf_sample(x, mod, g, w, final_g, st):
    R, D = x.shape
    tb = SAMPLE_TILE
    state = pl.BlockSpec((CCONV - 1, tb, INNER), lambda i: (0, i, 0))
    return pl.pallas_call(
        functools.partial(_cf_sample_kernel, tb=tb),
        grid=(R // tb,),
        in_specs=[
            _const_spec((R, D)),
            _const_spec((R, 3 * D)),
            _const_spec((1, D)),
            _const_spec(_chunk_shape(D, 3 * INNER)),
            _const_spec((1, 3 * INNER)),
            _const_spec((CCONV, INNER)),
            _const_spec((1, INNER)),
            _const_spec((1, INNER)),
            _const_spec((1, INNER)),
            _const_spec(_chunk_shape(INNER, D)),
            _const_spec((1, D)),
            state,
        ],
        out_specs=[pl.BlockSpec((R, D), lambda i: (0, 0)), state],
        out_shape=[jax.ShapeDtypeStruct((R, D), f32), jax.ShapeDtypeStruct((CCONV - 1, R, INNER), f32)],
        scratch_shapes=[pltpu.VMEM((R, INNER), f32)] * 3,
        compiler_params=_params("arbitrary"),
        name="cf_sample",
    )(x, mod, g, w["cf_w_in"], w["cf_b_in"], w["cf_w_dw"], w["cf_b_dw"], w["cf_ln_g"], w["cf_ln_b"],
      w["cf_w_out"], final_g, st)


def kernel(x_prompt, x_sample, c_prompt, c_sample, state_mlstm_C, state_mlstm_n, state_mlstm_m, state_mlstm_conv, state_conf_conv, norm_g, w_ada, b_ada, ml_w_in, ml_w_conv, ml_b_conv, ml_w_q, ml_w_k, ml_w_v, ml_w_ig, ml_b_ig, ml_w_fg, ml_b_fg, ml_ln_g, ml_skip, ml_w_out, cf_w_in, cf_b_in, cf_w_dw, cf_b_dw, cf_ln_g, cf_ln_b, cf_w_out, final_g):
    B = x_prompt.shape[0]
    R = x_sample.shape[0]
    D = D_MODEL

    gate_pad = LANES - 2 * HEADS
    w = {
        "ml_w_in": _col_chunks(ml_w_in[0]),
        "ml_w_conv": jnp.repeat(ml_w_conv[0], SUBLANES, axis=0),
        "ml_b_conv": ml_b_conv,
        "ml_w_q": _bf(ml_w_q[0]),
        "ml_w_k": _bf(ml_w_k[0]),
        "ml_w_v": _bf(ml_w_v[0]),
        "ml_w_igf": _bf(jnp.pad(jnp.concatenate([ml_w_ig[0], ml_w_fg[0]], axis=1), ((0, 0), (0, gate_pad)))),
        "ml_b_igf": jnp.pad(jnp.concatenate([ml_b_ig, ml_b_fg], axis=1), ((0, 0), (0, gate_pad))),
        "ml_ln_g": ml_ln_g.reshape(1, INNER),
        "ml_skip": ml_skip,
        "ml_w_out": _col_chunks(ml_w_out[0]),
        "cf_w_in": _col_chunks(cf_w_in[0]),
        "cf_b_in": cf_b_in,
        "cf_w_dw": cf_w_dw[0],
        "cf_b_dw": cf_b_dw,
        "cf_ln_g": cf_ln_g,
        "cf_ln_b": cf_ln_b,
        "cf_w_out": _col_chunks(cf_w_out[0]),
    }
    final_g2 = final_g.reshape(1, D)
    wp = dict(w, **{name: _rep(w[name]) for name in
                    ("ml_b_conv", "ml_ln_g", "ml_skip", "cf_b_in", "cf_ln_g", "cf_ln_b")})

    mod = _ada(jnp.concatenate([c_prompt, c_sample], axis=0), w_ada, b_ada)
    mod_p = jnp.broadcast_to(mod[:, :B, None, :], (mod.shape[0], B, SUBLANES, 3 * D))
    mod_s = mod[:, B:]

    tap_major = lambda a: jnp.transpose(a, (1, 0, 2))
    xs = x_sample.reshape(R, D)
    m_pad = jnp.pad(state_mlstm_m[0], ((0, 0), (0, LANES - HEADS)))
    (q_s, kw_s, v_s, xc_s, z_s, n_s, a1, a2, dec, m_s, mconv_s) = _ml_front_sample(
        xs, mod_s[0], norm_g[0:1], w, tap_major(state_mlstm_conv[0]), state_mlstm_n[0].reshape(R, INNER), m_pad)

    x1, mconv_p, C_p, n_p, m_p = _ml_prompt(x_prompt, mod_p, _rep(norm_g[0]), wp)
    y_prompt, cconv_p, C_s, qc = _cf_prompt(
        x1, mod_p, _rep(norm_g[1]), wp, _rep(final_g), dec[:, :HEADS].reshape(R * HEADS), state_mlstm_C[0],
        q_s, kw_s, v_s)

    x1_s = _ml_back_sample(a1, a2, v_s, qc, xc_s, z_s, xs, mod_s[0], w)
    y_sample, cconv_s = _cf_sample(x1_s, mod_s[1], norm_g[1:2], w, final_g2, tap_major(state_conf_conv[0]))

    return (
        y_prompt,
        y_sample.reshape(R, 1, D),
        C_p[None],
        C_s[None],
        n_p[None],
        n_s.reshape(1, R, HEADS, DH),
        m_p[:, 0, :HEADS][None],
        m_s[:, :HEADS][None],
        mconv_p[None],
        tap_major(mconv_s)[None],
        cconv_p[None],
        tap_major(cconv_s)[None],
    )
```

```python
import functools

import jax
import jax.numpy as jnp
from jax import lax
from jax.experimental import pallas as pl
from jax.experimental.pallas import tpu as pltpu

D_MODEL = 1024
INNER = 2 * D_MODEL
HEADS = 4
DH = INNER // HEADS
MCONV = 4
CCONV = 31
EPS = 1e-6
NEG = -1e30
K_SCALE = DH ** -0.5

LANES = 128
SUBLANES = 8
VMEM_LIMIT = 56 * 2 ** 20

ROW_TILE = 256
CHUNK = 256
STEP_BATCH = 8
CONV_ROWS = 64
CONV_COLS = 256
CCONV_HIST = 32
SAMPLE_TILE = 16
W_COLS = 512

f32 = jnp.float32
bf16 = jnp.bfloat16


def _bf(x):
    return x.astype(bf16)


def _dot(a, b):
    return jnp.dot(a, b, preferred_element_type=f32)


def _col_chunks(w, cols=W_COLS):
    k, n = w.shape
    return _bf(w).reshape(k, n // cols, cols).transpose(1, 0, 2)


def _dot_cols(a, w_ref):
    return jnp.concatenate([_dot(a, w_ref[c]) for c in range(w_ref.shape[0])], axis=1)


def _chunk_shape(k, n, cols=W_COLS):
    return (n // cols, k, cols)


def _sigmoid(x):
    return 0.5 * jnp.tanh(0.5 * x) + 0.5


def _silu(x):
    h = 0.5 * x
    return h * jnp.tanh(h) + h


def _log_sigmoid(x):
    return jnp.minimum(x, 0.0) - jnp.log1p(jnp.exp(-jnp.abs(x)))


def _rows_op(op, x, r):
    m, n = x.shape
    if r.shape[0] in (1, m):
        return op(x, r)
    return op(x.reshape(m // SUBLANES, SUBLANES, n), r[None]).reshape(m, n)


def _mul_rows(x, r):
    return _rows_op(jnp.multiply, x, r)


def _add_rows(x, r):
    return _rows_op(jnp.add, x, r)


def _rep(row):
    return jnp.broadcast_to(row.reshape(1, -1), (SUBLANES, row.size))


def _rms(x, g):
    return _mul_rows(x * lax.rsqrt(jnp.mean(x * x, axis=-1, keepdims=True) + EPS), g)


def _rms_mod(x, g, mod):
    return _add_rows(_mul_rows(_rms(x, g), 1.0 + mod[:, D_MODEL:2 * D_MODEL]), mod[:, :D_MODEL])


def _layernorm(x, g, b=None):
    mu = jnp.mean(x, axis=-1, keepdims=True)
    xc = x - mu
    var = jnp.mean(xc * xc, axis=-1, keepdims=True)
    y = _mul_rows(xc * lax.rsqrt(var + EPS), g)
    return y if b is None else _add_rows(y, b)


def _const_spec(shape):
    n = len(shape)
    return pl.BlockSpec(shape, lambda *_: (0,) * n, pipeline_mode=pl.Buffered(1))


def _params(*sem):
    return pltpu.CompilerParams(dimension_semantics=sem, vmem_limit_bytes=VMEM_LIMIT)


def _ada_kernel(c_ref, w_ref, b_ref, o_ref):
    o_ref[...] = _dot(_bf(_silu(c_ref[...])), _bf(w_ref[...])) + b_ref[...]


def _ada(c_all, w_ada, b_ada):
    depth, d, d3 = w_ada.shape
    rows = c_all.shape[0]
    tn = D_MODEL
    return pl.pallas_call(
        _ada_kernel,
        grid=(depth, d3 // tn),
        in_specs=[
            pl.BlockSpec((rows, d), lambda i, j: (0, 0)),
            pl.BlockSpec((None, d, tn), lambda i, j: (i, 0, j)),
            pl.BlockSpec((None, 1, tn), lambda i, j: (i, 0, j)),
        ],
        out_specs=pl.BlockSpec((None, rows, tn), lambda i, j: (i, 0, j)),
        out_shape=jax.ShapeDtypeStruct((depth, rows, d3), f32),
        compiler_params=_params("arbitrary", "arbitrary"),
        name="ada",
    )(c_all, w_ada, b_ada.reshape(depth, 1, d3))


def _ml_proj(xm, xc, wq_ref, wk_ref, wv_ref):
    xcb, xmb = _bf(xc), _bf(xm)
    qs, ks, vs = [], [], []
    for h in range(HEADS):
        seg = slice(h * DH, (h + 1) * DH)
        qs.append(_dot(xcb[:, seg], wq_ref[h]))
        ks.append(_dot(xcb[:, seg], wk_ref[h]))
        vs.append(_dot(xmb[:, seg], wv_ref[h]))
    return jnp.concatenate(qs, axis=1), jnp.concatenate(ks, axis=1), jnp.concatenate(vs, axis=1)


def _ml_qkv(xm, xc, wq_ref, wk_ref, wv_ref, wigf_ref, bigf_ref):
    q, k, v = _ml_proj(xm, xc, wq_ref, wk_ref, wv_ref)
    pre = (_dot(_bf(q), wigf_ref[0:INNER, :]) + _dot(_bf(k), wigf_ref[INNER:2 * INNER, :])
           + _dot(_bf(v), wigf_ref[2 * INNER:3 * INNER, :]) + bigf_ref[...])
    lane = lax.broadcasted_iota(jnp.int32, pre.shape, 1)
    gates = jnp.where(lane >= HEADS, _log_sigmoid(pre), pre)
    return q, k, v, gates


def _ml_act(hh, xc, z, skip):
    return _bf((hh + _mul_rows(xc, skip)) * _silu(z))


def _ml_residual(act, x, mod, wout_ref):
    return x + _mul_rows(_dot_cols(act, wout_ref), mod[:, 2 * D_MODEL:])


def _tap(wconv_ref, j):
    return wconv_ref[SUBLANES * j:SUBLANES * (j + 1), :]


def _ml_conv_prompt(xm, wconv_ref, bconv_ref, xbuf_ref, tm):
    xbuf_ref[SUBLANES:SUBLANES + tm, :] = xm
    acc = _add_rows(_mul_rows(xm, _tap(wconv_ref, MCONV - 1)), bconv_ref[...])
    for j in range(MCONV - 1):
        start = SUBLANES - (MCONV - 1) + j
        acc = acc + _mul_rows(xbuf_ref[start:start + tm, :], _tap(wconv_ref, j))
    return acc


def _mlstm_chunk(q, ks, v, gates, lng_ref, C_ref, n_ref, m_ref):
    L = q.shape[0]
    row = lax.broadcasted_iota(jnp.int32, (L, L), 0)
    col = lax.broadcasted_iota(jnp.int32, (L, L), 1)
    causal = row >= col
    tri = jnp.where(causal, 1.0, 0.0).astype(bf16)
    hi = _bf(gates)
    lo = _bf(gates - hi.astype(f32))
    bcum = _dot(tri, hi) + _dot(tri, lo)
    gates_t = gates.T
    bcum_t = bcum.T

    hs = []
    for h in range(HEADS):
        seg = slice(h * DH, (h + 1) * DH)
        ig_col = gates[:, h:h + 1]
        b_col = bcum[:, HEADS + h:HEADS + h + 1]
        ig_row = gates_t[h:h + 1, :]
        b_row = bcum_t[HEADS + h:HEADS + h + 1, :]
        m_prev = m_ref[:, h:h + 1]
        qh, kh, vh = q[:, seg], ks[:, seg], v[:, seg]

        dmat = jnp.where(causal, b_col - b_row + ig_row, NEG)
        inter = b_col + m_prev
        m_t = jnp.maximum(inter, jnp.max(dmat, axis=1, keepdims=True))
        w_intra = jnp.exp(dmat - m_t)
        w_inter = jnp.exp(inter - m_t)
        s = lax.dot_general(qh, kh, (((1,), (1,)), ((), ())), preferred_element_type=f32) * w_intra
        num = _dot(_bf(s), vh) + w_inter * _dot(qh, _bf(C_ref[h]))
        qn = jnp.sum(qh.astype(f32) * n_ref[h:h + 1, :], axis=1, keepdims=True)
        den = jnp.sum(s, axis=1, keepdims=True) + w_inter * qn
        hc = num / jnp.maximum(jnp.abs(den), jnp.exp(-m_t))
        hs.append(_layernorm(hc, lng_ref[:, seg]))

        m_new = m_t[L - 1:L, :]
        b_last = b_col[L - 1:L, :]
        decay = jnp.exp(b_last + m_prev - m_new)
        kw = kh.astype(f32) * jnp.exp(b_last - b_col + ig_col - m_new)
        C_ref[h] = decay * C_ref[h] + lax.dot_general(_bf(kw), vh, (((0,), (0,)), ((), ())),
                                                      preferred_element_type=f32)
        n_ref[h:h + 1, :] = decay * n_ref[h:h + 1, :] + jnp.sum(kw, axis=0, keepdims=True)
        m_ref[:, h:h + 1] = m_new
    return jnp.concatenate(hs, axis=1)


def _ml_prompt_kernel(x_ref, mod_ref, g_ref, win_ref, wconv_ref, bconv_ref, wq_ref, wk_ref, wv_ref, wigf_ref,
                      bigf_ref, lng_ref, skip_ref, wout_ref,
                      o_ref, mconv_ref, C_ref, n_ref, m_ref, xbuf_ref, *, tm):
    @pl.when(pl.program_id(1) == 0)
    def _():
        xbuf_ref[0:SUBLANES, :] = jnp.zeros((SUBLANES, INNER), f32)
        C_ref[...] = jnp.zeros_like(C_ref)
        n_ref[...] = jnp.zeros_like(n_ref)
        m_ref[...] = jnp.zeros_like(m_ref)

    x = x_ref[...]
    mod = mod_ref[...]
    xmz = _dot_cols(_bf(_rms_mod(x, g_ref[...], mod)), win_ref)
    xm = xmz[:, :INNER]
    z = xmz[:, INNER:]
    xc = _silu(_ml_conv_prompt(xm, wconv_ref, bconv_ref, xbuf_ref, tm))
    mconv_ref[...] = xbuf_ref[SUBLANES + tm - (MCONV - 1):SUBLANES + tm, :]
    xbuf_ref[0:SUBLANES, :] = xbuf_ref[tm:tm + SUBLANES, :]

    q, k, v, gates = _ml_qkv(xm, xc, wq_ref, wk_ref, wv_ref, wigf_ref, bigf_ref)
    hh = _mlstm_chunk(_bf(q), _bf(k * K_SCALE), _bf(v), gates, lng_ref, C_ref, n_ref, m_ref)
    o_ref[...] = _ml_residual(_ml_act(hh, xc, z, skip_ref[...]), x, mod, wout_ref)


def _ml_prompt(x, mod4, g, w):
    B, T, D = x.shape
    tm = CHUNK
    row = lambda b, t: (b, t, 0)
    return pl.pallas_call(
        functools.partial(_ml_prompt_kernel, tm=tm),
        grid=(B, T // tm),
        in_specs=[
            pl.BlockSpec((None, tm, D), row),
            pl.BlockSpec((None, None, SUBLANES, 3 * D), lambda b, t: (0, b, 0, 0)),
            _const_spec((SUBLANES, D)),
            _const_spec(_chunk_shape(D, 2 * INNER)),
            _const_spec((MCONV * SUBLANES, INNER)),
            _const_spec((SUBLANES, INNER)),
            _const_spec((HEADS, DH, DH)),
            _const_spec((HEADS, DH, DH)),
            _const_spec((HEADS, DH, DH)),
            _const_spec((3 * INNER, LANES)),
            _const_spec((1, LANES)),
            _const_spec((SUBLANES, INNER)),
            _const_spec((SUBLANES, INNER)),
            _const_spec(_chunk_shape(INNER, D)),
        ],
        out_specs=[
            pl.BlockSpec((None, tm, D), row),
            pl.BlockSpec((None, MCONV - 1, INNER), lambda b, t: (b, 0, 0)),
            pl.BlockSpec((None, HEADS, DH, DH), lambda b, t: (b, 0, 0, 0)),
            pl.BlockSpec((None, HEADS, DH), lambda b, t: (b, 0, 0)),
            pl.BlockSpec((None, 1, LANES), lambda b, t: (b, 0, 0)),
        ],
        out_shape=[
            jax.ShapeDtypeStruct((B, T, D), f32),
            jax.ShapeDtypeStruct((B, MCONV - 1, INNER), f32),
            jax.ShapeDtypeStruct((B, HEADS, DH, DH), f32),
            jax.ShapeDtypeStruct((B, HEADS, DH), f32),
            jax.ShapeDtypeStruct((B, 1, LANES), f32),
        ],
        scratch_shapes=[pltpu.VMEM((SUBLANES + tm, INNER), f32)],
        compiler_params=_params("arbitrary", "arbitrary"),
        name="ml_prompt",
    )(x, mod4, g, w["ml_w_in"], w["ml_w_conv"], w["ml_b_conv"], w["ml_w_q"], w["ml_w_k"], w["ml_w_v"],
      w["ml_w_igf"], w["ml_b_igf"], w["ml_ln_g"], w["ml_skip"], w["ml_w_out"])


def _ml_front_sample_kernel(x_ref, mod_ref, g_ref, win_ref, wconv_ref, bconv_ref, wq_ref, wk_ref, wv_ref,
                            wigf_ref, bigf_ref, st_ref, n_ref, m_ref,
                            q_ref, kw_ref, v_ref, xc_ref, z_ref, nnew_ref, a1_ref, a2_ref, dec_ref, mnew_ref,
                            mconv_ref):
    h = _rms_mod(x_ref[...], g_ref[...], mod_ref[...])
    xmz = _dot_cols(_bf(h), win_ref)
    xm = xmz[:, :INNER]
    z_ref[...] = xmz[:, INNER:]
    acc = _add_rows(_mul_rows(xm, _tap(wconv_ref, MCONV - 1)), bconv_ref[...])
    for j in range(MCONV - 1):
        acc = acc + _mul_rows(st_ref[j], _tap(wconv_ref, j))
    xc = _silu(acc)
    xc_ref[...] = xc
    mconv_ref[0:MCONV - 2] = st_ref[1:MCONV - 1]
    mconv_ref[MCONV - 2] = xm

    q, k, v, gates = _ml_qkv(xm, xc, wq_ref, wk_ref, wv_ref, wigf_ref, bigf_ref)
    ks = k * K_SCALE
    q_ref[...] = q
    v_ref[...] = v

    ig = gates
    lf = pltpu.roll(gates, LANES - HEADS, axis=1)
    inter = lf + m_ref[...]
    m_new = jnp.maximum(inter, ig)
    w_intra = jnp.exp(ig - m_new)
    w_inter = jnp.exp(inter - m_new)
    lane = lax.broadcasted_iota(jnp.int32, ig.shape, 1)
    qk = jnp.zeros_like(ig)
    qn = jnp.zeros_like(ig)
    for h in range(HEADS):
        seg = slice(h * DH, (h + 1) * DH)
        qk = jnp.where(lane == h, jnp.sum(q[:, seg] * ks[:, seg], axis=1, keepdims=True), qk)
        qn = jnp.where(lane == h, jnp.sum(q[:, seg] * n_ref[:, seg], axis=1, keepdims=True), qn)
        kw = w_intra[:, h:h + 1] * ks[:, seg]
        kw_ref[:, seg] = kw
        nnew_ref[:, seg] = w_inter[:, h:h + 1] * n_ref[:, seg] + kw
    s = qk * w_intra
    den = s + w_inter * qn
    dn = jnp.maximum(jnp.abs(den), jnp.exp(-m_new))
    valid = lane < HEADS
    a1_ref[...] = jnp.where(valid, s / dn, 0.0)
    a2_ref[...] = jnp.where(valid, w_inter / dn, 0.0)
    dec_ref[...] = jnp.where(valid, w_inter, 0.0)
    mnew_ref[...] = jnp.where(valid, m_new, 0.0)


def _ml_front_sample(x, mod, g, w, st, n, m):
    R, D = x.shape
    full = lambda shape: pl.BlockSpec(shape, lambda i: (0,) * len(shape))
    act = jax.ShapeDtypeStruct((R, INNER), f32)
    small = jax.ShapeDtypeStruct((R, LANES), f32)
    return pl.pallas_call(
        _ml_front_sample_kernel,
        grid=(1,),
        in_specs=[
            full((R, D)), full((R, 3 * D)), full((1, D)), full(_chunk_shape(D, 2 * INNER)),
            full((MCONV * SUBLANES, INNER)),
            full((1, INNER)), full((HEADS, DH, DH)), full((HEADS, DH, DH)), full((HEADS, DH, DH)),
            full((3 * INNER, LANES)), full((1, LANES)),
            full((MCONV - 1, R, INNER)), full((R, INNER)), full((R, LANES)),
        ],
        out_specs=[full((R, INNER))] * 6 + [full((R, LANES))] * 4 + [full((MCONV - 1, R, INNER))],
        out_shape=[act] * 6 + [small] * 4 + [jax.ShapeDtypeStruct((MCONV - 1, R, INNER), f32)],
        compiler_params=_params("arbitrary"),
        name="ml_front_sample",
    )(x, mod, g, w["ml_w_in"], w["ml_w_conv"], w["ml_b_conv"], w["ml_w_q"], w["ml_w_k"], w["ml_w_v"],
      w["ml_w_igf"], w["ml_b_igf"], st, n, m)


def _ml_back_sample_kernel(a1_ref, a2_ref, v_ref, qc_ref, lng_ref, xc_ref, z_ref, x_ref, mod_ref, skip_ref,
                           wout_ref, o_ref):
    hs = []
    for h in range(HEADS):
        seg = slice(h * DH, (h + 1) * DH)
        hc = a1_ref[:, h:h + 1] * v_ref[:, seg] + a2_ref[:, h:h + 1] * qc_ref[:, seg]
        hs.append(_layernorm(hc, lng_ref[:, seg]))
    hh = jnp.concatenate(hs, axis=1)
    act = _ml_act(hh, xc_ref[...], z_ref[...], skip_ref[...])
    o_ref[...] = _ml_residual(act, x_ref[...], mod_ref[...], wout_ref)


def _ml_back_sample(a1, a2, v, qc, xc, z, x, mod, w):
    R, D = x.shape
    full = lambda shape: pl.BlockSpec(shape, lambda i: (0,) * len(shape))
    return pl.pallas_call(
        _ml_back_sample_kernel,
        grid=(1,),
        in_specs=[full((R, LANES)), full((R, LANES)), full((R, INNER)), full((R, INNER)), full((1, INNER)),
                  full((R, INNER)), full((R, INNER)), full((R, D)), full((R, 3 * D)), full((1, INNER)),
                  full(_chunk_shape(INNER, D))],
        out_specs=full((R, D)),
        out_shape=jax.ShapeDtypeStruct((R, D), f32),
        compiler_params=_params("arbitrary"),
        name="ml_back_sample",
    )(a1, a2, v, qc, w["ml_ln_g"], xc, z, x, mod, w["ml_skip"], w["ml_w_out"])


def _cf_in(x, mod, g_ref, win_ref, bin_ref):
    h = _rms_mod(x, g_ref[...], mod)
    agz = _add_rows(_dot_cols(_bf(h), win_ref), bin_ref[...])
    u = agz[:, :INNER] * _sigmoid(agz[:, INNER:2 * INNER])
    return u, agz[:, 2 * INNER:]


def _cf_out(y, z, x, mod, lng_ref, lnb_ref, wout_ref, fg_ref):
    yn = _layernorm(y, lng_ref[...], lnb_ref[...])
    out = _dot_cols(_bf(_silu(yn) * _silu(z)), wout_ref)
    return _rms(x + _mul_rows(out, mod[:, 2 * D_MODEL:]), fg_ref[...])


def _channel_block(cb):
    start = cb * CONV_COLS
    return pl.ds(start if isinstance(cb, int) else pl.multiple_of(start, CONV_COLS), CONV_COLS)


def _dwconv_block(ubuf_ref, sh_ref, wb_ref, bdw_ref, y_ref, tm, cols):
    base = CCONV_HIST - (CCONV - 1)
    max_row_off = (base + CCONV - 2) // SUBLANES * SUBLANES
    for r in range(1, SUBLANES):
        sh_ref[r - 1] = ubuf_ref[r:r + tm + max_row_off, cols]
    b = bdw_ref[:, cols]
    for rc in range(tm // CONV_ROWS):
        acc = jnp.broadcast_to(b, (CONV_ROWS, CONV_COLS))
        for j in range(CCONV):
            a, r = divmod(base + j, SUBLANES)
            rows = pl.ds(SUBLANES * a + CONV_ROWS * rc, CONV_ROWS)
            src = ubuf_ref[rows, cols] if r == 0 else sh_ref[r - 1, rows, :]
            acc = acc + _mul_rows(src, wb_ref[SUBLANES * j:SUBLANES * (j + 1), cols])
        y_ref[pl.ds(CONV_ROWS * rc, CONV_ROWS), cols] = acc


def _memory_step(step, dec_ref, c_ref, q_ref, kw_ref, v_ref, cout_ref):
    head = (step // 2) % HEADS
    first = (step // (2 * HEADS)) * STEP_BATCH
    qt = q_ref[...].T
    kwt = kw_ref[...].T
    parts = []
    for i in range(STEP_BATCH):
        decay = dec_ref[(first + i) * HEADS + head]
        c = c_ref[i]
        parts.append(jnp.sum(c * qt[:, i:i + 1], axis=0, keepdims=True))
        cout_ref[i] = decay * c + kwt[:, i:i + 1] * v_ref[i:i + 1, :]
    return jnp.concatenate(parts, axis=0)


def _cf_prompt_kernel(x_ref, mod_ref, g_ref, win_ref, bin_ref, wdw_ref, bdw_ref, lng_ref, lnb_ref, wout_ref,
                      fg_ref, dec_ref, cs_ref, qs_ref, kws_ref, vs_ref,
                      o_ref, cconv_ref, csout_ref, qc_ref, ubuf_ref, y_ref, z_ref, sh_ref, wb_ref, *, tm):
    step = pl.program_id(0) * pl.num_programs(1) + pl.program_id(1)
    nblk = INNER // CONV_COLS

    @pl.when(step == 0)
    def _():
        for j in range(CCONV):
            wb_ref[SUBLANES * j:SUBLANES * (j + 1), :] = jnp.broadcast_to(wdw_ref[j:j + 1, :], (SUBLANES, INNER))

    @pl.when(pl.program_id(1) == 0)
    def _():
        ubuf_ref[0:CCONV_HIST, :] = jnp.zeros((CCONV_HIST, INNER), f32)

    def conv_blocks(lo, hi):
        def block_body(cb, carry):
            _dwconv_block(ubuf_ref, sh_ref, wb_ref, bdw_ref, y_ref, tm, _channel_block(cb))
            return carry

        lax.fori_loop(lo, hi, block_body, 0)

    @pl.when(step % 2 == 0)
    def _():
        qc_ref[...] = _memory_step(step, dec_ref, cs_ref, qs_ref, kws_ref, vs_ref, csout_ref)
        u, z = _cf_in(x_ref[...], mod_ref[...], g_ref, win_ref, bin_ref)
        ubuf_ref[CCONV_HIST:CCONV_HIST + tm, :] = u
        z_ref[...] = z
        conv_blocks(0, nblk // 2)

    @pl.when(step % 2 == 1)
    def _():
        qc_ref[...] += _memory_step(step, dec_ref, cs_ref, qs_ref, kws_ref, vs_ref, csout_ref)
        conv_blocks(nblk // 2, nblk)
        cconv_ref[...] = ubuf_ref[CCONV_HIST + tm - (CCONV - 1):CCONV_HIST + tm, :]
        ubuf_ref[0:CCONV_HIST, :] = ubuf_ref[tm:tm + CCONV_HIST, :]
        o_ref[...] = _cf_out(y_ref[...], z_ref[...], x_ref[...], mod_ref[...], lng_ref, lnb_ref, wout_ref, fg_ref)


def _cf_prompt(x, mod4, g, w, final_g, dec_flat, C_s, q_s, kw_s, v_s):
    B, T, D = x.shape
    R = C_s.shape[0]
    tm = ROW_TILE
    steps_t = 2 * (T // tm)
    assert B * steps_t * STEP_BATCH == R * HEADS * 2, "one (sequence group, head, key half) per grid step"
    row = lambda b, t: (b, t // 2, 0)

    def ghk(b, t):
        s = b * steps_t + t
        return s // (2 * HEADS), (s // 2) % HEADS, s % 2

    c_spec = pl.BlockSpec((STEP_BATCH, None, DH // 2, DH), lambda b, t: (*ghk(b, t), 0))
    half_spec = pl.BlockSpec((STEP_BATCH, DH // 2), lambda b, t: (ghk(b, t)[0], 2 * ghk(b, t)[1] + ghk(b, t)[2]))
    head_spec = pl.BlockSpec((STEP_BATCH, DH), lambda b, t: ghk(b, t)[:2])
    return pl.pallas_call(
        functools.partial(_cf_prompt_kernel, tm=tm),
        grid=(B, steps_t),
        in_specs=[
            pl.BlockSpec((None, tm, D), row),
            pl.BlockSpec((None, None, SUBLANES, 3 * D), lambda b, t: (1, b, 0, 0)),
            _const_spec((SUBLANES, D)),
            _const_spec(_chunk_shape(D, 3 * INNER)),
            _const_spec((SUBLANES, 3 * INNER)),
            _const_spec((CCONV, INNER)),
            _const_spec((1, INNER)),
            _const_spec((SUBLANES, INNER)),
            _const_spec((SUBLANES, INNER)),
            _const_spec(_chunk_shape(INNER, D)),
            _const_spec((SUBLANES, D)),
            pl.BlockSpec(memory_space=pltpu.SMEM),
            c_spec, half_spec, half_spec, head_spec,
        ],
        out_specs=[
            pl.BlockSpec((None, tm, D), row),
            pl.BlockSpec((None, CCONV - 1, INNER), lambda b, t: (b, 0, 0)),
            c_spec,
            head_spec,
        ],
        out_shape=[
            jax.ShapeDtypeStruct((B, T, D), f32),
            jax.ShapeDtypeStruct((B, CCONV - 1, INNER), f32),
            jax.ShapeDtypeStruct(C_s.shape, f32),
            jax.ShapeDtypeStruct((R, INNER), f32),
        ],
        scratch_shapes=[
            pltpu.VMEM((CCONV_HIST + tm, INNER), f32),
            pltpu.VMEM((tm, INNER), f32),
            pltpu.VMEM((tm, INNER), f32),
            pltpu.VMEM((SUBLANES - 1, tm + CCONV_HIST - SUBLANES, CONV_COLS), f32),
            pltpu.VMEM((SUBLANES * CCONV, INNER), f32),
        ],
        compiler_params=_params("arbitrary", "arbitrary"),
        name="cf_prompt",
    )(x, mod4, g, w["cf_w_in"], w["cf_b_in"], w["cf_w_dw"], w["cf_b_dw"], w["cf_ln_g"], w["cf_ln_b"],
      w["cf_w_out"], final_g, dec_flat, C_s, q_s, kw_s, v_s)


def _cf_sample_kernel(x_ref, mod_ref, g_ref, win_ref, bin_ref, wdw_ref, bdw_ref, lng_ref, lnb_ref, wout_ref,
                      fg_ref, st_ref, o_ref, stout_ref, u_ref, z_ref, y_ref, *, tb):
    i = pl.program_id(0)

    @pl.when(i == 0)
    def _():
        u, z = _cf_in(x_ref[...], mod_ref[...], g_ref, win_ref, bin_ref)
        u_ref[...] = u
        z_ref[...] = z

    rows = pl.ds(pl.multiple_of(i * tb, tb), tb)
    u = u_ref[rows, :]
    acc = bdw_ref[...] + wdw_ref[CCONV - 1:CCONV, :] * u
    for j in range(CCONV - 1):
        acc = acc + wdw_ref[j:j + 1, :] * st_ref[j]
    y_ref[rows, :] = acc
    stout_ref[0:CCONV - 2] = st_ref[1:CCONV - 1]
    stout_ref[CCONV - 2] = u

    @pl.when(i == pl.num_programs(0) - 1)
    def _():
        o_ref[...] = _cf_out(y_ref[...], z_ref[...], x_ref[...], mod_ref[...], lng_ref, lnb_ref, wout_ref, fg_ref)


def _cf_sample(x, mod, g, w, final_g, st):
    R, D = x.shape
    tb = SAMPLE_TILE
    state = pl.BlockSpec((CCONV - 1, tb, INNER), lambda i: (0, i, 0))
    return pl.pallas_call(
        functools.partial(_cf_sample_kernel, tb=tb),
        grid=(R // tb,),
        in_specs=[
            _const_spec((R, D)),
            _const_spec((R, 3 * D)),
            _const_spec((1, D)),
            _const_spec(_chunk_shape(D, 3 * INNER)),
            _const_spec((1, 3 * INNER)),
            _const_spec((CCONV, INNER)),
            _const_spec((1, INNER)),
            _const_spec((1, INNER)),
            _const_spec((1, INNER)),
            _const_spec(_chunk_shape(INNER, D)),
            _const_spec((1, D)),
            state,
        ],
        out_specs=[pl.BlockSpec((R, D), lambda i: (0, 0)), state],
        out_shape=[jax.ShapeDtypeStruct((R, D), f32), jax.ShapeDtypeStruct((CCONV - 1, R, INNER), f32)],
        scratch_shapes=[pltpu.VMEM((R, INNER), f32)] * 3,
        compiler_params=_params("arbitrary"),
        name="cf_sample",
    )(x, mod, g, w["cf_w_in"], w["cf_b_in"], w["cf_w_dw"], w["cf_b_dw"], w["cf_ln_g"], w["cf_ln_b"],
      w["cf_w_out"], final_g, st)


def kernel(x_prompt, x_sample, c_prompt, c_sample, state_mlstm_C, state_mlstm_n, state_mlstm_m, state_mlstm_conv, state_conf_conv, norm_g, w_ada, b_ada, ml_w_in, ml_w_conv, ml_b_conv, ml_w_q, ml_w_k, ml_w_v, ml_w_ig, ml_b_ig, ml_w_fg, ml_b_fg, ml_ln_g, ml_skip, ml_w_out, cf_w_in, cf_b_in, cf_w_dw, cf_b_dw, cf_ln_g, cf_ln_b, cf_w_out, final_g):
    B = x_prompt.shape[0]
    R = x_sample.shape[0]
    D = D_MODEL

    gate_pad = LANES - 2 * HEADS
    w = {
        "ml_w_in": _col_chunks(ml_w_in[0]),
        "ml_w_conv": jnp.repeat(ml_w_conv[0], SUBLANES, axis=0),
        "ml_b_conv": ml_b_conv,
        "ml_w_q": _bf(ml_w_q[0]),
        "ml_w_k": _bf(ml_w_k[0]),
        "ml_w_v": _bf(ml_w_v[0]),
        "ml_w_igf": _bf(jnp.pad(jnp.concatenate([ml_w_ig[0], ml_w_fg[0]], axis=1), ((0, 0), (0, gate_pad)))),
        "ml_b_igf": jnp.pad(jnp.concatenate([ml_b_ig, ml_b_fg], axis=1), ((0, 0), (0, gate_pad))),        "ml_ln_g": ml_ln_g.reshape(1, INNER),
        "ml_skip": ml_skip,
        "ml_w_out": _col_chunks(ml_w_out[0]),
        "cf_w_in": _col_chunks(cf_w_in[0]),
        "cf_b_in": cf_b_in,
        "cf_w_dw": cf_w_dw[0],
        "cf_b_dw": cf_b_dw,
        "cf_ln_g": cf_ln_g,
        "cf_ln_b": cf_ln_b,
        "cf_w_out": _col_chunks(cf_w_out[0]),
    }
    final_g2 = final_g.reshape(1, D)
    wp = dict(w, **{name: _rep(w[name]) for name in
                    ("ml_b_conv", "ml_ln_g", "ml_skip", "cf_b_in", "cf_ln_g", "cf_ln_b")})

    mod = _ada(jnp.concatenate([c_prompt, c_sample], axis=0), w_ada, b_ada)
    mod_p = jnp.broadcast_to(mod[:, :B, None, :], (mod.shape[0], B, SUBLANES, 3 * D))
    mod_s = mod[:, B:]

    tap_major = lambda a: jnp.transpose(a, (1, 0, 2))
    xs = x_sample.reshape(R, D)
    m_pad = jnp.pad(state_mlstm_m[0], ((0, 0), (0, LANES - HEADS)))
    (q_s, kw_s, v_s, xc_s, z_s, n_s, a1, a2, dec, m_s, mconv_s) = _ml_front_sample(
        xs, mod_s[0], norm_g[0:1], w, tap_major(state_mlstm_conv[0]), state_mlstm_n[0].reshape(R, INNER), m_pad)

    x1, mconv_p, C_p, n_p, m_p = _ml_prompt(x_prompt, mod_p, _rep(norm_g[0]), wp)
    y_prompt, cconv_p, C_s, qc = _cf_prompt(
        x1, mod_p, _rep(norm_g[1]), wp, _rep(final_g), dec[:, :HEADS].reshape(R * HEADS), state_mlstm_C[0],
        q_s, kw_s, v_s)

    x1_s = _ml_back_sample(a1, a2, v_s, qc, xc_s, z_s, xs, mod_s[0], w)
    y_sample, cconv_s = _cf_sample(x1_s, mod_s[1], norm_g[1:2], w, final_g2, tap_major(state_conf_conv[0]))

    return (
        y_prompt,
        y_sample.reshape(R, 1, D),
        C_p[None],
        C_s[None],
        n_p[None],
        n_s.reshape(1, R, HEADS, DH),
        m_p[:, 0, :HEADS][None],
        m_s[:, :HEADS][None],
        mconv_p[None],
        tap_major(mconv_s)[None],
        cconv_p[None],
        tap_major(cconv_s)[None],
    )
```

```python
import functools

import jax
import jax.numpy as jnp
from jax import lax
from jax.experimental import pallas as pl
from jax.experimental.pallas import tpu as pltpu

D_MODEL = 1024
INNER = 2 * D_MODEL
HEADS = 4
DH = INNER // HEADS
MCONV = 4
CCONV = 31
EPS = 1e-6
NEG = -1e30
K_SCALE = DH ** -0.5

LANES = 128
SUBLANES = 8
VMEM_LIMIT = 56 * 2 ** 20

ROW_TILE = 256
CHUNK = 256
STEP_BATCH = 8
CONV_ROWS = 64
CONV_COLS = 256
CCONV_HIST = 32
SAMPLE_TILE = 16
W_COLS = 512

f32 = jnp.float32
bf16 = jnp.bfloat16


def _bf(x):
    return x.astype(bf16)


def _dot(a, b):
    return jnp.dot(a, b, preferred_element_type=f32)


def _dot_cols(a, w_refs):
    return jnp.concatenate([_dot(a, r[...]) for r in w_refs], axis=1)


def _col_specs(k, n):
    return [pl.BlockSpec((k, W_COLS), lambda *_, c=c: (0, c), pipeline_mode=pl.Buffered(1))
            for c in range(n // W_COLS)]


def _grouped(kernel, counts):
    def body(*refs, **kw):
        it = iter(refs)
        args = [next(it) if n == 1 else tuple(next(it) for _ in range(n)) for n in counts]
        return kernel(*args, *it, **kw)
    return body


def _sigmoid(x):
    return 0.5 * jnp.tanh(0.5 * x) + 0.5


def _silu(x):
    h = 0.5 * x
    return h * jnp.tanh(h) + h


def _log_sigmoid(x):
    return jnp.minimum(x, 0.0) - jnp.log1p(jnp.exp(-jnp.abs(x)))


def _rows_op(op, x, r):
    m, n = x.shape
    if r.shape[0] in (1, m):
        return op(x, r)
    return op(x.reshape(m // SUBLANES, SUBLANES, n), r[None]).reshape(m, n)


def _mul_rows(x, r):
    return _rows_op(jnp.multiply, x, r)


def _add_rows(x, r):
    return _rows_op(jnp.add, x, r)


def _rep(row):
    return jnp.broadcast_to(row.reshape(1, -1), (SUBLANES, row.size))


def _rms(x, g):
    return _mul_rows(x * lax.rsqrt(jnp.mean(x * x, axis=-1, keepdims=True) + EPS), g)


def _rms_mod(x, g, mod):
    return _add_rows(_mul_rows(_rms(x, g), 1.0 + mod[:, D_MODEL:2 * D_MODEL]), mod[:, :D_MODEL])


def _layernorm(x, g, b=None):
    mu = jnp.mean(x, axis=-1, keepdims=True)
    xc = x - mu
    var = jnp.mean(xc * xc, axis=-1, keepdims=True)
    y = _mul_rows(xc * lax.rsqrt(var + EPS), g)
    return y if b is None else _add_rows(y, b)


def _const_spec(shape):
    n = len(shape)
    return pl.BlockSpec(shape, lambda *_: (0,) * n, pipeline_mode=pl.Buffered(1))


def _params(*sem):
    return pltpu.CompilerParams(dimension_semantics=sem, vmem_limit_bytes=VMEM_LIMIT)


def _ada_kernel(c_ref, w_ref, b_ref, o_ref):
    o_ref[...] = _dot(_bf(_silu(c_ref[...])), _bf(w_ref[...])) + b_ref[...]


def _ada(c_all, w_ada, b_ada):
    depth, d, d3 = w_ada.shape
    rows = c_all.shape[0]
    tn = D_MODEL
    return pl.pallas_call(
        _ada_kernel,
        grid=(depth, d3 // tn),
        in_specs=[
            pl.BlockSpec((rows, d), lambda i, j: (0, 0)),
            pl.BlockSpec((None, d, tn), lambda i, j: (i, 0, j)),
            pl.BlockSpec((None, 1, tn), lambda i, j: (i, 0, j)),
        ],
        out_specs=pl.BlockSpec((None, rows, tn), lambda i, j: (i, 0, j)),
        out_shape=jax.ShapeDtypeStruct((depth, rows, d3), f32),
        compiler_params=_params("arbitrary", "arbitrary"),
        name="ada",
    )(c_all, w_ada, b_ada.reshape(depth, 1, d3))


def _ml_proj(xm, xc, wq_ref, wk_ref, wv_ref):
    xcb, xmb = _bf(xc), _bf(xm)
    qs, ks, vs = [], [], []
    for h in range(HEADS):
        seg = slice(h * DH, (h + 1) * DH)
        qs.append(_dot(xcb[:, seg], wq_ref[h]))
        ks.append(_dot(xcb[:, seg], wk_ref[h]))
        vs.append(_dot(xmb[:, seg], wv_ref[h]))
    return jnp.concatenate(qs, axis=1), jnp.concatenate(ks, axis=1), jnp.concatenate(vs, axis=1)


def _ml_qkv(xm, xc, wq_ref, wk_ref, wv_ref, wigf_ref, bigf_ref):
    q, k, v = _ml_proj(xm, xc, wq_ref, wk_ref, wv_ref)
    pre = (_dot(_bf(q), wigf_ref[0:INNER, :]) + _dot(_bf(k), wigf_ref[INNER:2 * INNER, :])
           + _dot(_bf(v), wigf_ref[2 * INNER:3 * INNER, :]) + bigf_ref[...])
    lane = lax.broadcasted_iota(jnp.int32, pre.shape, 1)
    gates = jnp.where(lane >= HEADS, _log_sigmoid(pre), pre)
    return q, k, v, gates


def _ml_act(hh, xc, z, skip):
    return _bf((hh + _mul_rows(xc, skip)) * _silu(z))


def _ml_residual(act, x, mod, wout_ref):
    return x + _mul_rows(_dot_cols(act, wout_ref), mod[:, 2 * D_MODEL:])


def _tap(wconv_ref, j):
    return wconv_ref[SUBLANES * j:SUBLANES * (j + 1), :]


def _ml_conv_prompt(xm, wconv_ref, bconv_ref, xbuf_ref, tm):
    xbuf_ref[SUBLANES:SUBLANES + tm, :] = xm
    acc = _add_rows(_mul_rows(xm, _tap(wconv_ref, MCONV - 1)), bconv_ref[...])
    for j in range(MCONV - 1):
        start = SUBLANES - (MCONV - 1) + j
        acc = acc + _mul_rows(xbuf_ref[start:start + tm, :], _tap(wconv_ref, j))
    return acc


def _mlstm_chunk(q, ks, v, gates, lng_ref, C_ref, n_ref, m_ref):
    L = q.shape[0]
    row = lax.broadcasted_iota(jnp.int32, (L, L), 0)
    col = lax.broadcasted_iota(jnp.int32, (L, L), 1)
    causal = row >= col
    tri = jnp.where(causal, 1.0, 0.0).astype(bf16)
    hi = _bf(gates)
    lo = _bf(gates - hi.astype(f32))
    bcum = _dot(tri, hi) + _dot(tri, lo)
    gates_t = gates.T
    bcum_t = bcum.T

    hs = []
    for h in range(HEADS):
        seg = slice(h * DH, (h + 1) * DH)
        ig_col = gates[:, h:h + 1]
        b_col = bcum[:, HEADS + h:HEADS + h + 1]
        ig_row = gates_t[h:h + 1, :]
        b_row = bcum_t[HEADS + h:HEADS + h + 1, :]
        m_prev = m_ref[:, h:h + 1]
        qh, kh, vh = q[:, seg], ks[:, seg], v[:, seg]

        dmat = jnp.where(causal, b_col - b_row + ig_row, NEG)
        inter = b_col + m_prev
        m_t = jnp.maximum(inter, jnp.max(dmat, axis=1, keepdims=True))
        w_intra = jnp.exp(dmat - m_t)
        w_inter = jnp.exp(inter - m_t)
        s = lax.dot_general(qh, kh, (((1,), (1,)), ((), ())), preferred_element_type=f32) * w_intra
        num = _dot(_bf(s), vh) + w_inter * _dot(qh, _bf(C_ref[h]))
        qn = jnp.sum(qh.astype(f32) * n_ref[h:h + 1, :], axis=1, keepdims=True)
        den = jnp.sum(s, axis=1, keepdims=True) + w_inter * qn
        hc = num / jnp.maximum(jnp.abs(den), jnp.exp(-m_t))
        hs.append(_layernorm(hc, lng_ref[:, seg]))

        m_new = m_t[L - 1:L, :]
        b_last = b_col[L - 1:L, :]
        decay = jnp.exp(b_last + m_prev - m_new)
        kw = kh.astype(f32) * jnp.exp(b_last - b_col + ig_col - m_new)
        C_ref[h] = decay * C_ref[h] + lax.dot_general(_bf(kw), vh, (((0,), (0,)), ((), ())),
                                                      preferred_element_type=f32)
        n_ref[h:h + 1, :] = decay * n_ref[h:h + 1, :] + jnp.sum(kw, axis=0, keepdims=True)
        m_ref[:, h:h + 1] = m_new
    return jnp.concatenate(hs, axis=1)


def _ml_prompt_kernel(x_ref, mod_ref, g_ref, win_ref, wconv_ref, bconv_ref, wq_ref, wk_ref, wv_ref, wigf_ref,
                      bigf_ref, lng_ref, skip_ref, wout_ref,
                      o_ref, mconv_ref, C_ref, n_ref, m_ref, xbuf_ref, *, tm):
    @pl.when(pl.program_id(1) == 0)
    def _():
        xbuf_ref[0:SUBLANES, :] = jnp.zeros((SUBLANES, INNER), f32)
        C_ref[...] = jnp.zeros_like(C_ref)
        n_ref[...] = jnp.zeros_like(n_ref)
        m_ref[...] = jnp.zeros_like(m_ref)

    x = x_ref[...]
    mod = mod_ref[...]
    xmz = _dot_cols(_bf(_rms_mod(x, g_ref[...], mod)), win_ref)
    xm = xmz[:, :INNER]
    z = xmz[:, INNER:]
    xc = _silu(_ml_conv_prompt(xm, wconv_ref, bconv_ref, xbuf_ref, tm))
    mconv_ref[...] = xbuf_ref[SUBLANES + tm - (MCONV - 1):SUBLANES + tm, :]
    xbuf_ref[0:SUBLANES, :] = xbuf_ref[tm:tm + SUBLANES, :]

    q, k, v, gates = _ml_qkv(xm, xc, wq_ref, wk_ref, wv_ref, wigf_ref, bigf_ref)
    hh = _mlstm_chunk(_bf(q), _bf(k * K_SCALE), _bf(v), gates, lng_ref, C_ref, n_ref, m_ref)
    o_ref[...] = _ml_residual(_ml_act(hh, xc, z, skip_ref[...]), x, mod, wout_ref)


def _ml_prompt(x, mod4, g, w):
    B, T, D = x.shape
    tm = CHUNK
    row = lambda b, t: (b, t, 0)
    win, wout = _col_specs(D, 2 * INNER), _col_specs(INNER, D)
    return pl.pallas_call(
        functools.partial(_grouped(_ml_prompt_kernel, (1, 1, 1, len(win), 1, 1, 1, 1, 1, 1, 1, 1, 1, len(wout))),
                          tm=tm),
        grid=(B, T // tm),
        in_specs=[
            pl.BlockSpec((None, tm, D), row),
            pl.BlockSpec((None, None, SUBLANES, 3 * D), lambda b, t: (0, b, 0, 0)),
            _const_spec((SUBLANES, D)),
            *win,
            _const_spec((MCONV * SUBLANES, INNER)),
            _const_spec((SUBLANES, INNER)),
            _const_spec((HEADS, DH, DH)),
            _const_spec((HEADS, DH, DH)),
            _const_spec((HEADS, DH, DH)),
            _const_spec((3 * INNER, LANES)),
            _const_spec((1, LANES)),
            _const_spec((SUBLANES, INNER)),
            _const_spec((SUBLANES, INNER)),
            *wout,
        ],
        out_specs=[
            pl.BlockSpec((None, tm, D), row),
            pl.BlockSpec((None, MCONV - 1, INNER), lambda b, t: (b, 0, 0)),
            pl.BlockSpec((None, HEADS, DH, DH), lambda b, t: (b, 0, 0, 0)),
            pl.BlockSpec((None, HEADS, DH), lambda b, t: (b, 0, 0)),
            pl.BlockSpec((None, 1, LANES), lambda b, t: (b, 0, 0)),
        ],
        out_shape=[
            jax.ShapeDtypeStruct((B, T, D), f32),
            jax.ShapeDtypeStruct((B, MCONV - 1, INNER), f32),
            jax.ShapeDtypeStruct((B, HEADS, DH, DH), f32),
            jax.ShapeDtypeStruct((B, HEADS, DH), f32),
            jax.ShapeDtypeStruct((B, 1, LANES), f32),
        ],
        scratch_shapes=[pltpu.VMEM((SUBLANES + tm, INNER), f32)],
        compiler_params=_params("arbitrary", "arbitrary"),
        name="ml_prompt",
    )(x, mod4, g, *[w["ml_w_in"]] * len(win), w["ml_w_conv"], w["ml_b_conv"], w["ml_w_q"], w["ml_w_k"],
      w["ml_w_v"], w["ml_w_igf"], w["ml_b_igf"], w["ml_ln_g"], w["ml_skip"], *[w["ml_w_out"]] * len(wout))


def _ml_front_sample_kernel(x_ref, mod_ref, g_ref, win_ref, wconv_ref, bconv_ref, wq_ref, wk_ref, wv_ref,
                            wigf_ref, bigf_ref, st_ref, n_ref, m_ref,
                            q_ref, kw_ref, v_ref, xc_ref, z_ref, nnew_ref, a1_ref, a2_ref, dec_ref, mnew_ref,
                            mconv_ref):
    h = _rms_mod(x_ref[...], g_ref[...], mod_ref[...])
    xmz = _dot_cols(_bf(h), win_ref)
    xm = xmz[:, :INNER]
    z_ref[...] = xmz[:, INNER:]
    acc = _add_rows(_mul_rows(xm, _tap(wconv_ref, MCONV - 1)), bconv_ref[...])
    for j in range(MCONV - 1):
        acc = acc + _mul_rows(st_ref[j], _tap(wconv_ref, j))
    xc = _silu(acc)
    xc_ref[...] = xc
    mconv_ref[0:MCONV - 2] = st_ref[1:MCONV - 1]
    mconv_ref[MCONV - 2] = xm

    q, k, v, gates = _ml_qkv(xm, xc, wq_ref, wk_ref, wv_ref, wigf_ref, bigf_ref)
    ks = k * K_SCALE
    q_ref[...] = q
    v_ref[...] = v

    ig = gates
    lf = pltpu.roll(gates, LANES - HEADS, axis=1)
    inter = lf + m_ref[...]
    m_new = jnp.maximum(inter, ig)
    w_intra = jnp.exp(ig - m_new)
    w_inter = jnp.exp(inter - m_new)
    lane = lax.broadcasted_iota(jnp.int32, ig.shape, 1)
    qk = jnp.zeros_like(ig)
    qn = jnp.zeros_like(ig)
    for h in range(HEADS):
        seg = slice(h * DH, (h + 1) * DH)
        qk = jnp.where(lane == h, jnp.sum(q[:, seg] * ks[:, seg], axis=1, keepdims=True), qk)
        qn = jnp.where(lane == h, jnp.sum(q[:, seg] * n_ref[:, seg], axis=1, keepdims=True), qn)
        kw = w_intra[:, h:h + 1] * ks[:, seg]
        kw_ref[:, seg] = kw
        nnew_ref[:, seg] = w_inter[:, h:h + 1] * n_ref[:, seg] + kw
    s = qk * w_intra
    den = s + w_inter * qn
    dn = jnp.maximum(jnp.abs(den), jnp.exp(-m_new))
    valid = lane < HEADS
    a1_ref[...] = jnp.where(valid, s / dn, 0.0)
    a2_ref[...] = jnp.where(valid, w_inter / dn, 0.0)
    dec_ref[...] = jnp.where(valid, w_inter, 0.0)
    mnew_ref[...] = jnp.where(valid, m_new, 0.0)


def _ml_front_sample(x, mod, g, w, st, n, m):
    R, D = x.shape
    full = lambda shape: pl.BlockSpec(shape, lambda i: (0,) * len(shape))
    act = jax.ShapeDtypeStruct((R, INNER), f32)
    small = jax.ShapeDtypeStruct((R, LANES), f32)
    win = _col_specs(D, 2 * INNER)
    return pl.pallas_call(
        _grouped(_ml_front_sample_kernel, (1, 1, 1, len(win))),
        grid=(1,),
        in_specs=[
            full((R, D)), full((R, 3 * D)), full((1, D)), *win,
            full((MCONV * SUBLANES, INNER)),
            full((1, INNER)), full((HEADS, DH, DH)), full((HEADS, DH, DH)), full((HEADS, DH, DH)),
            full((3 * INNER, LANES)), full((1, LANES)),
            full((MCONV - 1, R, INNER)), full((R, INNER)), full((R, LANES)),
        ],
        out_specs=[full((R, INNER))] * 6 + [full((R, LANES))] * 4 + [full((MCONV - 1, R, INNER))],
        out_shape=[act] * 6 + [small] * 4 + [jax.ShapeDtypeStruct((MCONV - 1, R, INNER), f32)],
        compiler_params=_params("arbitrary"),
        name="ml_front_sample",
    )(x, mod, g, *[w["ml_w_in"]] * len(win), w["ml_w_conv"], w["ml_b_conv"], w["ml_w_q"], w["ml_w_k"],
      w["ml_w_v"], w["ml_w_igf"], w["ml_b_igf"], st, n, m)


def _ml_back_sample_kernel(a1_ref, a2_ref, v_ref, qc_ref, lng_ref, xc_ref, z_ref, x_ref, mod_ref, skip_ref,
                           wout_ref, o_ref):
    hs = []
    for h in range(HEADS):
        seg = slice(h * DH, (h + 1) * DH)
        hc = a1_ref[:, h:h + 1] * v_ref[:, seg] + a2_ref[:, h:h + 1] * qc_ref[:, seg]
        hs.append(_layernorm(hc, lng_ref[:, seg]))
    hh = jnp.concatenate(hs, axis=1)
    act = _ml_act(hh, xc_ref[...], z_ref[...], skip_ref[...])
    o_ref[...] = _ml_residual(act, x_ref[...], mod_ref[...], wout_ref)


def _ml_back_sample(a1, a2, v, qc, xc, z, x, mod, w):
    R, D = x.shape
    full = lambda shape: pl.BlockSpec(shape, lambda i: (0,) * len(shape))
    wout = _col_specs(INNER, D)
    return pl.pallas_call(
        _grouped(_ml_back_sample_kernel, (1,) * 10 + (len(wout),)),
        grid=(1,),
        in_specs=[full((R, LANES)), full((R, LANES)), full((R, INNER)), full((R, INNER)), full((1, INNER)),
                  full((R, INNER)), full((R, INNER)), full((R, D)), full((R, 3 * D)), full((1, INNER)),
                  *wout],
        out_specs=full((R, D)),
        out_shape=jax.ShapeDtypeStruct((R, D), f32),
        compiler_params=_params("arbitrary"),
        name="ml_back_sample",
    )(a1, a2, v, qc, w["ml_ln_g"], xc, z, x, mod, w["ml_skip"], *[w["ml_w_out"]] * len(wout))


def _cf_in(x, mod, g_ref, win_ref, bin_ref):
    h = _rms_mod(x, g_ref[...], mod)
    agz = _add_rows(_dot_cols(_bf(h), win_ref), bin_ref[...])
    u = agz[:, :INNER] * _sigmoid(agz[:, INNER:2 * INNER])
    return u, agz[:, 2 * INNER:]


def _cf_out(y, z, x, mod, lng_ref, lnb_ref, wout_ref, fg_ref):
    yn = _layernorm(y, lng_ref[...], lnb_ref[...])
    out = _dot_cols(_bf(_silu(yn) * _silu(z)), wout_ref)
    return _rms(x + _mul_rows(out, mod[:, 2 * D_MODEL:]), fg_ref[...])


def _channel_block(cb):
    start = cb * CONV_COLS
    return pl.ds(start if isinstance(cb, int) else pl.multiple_of(start, CONV_COLS), CONV_COLS)


def _dwconv_block(ubuf_ref, sh_ref, wb_ref, bdw_ref, y_ref, tm, cols):
    base = CCONV_HIST - (CCONV - 1)
    max_row_off = (base + CCONV - 2) // SUBLANES * SUBLANES
    for r in range(1, SUBLANES):
        sh_ref[r - 1] = ubuf_ref[r:r + tm + max_row_off, cols]
    b = bdw_ref[:, cols]
    for rc in range(tm // CONV_ROWS):
        acc = jnp.broadcast_to(b, (CONV_ROWS, CONV_COLS))
        for j in range(CCONV):
            a, r = divmod(base + j, SUBLANES)
            rows = pl.ds(SUBLANES * a + CONV_ROWS * rc, CONV_ROWS)
            src = ubuf_ref[rows, cols] if r == 0 else sh_ref[r - 1, rows, :]
            acc = acc + _mul_rows(src, wb_ref[SUBLANES * j:SUBLANES * (j + 1), cols])
        y_ref[pl.ds(CONV_ROWS * rc, CONV_ROWS), cols] = acc


def _memory_step(step, dec_ref, c_ref, q_ref, kw_ref, v_ref, cout_ref):
    head = (step // 2) % HEADS
    first = (step // (2 * HEADS)) * STEP_BATCH
    qt = q_ref[...].T
    kwt = kw_ref[...].T
    parts = []
    for i in range(STEP_BATCH):
        decay = dec_ref[(first + i) * HEADS + head]
        c = c_ref[i]
        parts.append(jnp.sum(c * qt[:, i:i + 1], axis=0, keepdims=True))
        cout_ref[i] = decay * c + kwt[:, i:i + 1] * v_ref[i:i + 1, :]
    return jnp.concatenate(parts, axis=0)


def _cf_prompt_kernel(x_ref, mod_ref, g_ref, win_ref, bin_ref, wdw_ref, bdw_ref, lng_ref, lnb_ref, wout_ref,
                      fg_ref, dec_ref, cs_ref, qs_ref, kws_ref, vs_ref,
                      o_ref, cconv_ref, csout_ref, qc_ref, ubuf_ref, y_ref, z_ref, sh_ref, wb_ref, *, tm):
    step = pl.program_id(0) * pl.num_programs(1) + pl.program_id(1)
    nblk = INNER // CONV_COLS

    @pl.when(step == 0)
    def _():
        for j in range(CCONV):
            wb_ref[SUBLANES * j:SUBLANES * (j + 1), :] = jnp.broadcast_to(wdw_ref[j:j + 1, :], (SUBLANES, INNER))

    @pl.when(pl.program_id(1) == 0)
    def _():
        ubuf_ref[0:CCONV_HIST, :] = jnp.zeros((CCONV_HIST, INNER), f32)

    def conv_blocks(lo, hi):
        def block_body(cb, carry):
            _dwconv_block(ubuf_ref, sh_ref, wb_ref, bdw_ref, y_ref, tm, _channel_block(cb))
            return carry

        lax.fori_loop(lo, hi, block_body, 0)

    @pl.when(step % 2 == 0)
    def _():
        qc_ref[...] = _memory_step(step, dec_ref, cs_ref, qs_ref, kws_ref, vs_ref, csout_ref)
        u, z = _cf_in(x_ref[...], mod_ref[...], g_ref, win_ref, bin_ref)
        ubuf_ref[CCONV_HIST:CCONV_HIST + tm, :] = u
        z_ref[...] = z
        conv_blocks(0, nblk // 2)

    @pl.when(step % 2 == 1)
    def _():
        qc_ref[...] += _memory_step(step, dec_ref, cs_ref, qs_ref, kws_ref, vs_ref, csout_ref)
        conv_blocks(nblk // 2, nblk)
        cconv_ref[...] = ubuf_ref[CCONV_HIST + tm - (CCONV - 1):CCONV_HIST + tm, :]
        ubuf_ref[0:CCONV_HIST, :] = ubuf_ref[tm:tm + CCONV_HIST, :]
        o_ref[...] = _cf_out(y_ref[...], z_ref[...], x_ref[...], mod_ref[...], lng_ref, lnb_ref, wout_ref, fg_ref)


def _cf_prompt(x, mod4, g, w, final_g, dec_flat, C_s, q_s, kw_s, v_s):
    B, T, D = x.shape
    R = C_s.shape[0]
    tm = ROW_TILE
    steps_t = 2 * (T // tm)
    assert B * steps_t * STEP_BATCH == R * HEADS * 2, "one (sequence group, head, key half) per grid step"
    row = lambda b, t: (b, t // 2, 0)

    def ghk(b, t):
        s = b * steps_t + t
        return s // (2 * HEADS), (s // 2) % HEADS, s % 2

    c_spec = pl.BlockSpec((STEP_BATCH, None, DH // 2, DH), lambda b, t: (*ghk(b, t), 0))
    half_spec = pl.BlockSpec((STEP_BATCH, DH // 2), lambda b, t: (ghk(b, t)[0], 2 * ghk(b, t)[1] + ghk(b, t)[2]))
    head_spec = pl.BlockSpec((STEP_BATCH, DH), lambda b, t: ghk(b, t)[:2])
    win, wout = _col_specs(D, 3 * INNER), _col_specs(INNER, D)
    return pl.pallas_call(
        functools.partial(_grouped(_cf_prompt_kernel, (1, 1, 1, len(win), 1, 1, 1, 1, 1, len(wout))), tm=tm),
        grid=(B, steps_t),
        in_specs=[
            pl.BlockSpec((None, tm, D), row),
            pl.BlockSpec((None, None, SUBLANES, 3 * D), lambda b, t: (1, b, 0, 0)),
            _const_spec((SUBLANES, D)),
            *win,
            _const_spec((SUBLANES, 3 * INNER)),
            _const_spec((CCONV, INNER)),
            _const_spec((1, INNER)),
            _const_spec((SUBLANES, INNER)),
            _const_spec((SUBLANES, INNER)),
            *wout,
            _const_spec((SUBLANES, D)),
            pl.BlockSpec(memory_space=pltpu.SMEM),
            c_spec, half_spec, half_spec, head_spec,
        ],
        out_specs=[
            pl.BlockSpec((None, tm, D), row),
            pl.BlockSpec((None, CCONV - 1, INNER), lambda b, t: (b, 0, 0)),
            c_spec,
            head_spec,
        ],
        out_shape=[
            jax.ShapeDtypeStruct((B, T, D), f32),
            jax.ShapeDtypeStruct((B, CCONV - 1, INNER), f32),
            jax.ShapeDtypeStruct(C_s.shape, f32),
            jax.ShapeDtypeStruct((R, INNER), f32),
        ],
        scratch_shapes=[
            pltpu.VMEM((CCONV_HIST + tm, INNER), f32),
            pltpu.VMEM((tm, INNER), f32),
            pltpu.VMEM((tm, INNER), f32),
            pltpu.VMEM((SUBLANES - 1, tm + CCONV_HIST - SUBLANES, CONV_COLS), f32),
            pltpu.VMEM((SUBLANES * CCONV, INNER), f32),
        ],
        compiler_params=_params("arbitrary", "arbitrary"),
        name="cf_prompt",
    )(x, mod4, g, *[w["cf_w_in"]] * len(win), w["cf_b_in"], w["cf_w_dw"], w["cf_b_dw"], w["cf_ln_g"],
      w["cf_ln_b"], *[w["cf_w_out"]] * len(wout), final_g, dec_flat, C_s, q_s, kw_s, v_s)


def _cf_sample_kernel(x_ref, mod_ref, g_ref, win_ref, bin_ref, wdw_ref, bdw_ref, lng_ref, lnb_ref, wout_ref,
                      fg_ref, st_ref, o_ref, stout_ref, u_ref, z_ref, y_ref, *, tb):
    i = pl.program_id(0)

    @pl.when(i == 0)
    def _():
        u, z = _cf_in(x_ref[...], mod_ref[...], g_ref, win_ref, bin_ref)
        u_ref[...] = u
        z_ref[...] = z

    rows = pl.ds(pl.multiple_of(i * tb, tb), tb)
    u = u_ref[rows, :]
    acc = bdw_ref[...] + wdw_ref[CCONV - 1:CCONV, :] * u
    for j in range(CCONV - 1):
        acc = acc + wdw_ref[j:j + 1, :] * st_ref[j]
    y_ref[rows, :] = acc
    stout_ref[0:CCONV - 2] = st_ref[1:CCONV - 1]
    stout_ref[CCONV - 2] = u

    @pl.when(i == pl.num_programs(0) - 1)
    def _():
        o_ref[...] = _cf_out(y_ref[...], z_ref[...], x_ref[...], mod_ref[...], lng_ref, lnb_ref, wout_ref, fg_ref)


def _cf_sample(x, mod, g, w, final_g, st):
    R, D = x.shape
    tb = SAMPLE_TILE
    state = pl.BlockSpec((CCONV - 1, tb, INNER), lambda i: (0, i, 0))
    win, wout = _col_specs(D, 3 * INNER), _col_specs(INNER, D)
    return pl.pallas_call(
        functools.partial(_grouped(_cf_sample_kernel, (1, 1, 1, len(win), 1, 1, 1, 1, 1, len(wout))), tb=tb),
        grid=(R // tb,),
        in_specs=[
            _const_spec((R, D)),
            _const_spec((R, 3 * D)),
            _const_spec((1, D)),
            *win,
            _const_spec((1, 3 * INNER)),
            _const_spec((CCONV, INNER)),
            _const_spec((1, INNER)),
            _const_spec((1, INNER)),
            _const_spec((1, INNER)),
            *wout,
            _const_spec((1, D)),
            state,
        ],
        out_specs=[pl.BlockSpec((R, D), lambda i: (0, 0)), state],
        out_shape=[jax.ShapeDtypeStruct((R, D), f32), jax.ShapeDtypeStruct((CCONV - 1, R, INNER), f32)],
        scratch_shapes=[pltpu.VMEM((R, INNER), f32)] * 3,
        compiler_params=_params("arbitrary"),
        name="cf_sample",
    )(x, mod, g, *[w["cf_w_in"]] * len(win), w["cf_b_in"], w["cf_w_dw"], w["cf_b_dw"], w["cf_ln_g"],
      w["cf_ln_b"], *[w["cf_w_out"]] * len(wout), final_g, st)


def kernel(x_prompt, x_sample, c_prompt, c_sample, state_mlstm_C, state_mlstm_n, state_mlstm_m, state_mlstm_conv, state_conf_conv, norm_g, w_ada, b_ada, ml_w_in, ml_w_conv, ml_b_conv, ml_w_q, ml_w_k, ml_w_v, ml_w_ig, ml_b_ig, ml_w_fg, ml_b_fg, ml_ln_g, ml_skip, ml_w_out, cf_w_in, cf_b_in, cf_w_dw, cf_b_dw, cf_ln_g, cf_ln_b, cf_w_out, final_g):
    B = x_prompt.shape[0]
    R = x_sample.shape[0]
    D = D_MODEL

    gate_pad = LANES - 2 * HEADS
    w = {
        "ml_w_in": _bf(ml_w_in[0]),
        "ml_w_conv": jnp.repeat(ml_w_conv[0], SUBLANES, axis=0),
        "ml_b_conv": ml_b_conv,
        "ml_w_q": _bf(ml_w_q[0]),
        "ml_w_k": _bf(ml_w_k[0]),
        "ml_w_v": _bf(ml_w_v[0]),
        "ml_w_igf": _bf(jnp.pad(jnp.concatenate([ml_w_ig[0], ml_w_fg[0]], axis=1), ((0, 0), (0, gate_pad)))),
        "ml_b_igf": jnp.pad(jnp.concatenate([ml_b_ig, ml_b_fg], axis=1), ((0, 0), (0, gate_pad))),        "ml_ln_g": ml_ln_g.reshape(1, INNER),
        "ml_skip": ml_skip,
        "ml_w_out": _bf(ml_w_out[0]),
        "cf_w_in": _bf(cf_w_in[0]),
        "cf_b_in": cf_b_in,
        "cf_w_dw": cf_w_dw[0],
        "cf_b_dw": cf_b_dw,
        "cf_ln_g": cf_ln_g,
        "cf_ln_b": cf_ln_b,
        "cf_w_out": _bf(cf_w_out[0]),
    }
    final_g2 = final_g.reshape(1, D)
    wp = dict(w, **{name: _rep(w[name]) for name in
                    ("ml_b_conv", "ml_ln_g", "ml_skip", "cf_b_in", "cf_ln_g", "cf_ln_b")})

    mod = _ada(jnp.concatenate([c_prompt, c_sample], axis=0), w_ada, b_ada)
    mod_p = jnp.broadcast_to(mod[:, :B, None, :], (mod.shape[0], B, SUBLANES, 3 * D))
    mod_s = mod[:, B:]

    tap_major = lambda a: jnp.transpose(a, (1, 0, 2))
    xs = x_sample.reshape(R, D)
    m_pad = jnp.pad(state_mlstm_m[0], ((0, 0), (0, LANES - HEADS)))
    (q_s, kw_s, v_s, xc_s, z_s, n_s, a1, a2, dec, m_s, mconv_s) = _ml_front_sample(
        xs, mod_s[0], norm_g[0:1], w, tap_major(state_mlstm_conv[0]), state_mlstm_n[0].reshape(R, INNER), m_pad)

    x1, mconv_p, C_p, n_p, m_p = _ml_prompt(x_prompt, mod_p, _rep(norm_g[0]), wp)
    y_prompt, cconv_p, C_s, qc = _cf_prompt(
        x1, mod_p, _rep(norm_g[1]), wp, _rep(final_g), dec[:, :HEADS].reshape(R * HEADS), state_mlstm_C[0],
        q_s, kw_s, v_s)

    x1_s = _ml_back_sample(a1, a2, v_s, qc, xc_s, z_s, xs, mod_s[0], w)
    y_sample, cconv_s = _cf_sample(x1_s, mod_s[1], norm_g[1:2], w, final_g2, tap_major(state_conf_conv[0]))

    return (
        y_prompt,
        y_sample.reshape(R, 1, D),
        C_p[None],
        C_s[None],
        n_p[None],
        n_s.reshape(1, R, HEADS, DH),
        m_p[:, 0, :HEADS][None],
        m_s[:, :HEADS][None],
        mconv_p[None],
        tap_major(mconv_s)[None],
        cconv_p[None],
        tap_major(cconv_s)[None],
    )
```

```python
import functools

import jax
import jax.numpy as jnp
from jax import lax
from jax.experimental import pallas as pl
from jax.experimental.pallas import tpu as pltpu

D_MODEL = 1024
INNER = 2 * D_MODEL
HEADS = 4
DH = INNER // HEADS
MCONV = 4
CCONV = 31
EPS = 1e-6
NEG = -1e30
K_SCALE = DH ** -0.5

LANES = 128
SUBLANES = 8
VMEM_LIMIT = 56 * 2 ** 20

ROW_TILE = 256
CHUNK = 256
STEP_BATCH = 8
CONV_ROWS = 64
CONV_COLS = 128
CCONV_HIST = 32
SAMPLE_TILE = 16
W_COLS = 512

f32 = jnp.float32
bf16 = jnp.bfloat16


def _bf(x):
    return x.astype(bf16)


def _dot(a, b):
    return jnp.dot(a, b, preferred_element_type=f32)


def _dot_cols(a, w_refs):
    return jnp.concatenate([_dot(a, r[...]) for r in w_refs], axis=1)


def _col_specs(k, n):
    return [pl.BlockSpec((k, W_COLS), lambda *_, c=c: (0, c), pipeline_mode=pl.Buffered(1))
            for c in range(n // W_COLS)]


def _grouped(kernel, counts):
    def body(*refs, **kw):
        it = iter(refs)
        args = [next(it) if n == 1 else tuple(next(it) for _ in range(n)) for n in counts]
        return kernel(*args, *it, **kw)
    return body


def _sigmoid(x):
    return 0.5 * jnp.tanh(0.5 * x) + 0.5


def _silu(x):
    h = 0.5 * x
    return h * jnp.tanh(h) + h


def _log_sigmoid(x):
    return jnp.minimum(x, 0.0) - jnp.log1p(jnp.exp(-jnp.abs(x)))


def _rows_op(op, x, r):
    m, n = x.shape
    if r.shape[0] in (1, m):
        return op(x, r)
    return op(x.reshape(m // SUBLANES, SUBLANES, n), r[None]).reshape(m, n)


def _mul_rows(x, r):
    return _rows_op(jnp.multiply, x, r)


def _add_rows(x, r):
    return _rows_op(jnp.add, x, r)


def _rep(row):
    return jnp.broadcast_to(row.reshape(1, -1), (SUBLANES, row.size))


def _rms(x, g):
    return _mul_rows(x * lax.rsqrt(jnp.mean(x * x, axis=-1, keepdims=True) + EPS), g)


def _rms_mod(x, g, mod):
    return _add_rows(_mul_rows(_rms(x, g), 1.0 + mod[:, D_MODEL:2 * D_MODEL]), mod[:, :D_MODEL])


def _layernorm(x, g, b=None):
    mu = jnp.mean(x, axis=-1, keepdims=True)
    xc = x - mu
    var = jnp.mean(xc * xc, axis=-1, keepdims=True)
    y = _mul_rows(xc * lax.rsqrt(var + EPS), g)
    return y if b is None else _add_rows(y, b)


def _const_spec(shape):
    n = len(shape)
    return pl.BlockSpec(shape, lambda *_: (0,) * n, pipeline_mode=pl.Buffered(1))


def _params(*sem):
    return pltpu.CompilerParams(dimension_semantics=sem, vmem_limit_bytes=VMEM_LIMIT)


def _ada_kernel(c_ref, w_ref, b_ref, o_ref):
    o_ref[...] = _dot(_bf(_silu(c_ref[...])), _bf(w_ref[...])) + b_ref[...]


def _ada(c_all, w_ada, b_ada):
    depth, d, d3 = w_ada.shape
    rows = c_all.shape[0]
    tn = D_MODEL
    return pl.pallas_call(
        _ada_kernel,
        grid=(depth, d3 // tn),
        in_specs=[
            pl.BlockSpec((rows, d), lambda i, j: (0, 0)),
            pl.BlockSpec((None, d, tn), lambda i, j: (i, 0, j)),
            pl.BlockSpec((None, 1, tn), lambda i, j: (i, 0, j)),
        ],
        out_specs=pl.BlockSpec((None, rows, tn), lambda i, j: (i, 0, j)),
        out_shape=jax.ShapeDtypeStruct((depth, rows, d3), f32),
        compiler_params=_params("arbitrary", "arbitrary"),
        name="ada",
    )(c_all, w_ada, b_ada.reshape(depth, 1, d3))


def _ml_proj(xm, xc, wq_ref, wk_ref, wv_ref):
    xcb, xmb = _bf(xc), _bf(xm)
    qs, ks, vs = [], [], []
    for h in range(HEADS):
        seg = slice(h * DH, (h + 1) * DH)
        qs.append(_dot(xcb[:, seg], wq_ref[h]))
        ks.append(_dot(xcb[:, seg], wk_ref[h]))
        vs.append(_dot(xmb[:, seg], wv_ref[h]))
    return jnp.concatenate(qs, axis=1), jnp.concatenate(ks, axis=1), jnp.concatenate(vs, axis=1)


def _ml_qkv(xm, xc, wq_ref, wk_ref, wv_ref, wigf_ref, bigf_ref):
    q, k, v = _ml_proj(xm, xc, wq_ref, wk_ref, wv_ref)
    pre = (_dot(_bf(q), wigf_ref[0:INNER, :]) + _dot(_bf(k), wigf_ref[INNER:2 * INNER, :])
           + _dot(_bf(v), wigf_ref[2 * INNER:3 * INNER, :]) + bigf_ref[...])
    lane = lax.broadcasted_iota(jnp.int32, pre.shape, 1)
    gates = jnp.where(lane >= HEADS, _log_sigmoid(pre), pre)
    return q, k, v, gates


def _ml_act(hh, xc, z, skip):
    return _bf((hh + _mul_rows(xc, skip)) * _silu(z))


def _ml_residual(act, x, mod, wout_ref):
    return x + _mul_rows(_dot_cols(act, wout_ref), mod[:, 2 * D_MODEL:])


def _tap(wconv_ref, j):
    return wconv_ref[SUBLANES * j:SUBLANES * (j + 1), :]


def _ml_conv_prompt(xm, wconv_ref, bconv_ref, xbuf_ref, tm):
    xbuf_ref[SUBLANES:SUBLANES + tm, :] = xm
    acc = _add_rows(_mul_rows(xm, _tap(wconv_ref, MCONV - 1)), bconv_ref[...])
    for j in range(MCONV - 1):
        start = SUBLANES - (MCONV - 1) + j
        acc = acc + _mul_rows(xbuf_ref[start:start + tm, :], _tap(wconv_ref, j))
    return acc


def _mlstm_chunk(q, ks, v, gates, lng_ref, C_ref, n_ref, m_ref):
    L = q.shape[0]
    row = lax.broadcasted_iota(jnp.int32, (L, L), 0)
    col = lax.broadcasted_iota(jnp.int32, (L, L), 1)
    causal = row >= col
    tri = jnp.where(causal, 1.0, 0.0).astype(bf16)
    hi = _bf(gates)
    lo = _bf(gates - hi.astype(f32))
    bcum = _dot(tri, hi) + _dot(tri, lo)
    gates_t = gates.T
    bcum_t = bcum.T

    hs = []
    for h in range(HEADS):
        seg = slice(h * DH, (h + 1) * DH)
        ig_col = gates[:, h:h + 1]
        b_col = bcum[:, HEADS + h:HEADS + h + 1]
        ig_row = gates_t[h:h + 1, :]
        b_row = bcum_t[HEADS + h:HEADS + h + 1, :]
        m_prev = m_ref[:, h:h + 1]
        qh, kh, vh = q[:, seg], ks[:, seg], v[:, seg]

        dmat = jnp.where(causal, b_col - b_row + ig_row, NEG)
        inter = b_col + m_prev
        m_t = jnp.maximum(inter, jnp.max(dmat, axis=1, keepdims=True))
        w_intra = jnp.exp(dmat - m_t)
        w_inter = jnp.exp(inter - m_t)
        s = lax.dot_general(qh, kh, (((1,), (1,)), ((), ())), preferred_element_type=f32) * w_intra
        num = _dot(_bf(s), vh) + w_inter * _dot(qh, _bf(C_ref[h]))
        qn = jnp.sum(qh.astype(f32) * n_ref[h:h + 1, :], axis=1, keepdims=True)
        den = jnp.sum(s, axis=1, keepdims=True) + w_inter * qn
        hc = num / jnp.maximum(jnp.abs(den), jnp.exp(-m_t))
        hs.append(_layernorm(hc, lng_ref[:, seg]))

        m_new = m_t[L - 1:L, :]
        b_last = b_col[L - 1:L, :]
        decay = jnp.exp(b_last + m_prev - m_new)
        kw = kh.astype(f32) * jnp.exp(b_last - b_col + ig_col - m_new)
        C_ref[h] = decay * C_ref[h] + lax.dot_general(_bf(kw), vh, (((0,), (0,)), ((), ())),
                                                      preferred_element_type=f32)
        n_ref[h:h + 1, :] = decay * n_ref[h:h + 1, :] + jnp.sum(kw, axis=0, keepdims=True)
        m_ref[:, h:h + 1] = m_new
    return jnp.concatenate(hs, axis=1)


def _ml_prompt_kernel(x_ref, mod_ref, g_ref, win_ref, wconv_ref, bconv_ref, wq_ref, wk_ref, wv_ref, wigf_ref,
                      bigf_ref, lng_ref, skip_ref, wout_ref,
                      o_ref, mconv_ref, C_ref, n_ref, m_ref, xbuf_ref, *, tm):
    @pl.when(pl.program_id(1) == 0)
    def _():
        xbuf_ref[0:SUBLANES, :] = jnp.zeros((SUBLANES, INNER), f32)
        C_ref[...] = jnp.zeros_like(C_ref)
        n_ref[...] = jnp.zeros_like(n_ref)
        m_ref[...] = jnp.zeros_like(m_ref)

    x = x_ref[...]
    mod = mod_ref[...]
    xmz = _dot_cols(_bf(_rms_mod(x, g_ref[...], mod)), win_ref)
    xm = xmz[:, :INNER]
    z = xmz[:, INNER:]
    xc = _silu(_ml_conv_prompt(xm, wconv_ref, bconv_ref, xbuf_ref, tm))
    mconv_ref[...] = xbuf_ref[SUBLANES + tm - (MCONV - 1):SUBLANES + tm, :]
    xbuf_ref[0:SUBLANES, :] = xbuf_ref[tm:tm + SUBLANES, :]

    q, k, v, gates = _ml_qkv(xm, xc, wq_ref, wk_ref, wv_ref, wigf_ref, bigf_ref)
    hh = _mlstm_chunk(_bf(q), _bf(k * K_SCALE), _bf(v), gates, lng_ref, C_ref, n_ref, m_ref)
    o_ref[...] = _ml_residual(_ml_act(hh, xc, z, skip_ref[...]), x, mod, wout_ref)


def _ml_prompt(x, mod4, g, w):
    B, T, D = x.shape
    tm = CHUNK
    row = lambda b, t: (b, t, 0)
    win, wout = _col_specs(D, 2 * INNER), _col_specs(INNER, D)
    return pl.pallas_call(
        functools.partial(_grouped(_ml_prompt_kernel, (1, 1, 1, len(win), 1, 1, 1, 1, 1, 1, 1, 1, 1, len(wout))),
                          tm=tm),
        grid=(B, T // tm),
        in_specs=[
            pl.BlockSpec((None, tm, D), row),
            pl.BlockSpec((None, None, SUBLANES, 3 * D), lambda b, t: (0, b, 0, 0)),
            _const_spec((SUBLANES, D)),
            *win,
            _const_spec((MCONV * SUBLANES, INNER)),
            _const_spec((SUBLANES, INNER)),
            _const_spec((HEADS, DH, DH)),
            _const_spec((HEADS, DH, DH)),
            _const_spec((HEADS, DH, DH)),
            _const_spec((3 * INNER, LANES)),
            _const_spec((1, LANES)),
            _const_spec((SUBLANES, INNER)),
            _const_spec((SUBLANES, INNER)),
            *wout,
        ],
        out_specs=[
            pl.BlockSpec((None, tm, D), row),
            pl.BlockSpec((None, MCONV - 1, INNER), lambda b, t: (b, 0, 0)),
            pl.BlockSpec((None, HEADS, DH, DH), lambda b, t: (b, 0, 0, 0)),
            pl.BlockSpec((None, HEADS, DH), lambda b, t: (b, 0, 0)),
            pl.BlockSpec((None, 1, LANES), lambda b, t: (b, 0, 0)),
        ],
        out_shape=[
            jax.ShapeDtypeStruct((B, T, D), f32),
            jax.ShapeDtypeStruct((B, MCONV - 1, INNER), f32),
            jax.ShapeDtypeStruct((B, HEADS, DH, DH), f32),
            jax.ShapeDtypeStruct((B, HEADS, DH), f32),
            jax.ShapeDtypeStruct((B, 1, LANES), f32),
        ],
        scratch_shapes=[pltpu.VMEM((SUBLANES + tm, INNER), f32)],
        compiler_params=_params("arbitrary", "arbitrary"),
        name="ml_prompt",
    )(x, mod4, g, *[w["ml_w_in"]] * len(win), w["ml_w_conv"], w["ml_b_conv"], w["ml_w_q"], w["ml_w_k"],
      w["ml_w_v"], w["ml_w_igf"], w["ml_b_igf"], w["ml_ln_g"], w["ml_skip"], *[w["ml_w_out"]] * len(wout))


def _ml_front_sample_kernel(x_ref, mod_ref, g_ref, win_ref, wconv_ref, bconv_ref, wq_ref, wk_ref, wv_ref,
                            wigf_ref, bigf_ref, st_ref, n_ref, m_ref,
                            q_ref, kw_ref, v_ref, xc_ref, z_ref, nnew_ref, a1_ref, a2_ref, dec_ref, mnew_ref,
                            mconv_ref):
    h = _rms_mod(x_ref[...], g_ref[...], mod_ref[...])
    xmz = _dot_cols(_bf(h), win_ref)
    xm = xmz[:, :INNER]
    z_ref[...] = xmz[:, INNER:]
    acc = _add_rows(_mul_rows(xm, _tap(wconv_ref, MCONV - 1)), bconv_ref[...])
    for j in range(MCONV - 1):
        acc = acc + _mul_rows(st_ref[j], _tap(wconv_ref, j))
    xc = _silu(acc)
    xc_ref[...] = xc
    mconv_ref[0:MCONV - 2] = st_ref[1:MCONV - 1]
    mconv_ref[MCONV - 2] = xm

    q, k, v, gates = _ml_qkv(xm, xc, wq_ref, wk_ref, wv_ref, wigf_ref, bigf_ref)
    ks = k * K_SCALE
    q_ref[...] = q
    v_ref[...] = v

    ig = gates
    lf = pltpu.roll(gates, LANES - HEADS, axis=1)
    inter = lf + m_ref[...]
    m_new = jnp.maximum(inter, ig)
    w_intra = jnp.exp(ig - m_new)
    w_inter = jnp.exp(inter - m_new)
    lane = lax.broadcasted_iota(jnp.int32, ig.shape, 1)
    qk = jnp.zeros_like(ig)
    qn = jnp.zeros_like(ig)
    for h in range(HEADS):
        seg = slice(h * DH, (h + 1) * DH)
        qk = jnp.where(lane == h, jnp.sum(q[:, seg] * ks[:, seg], axis=1, keepdims=True), qk)
        qn = jnp.where(lane == h, jnp.sum(q[:, seg] * n_ref[:, seg], axis=1, keepdims=True), qn)
        kw = w_intra[:, h:h + 1] * ks[:, seg]
        kw_ref[:, seg] = kw
        nnew_ref[:, seg] = w_inter[:, h:h + 1] * n_ref[:, seg] + kw
    s = qk * w_intra
    den = s + w_inter * qn
    dn = jnp.maximum(jnp.abs(den), jnp.exp(-m_new))
    valid = lane < HEADS
    a1_ref[...] = jnp.where(valid, s / dn, 0.0)
    a2_ref[...] = jnp.where(valid, w_inter / dn, 0.0)
    dec_ref[...] = jnp.where(valid, w_inter, 0.0)
    mnew_ref[...] = jnp.where(valid, m_new, 0.0)


def _ml_front_sample(x, mod, g, w, st, n, m):
    R, D = x.shape
    full = lambda shape: pl.BlockSpec(shape, lambda i: (0,) * len(shape))
    act = jax.ShapeDtypeStruct((R, INNER), f32)
    small = jax.ShapeDtypeStruct((R, LANES), f32)
    win = _col_specs(D, 2 * INNER)
    return pl.pallas_call(
        _grouped(_ml_front_sample_kernel, (1, 1, 1, len(win))),
        grid=(1,),
        in_specs=[
            full((R, D)), full((R, 3 * D)), full((1, D)), *win,
            full((MCONV * SUBLANES, INNER)),
            full((1, INNER)), full((HEADS, DH, DH)), full((HEADS, DH, DH)), full((HEADS, DH, DH)),
            full((3 * INNER, LANES)), full((1, LANES)),
            full((MCONV - 1, R, INNER)), full((R, INNER)), full((R, LANES)),
        ],
        out_specs=[full((R, INNER))] * 6 + [full((R, LANES))] * 4 + [full((MCONV - 1, R, INNER))],
        out_shape=[act] * 6 + [small] * 4 + [jax.ShapeDtypeStruct((MCONV - 1, R, INNER), f32)],
        compiler_params=_params("arbitrary"),
        name="ml_front_sample",
    )(x, mod, g, *[w["ml_w_in"]] * len(win), w["ml_w_conv"], w["ml_b_conv"], w["ml_w_q"], w["ml_w_k"],
      w["ml_w_v"], w["ml_w_igf"], w["ml_b_igf"], st, n, m)


def _ml_back_sample_kernel(a1_ref, a2_ref, v_ref, qc_ref, lng_ref, xc_ref, z_ref, x_ref, mod_ref, skip_ref,
                           wout_ref, o_ref):
    hs = []
    for h in range(HEADS):
        seg = slice(h * DH, (h + 1) * DH)
        hc = a1_ref[:, h:h + 1] * v_ref[:, seg] + a2_ref[:, h:h + 1] * qc_ref[:, seg]
        hs.append(_layernorm(hc, lng_ref[:, seg]))
    hh = jnp.concatenate(hs, axis=1)
    act = _ml_act(hh, xc_ref[...], z_ref[...], skip_ref[...])
    o_ref[...] = _ml_residual(act, x_ref[...], mod_ref[...], wout_ref)


def _ml_back_sample(a1, a2, v, qc, xc, z, x, mod, w):
    R, D = x.shape
    full = lambda shape: pl.BlockSpec(shape, lambda i: (0,) * len(shape))
    wout = _col_specs(INNER, D)
    return pl.pallas_call(
        _grouped(_ml_back_sample_kernel, (1,) * 10 + (len(wout),)),
        grid=(1,),
        in_specs=[full((R, LANES)), full((R, LANES)), full((R, INNER)), full((R, INNER)), full((1, INNER)),
                  full((R, INNER)), full((R, INNER)), full((R, D)), full((R, 3 * D)), full((1, INNER)),
                  *wout],
        out_specs=full((R, D)),
        out_shape=jax.ShapeDtypeStruct((R, D), f32),
        compiler_params=_params("arbitrary"),
        name="ml_back_sample",
    )(a1, a2, v, qc, w["ml_ln_g"], xc, z, x, mod, w["ml_skip"], *[w["ml_w_out"]] * len(wout))


def _cf_in(x, mod, g_ref, win_ref, bin_ref):
    h = _rms_mod(x, g_ref[...], mod)
    agz = _add_rows(_dot_cols(_bf(h), win_ref), bin_ref[...])
    u = agz[:, :INNER] * _sigmoid(agz[:, INNER:2 * INNER])
    return u, agz[:, 2 * INNER:]


def _cf_out(y, z, x, mod, lng_ref, lnb_ref, wout_ref, fg_ref):
    yn = _layernorm(y, lng_ref[...], lnb_ref[...])
    out = _dot_cols(_bf(_silu(yn) * _silu(z)), wout_ref)
    return _rms(x + _mul_rows(out, mod[:, 2 * D_MODEL:]), fg_ref[...])


def _channel_block(cb):
    start = cb * CONV_COLS
    return pl.ds(start if isinstance(cb, int) else pl.multiple_of(start, CONV_COLS), CONV_COLS)


def _dwconv_block(ubuf_ref, sh_ref, wb_ref, bdw_ref, y_ref, tm, cols):
    base = CCONV_HIST - (CCONV - 1)
    max_row_off = (base + CCONV - 2) // SUBLANES * SUBLANES
    for r in range(1, SUBLANES):
        sh_ref[r - 1] = ubuf_ref[r:r + tm + max_row_off, cols]
    b = bdw_ref[:, cols]
    for rc in range(tm // CONV_ROWS):
        acc = jnp.broadcast_to(b, (CONV_ROWS, CONV_COLS))
        for j in range(CCONV):
            a, r = divmod(base + j, SUBLANES)
            rows = pl.ds(SUBLANES * a + CONV_ROWS * rc, CONV_ROWS)
            src = ubuf_ref[rows, cols] if r == 0 else sh_ref[r - 1, rows, :]
            acc = acc + _mul_rows(src, wb_ref[SUBLANES * j:SUBLANES * (j + 1), cols])
        y_ref[pl.ds(CONV_ROWS * rc, CONV_ROWS), cols] = acc


def _memory_step(step, dec_ref, c_ref, q_ref, kw_ref, v_ref, cout_ref):
    head = (step // 2) % HEADS
    first = (step // (2 * HEADS)) * STEP_BATCH
    qt = q_ref[...].T
    kwt = kw_ref[...].T
    parts = []
    for i in range(STEP_BATCH):
        decay = dec_ref[(first + i) * HEADS + head]
        c = c_ref[i]
        parts.append(jnp.sum(c * qt[:, i:i + 1], axis=0, keepdims=True))
        cout_ref[i] = decay * c + kwt[:, i:i + 1] * v_ref[i:i + 1, :]
    return jnp.concatenate(parts, axis=0)


def _cf_prompt_kernel(x_ref, mod_ref, g_ref, win_ref, bin_ref, wdw_ref, bdw_ref, lng_ref, lnb_ref, wout_ref,
                      fg_ref, dec_ref, cs_ref, qs_ref, kws_ref, vs_ref,
                      o_ref, cconv_ref, csout_ref, qc_ref, ubuf_ref, y_ref, z_ref, sh_ref, wb_ref, *, tm):
    step = pl.program_id(0) * pl.num_programs(1) + pl.program_id(1)
    nblk = INNER // CONV_COLS

    @pl.when(step == 0)
    def _():
        for j in range(CCONV):
            wb_ref[SUBLANES * j:SUBLANES * (j + 1), :] = jnp.broadcast_to(wdw_ref[j:j + 1, :], (SUBLANES, INNER))

    @pl.when(pl.program_id(1) == 0)
    def _():
        ubuf_ref[0:CCONV_HIST, :] = jnp.zeros((CCONV_HIST, INNER), f32)

    def conv_blocks(lo, hi):
        def block_body(cb, carry):
            _dwconv_block(ubuf_ref, sh_ref, wb_ref, bdw_ref, y_ref, tm, _channel_block(cb))
            return carry

        lax.fori_loop(lo, hi, block_body, 0)

    @pl.when(step % 2 == 0)
    def _():
        qc_ref[...] = _memory_step(step, dec_ref, cs_ref, qs_ref, kws_ref, vs_ref, csout_ref)
        u, z = _cf_in(x_ref[...], mod_ref[...], g_ref, win_ref, bin_ref)
        ubuf_ref[CCONV_HIST:CCONV_HIST + tm, :] = u
        z_ref[...] = z
        conv_blocks(0, nblk // 2)

    @pl.when(step % 2 == 1)
    def _():
        qc_ref[...] += _memory_step(step, dec_ref, cs_ref, qs_ref, kws_ref, vs_ref, csout_ref)
        conv_blocks(nblk // 2, nblk)
        cconv_ref[...] = ubuf_ref[CCONV_HIST + tm - (CCONV - 1):CCONV_HIST + tm, :]
        ubuf_ref[0:CCONV_HIST, :] = ubuf_ref[tm:tm + CCONV_HIST, :]
        o_ref[...] = _cf_out(y_ref[...], z_ref[...], x_ref[...], mod_ref[...], lng_ref, lnb_ref, wout_ref, fg_ref)


def _cf_prompt(x, mod4, g, w, final_g, dec_flat, C_s, q_s, kw_s, v_s):
    B, T, D = x.shape
    R = C_s.shape[0]
    tm = ROW_TILE
    steps_t = 2 * (T // tm)
    assert B * steps_t * STEP_BATCH == R * HEADS * 2, "one (sequence group, head, key half) per grid step"
    row = lambda b, t: (b, t // 2, 0)

    def ghk(b, t):
        s = b * steps_t + t
        return s // (2 * HEADS), (s // 2) % HEADS, s % 2

    c_spec = pl.BlockSpec((STEP_BATCH, None, DH // 2, DH), lambda b, t: (*ghk(b, t), 0))
    half_spec = pl.BlockSpec((STEP_BATCH, DH // 2), lambda b, t: (ghk(b, t)[0], 2 * ghk(b, t)[1] + ghk(b, t)[2]))
    head_spec = pl.BlockSpec((STEP_BATCH, DH), lambda b, t: ghk(b, t)[:2])
    win, wout = _col_specs(D, 3 * INNER), _col_specs(INNER, D)
    return pl.pallas_call(
        functools.partial(_grouped(_cf_prompt_kernel, (1, 1, 1, len(win), 1, 1, 1, 1, 1, len(wout))), tm=tm),
        grid=(B, steps_t),
        in_specs=[
            pl.BlockSpec((None, tm, D), row),
            pl.BlockSpec((None, None, SUBLANES, 3 * D), lambda b, t: (1, b, 0, 0)),
            _const_spec((SUBLANES, D)),
            *win,
            _const_spec((SUBLANES, 3 * INNER)),
            _const_spec((CCONV, INNER)),
            _const_spec((1, INNER)),
            _const_spec((SUBLANES, INNER)),
            _const_spec((SUBLANES, INNER)),
            *wout,
            _const_spec((SUBLANES, D)),
            pl.BlockSpec(memory_space=pltpu.SMEM),
            c_spec, half_spec, half_spec, head_spec,
        ],
        out_specs=[
            pl.BlockSpec((None, tm, D), row),
            pl.BlockSpec((None, CCONV - 1, INNER), lambda b, t: (b, 0, 0)),
            c_spec,
            head_spec,
        ],
        out_shape=[
            jax.ShapeDtypeStruct((B, T, D), f32),
            jax.ShapeDtypeStruct((B, CCONV - 1, INNER), f32),
            jax.ShapeDtypeStruct(C_s.shape, f32),
            jax.ShapeDtypeStruct((R, INNER), f32),
        ],
        scratch_shapes=[
            pltpu.VMEM((CCONV_HIST + tm, INNER), f32),
            pltpu.VMEM((tm, INNER), f32),
            pltpu.VMEM((tm, INNER), f32),
            pltpu.VMEM((SUBLANES - 1, tm + CCONV_HIST - SUBLANES, CONV_COLS), f32),
            pltpu.VMEM((SUBLANES * CCONV, INNER), f32),
        ],
        compiler_params=_params("arbitrary", "arbitrary"),
        name="cf_prompt",
    )(x, mod4, g, *[w["cf_w_in"]] * len(win), w["cf_b_in"], w["cf_w_dw"], w["cf_b_dw"], w["cf_ln_g"],
      w["cf_ln_b"], *[w["cf_w_out"]] * len(wout), final_g, dec_flat, C_s, q_s, kw_s, v_s)


def _cf_sample_kernel(x_ref, mod_ref, g_ref, win_ref, bin_ref, wdw_ref, bdw_ref, lng_ref, lnb_ref, wout_ref,
                      fg_ref, st_ref, o_ref, stout_ref, u_ref, z_ref, y_ref, *, tb):
    i = pl.program_id(0)

    @pl.when(i == 0)
    def _():
        u, z = _cf_in(x_ref[...], mod_ref[...], g_ref, win_ref, bin_ref)
        u_ref[...] = u
        z_ref[...] = z

    rows = pl.ds(pl.multiple_of(i * tb, tb), tb)
    u = u_ref[rows, :]
    acc = bdw_ref[...] + wdw_ref[CCONV - 1:CCONV, :] * u
    for j in range(CCONV - 1):
        acc = acc + wdw_ref[j:j + 1, :] * st_ref[j]
    y_ref[rows, :] = acc
    stout_ref[0:CCONV - 2] = st_ref[1:CCONV - 1]
    stout_ref[CCONV - 2] = u

    @pl.when(i == pl.num_programs(0) - 1)
    def _():
        o_ref[...] = _cf_out(y_ref[...], z_ref[...], x_ref[...], mod_ref[...], lng_ref, lnb_ref, wout_ref, fg_ref)


def _cf_sample(x, mod, g, w, final_g, st):
    R, D = x.shape
    tb = SAMPLE_TILE
    state = pl.BlockSpec((CCONV - 1, tb, INNER), lambda i: (0, i, 0))
    win, wout = _col_specs(D, 3 * INNER), _col_specs(INNER, D)
    return pl.pallas_call(
        functools.partial(_grouped(_cf_sample_kernel, (1, 1, 1, len(win), 1, 1, 1, 1, 1, len(wout))), tb=tb),
        grid=(R // tb,),
        in_specs=[
            _const_spec((R, D)),
            _const_spec((R, 3 * D)),
            _const_spec((1, D)),
            *win,
            _const_spec((1, 3 * INNER)),
            _const_spec((CCONV, INNER)),
            _const_spec((1, INNER)),
            _const_spec((1, INNER)),
            _const_spec((1, INNER)),
            *wout,
            _const_spec((1, D)),
            state,
        ],
        out_specs=[pl.BlockSpec((R, D), lambda i: (0, 0)), state],
        out_shape=[jax.ShapeDtypeStruct((R, D), f32), jax.ShapeDtypeStruct((CCONV - 1, R, INNER), f32)],
        scratch_shapes=[pltpu.VMEM((R, INNER), f32)] * 3,
        compiler_params=_params("arbitrary"),
        name="cf_sample",
    )(x, mod, g, *[w["cf_w_in"]] * len(win), w["cf_b_in"], w["cf_w_dw"], w["cf_b_dw"], w["cf_ln_g"],
      w["cf_ln_b"], *[w["cf_w_out"]] * len(wout), final_g, st)


def kernel(x_prompt, x_sample, c_prompt, c_sample, state_mlstm_C, state_mlstm_n, state_mlstm_m, state_mlstm_conv, state_conf_conv, norm_g, w_ada, b_ada, ml_w_in, ml_w_conv, ml_b_conv, ml_w_q, ml_w_k, ml_w_v, ml_w_ig, ml_b_ig, ml_w_fg, ml_b_fg, ml_ln_g, ml_skip, ml_w_out, cf_w_in, cf_b_in, cf_w_dw, cf_b_dw, cf_ln_g, cf_ln_b, cf_w_out, final_g):
    B = x_prompt.shape[0]
    R = x_sample.shape[0]
    D = D_MODEL

    gate_pad = LANES - 2 * HEADS
    w = {
        "ml_w_in": _bf(ml_w_in[0]),
        "ml_w_conv": jnp.repeat(ml_w_conv[0], SUBLANES, axis=0),
        "ml_b_conv": ml_b_conv,
        "ml_w_q": _bf(ml_w_q[0]),
        "ml_w_k": _bf(ml_w_k[0]),
        "ml_w_v": _bf(ml_w_v[0]),
        "ml_w_igf": _bf(jnp.pad(jnp.concatenate([ml_w_ig[0], ml_w_fg[0]], axis=1), ((0, 0), (0, gate_pad)))),
        "ml_b_igf": jnp.pad(jnp.concatenate([ml_b_ig, ml_b_fg], axis=1), ((0, 0), (0, gate_pad))),        "ml_ln_g": ml_ln_g.reshape(1, INNER),
        "ml_skip": ml_skip,
        "ml_w_out": _bf(ml_w_out[0]),
        "cf_w_in": _bf(cf_w_in[0]),
        "cf_b_in": cf_b_in,
        "cf_w_dw": cf_w_dw[0],
        "cf_b_dw": cf_b_dw,
        "cf_ln_g": cf_ln_g,
        "cf_ln_b": cf_ln_b,
        "cf_w_out": _bf(cf_w_out[0]),
    }
    final_g2 = final_g.reshape(1, D)
    wp = dict(w, **{name: _rep(w[name]) for name in
                    ("ml_b_conv", "ml_ln_g", "ml_skip", "cf_b_in", "cf_ln_g", "cf_ln_b")})

    mod = _ada(jnp.concatenate([c_prompt, c_sample], axis=0), w_ada, b_ada)
    mod_p = jnp.broadcast_to(mod[:, :B, None, :], (mod.shape[0], B, SUBLANES, 3 * D))
    mod_s = mod[:, B:]

    tap_major = lambda a: jnp.transpose(a, (1, 0, 2))
    xs = x_sample.reshape(R, D)
    m_pad = jnp.pad(state_mlstm_m[0], ((0, 0), (0, LANES - HEADS)))
    (q_s, kw_s, v_s, xc_s, z_s, n_s, a1, a2, dec, m_s, mconv_s) = _ml_front_sample(
        xs, mod_s[0], norm_g[0:1], w, tap_major(state_mlstm_conv[0]), state_mlstm_n[0].reshape(R, INNER), m_pad)

    x1, mconv_p, C_p, n_p, m_p = _ml_prompt(x_prompt, mod_p, _rep(norm_g[0]), wp)
    y_prompt, cconv_p, C_s, qc = _cf_prompt(
        x1, mod_p, _rep(norm_g[1]), wp, _rep(final_g), dec[:, :HEADS].reshape(R * HEADS), state_mlstm_C[0],
        q_s, kw_s, v_s)

    x1_s = _ml_back_sample(a1, a2, v_s, qc, xc_s, z_s, xs, mod_s[0], w)
    y_sample, cconv_s = _cf_sample(x1_s, mod_s[1], norm_g[1:2], w, final_g2, tap_major(state_conf_conv[0]))

    return (
        y_prompt,
        y_sample.reshape(R, 1, D),
        C_p[None],
        C_s[None],
        n_p[None],
        n_s.reshape(1, R, HEADS, DH),
        m_p[:, 0, :HEADS][None],
        m_s[:, :HEADS][None],
        mconv_p[None],
        tap_major(mconv_s)[None],
        cconv_p[None],
        tap_major(cconv_s)[None],
    )
```

```python
import functools

import jax
import jax.numpy as jnp
from jax import lax
from jax.experimental import pallas as pl
from jax.experimental.pallas import tpu as pltpu

D_MODEL = 1024
INNER = 2 * D_MODEL
HEADS = 4
DH = INNER // HEADS
MCONV = 4
CCONV = 31
EPS = 1e-6
NEG = -1e30
K_SCALE = DH ** -0.5

LANES = 128
SUBLANES = 8
VMEM_LIMIT = 56 * 2 ** 20

ROW_TILE = 256
CHUNK = 256
STEP_BATCH = 8
CONV_ROWS = 64
CONV_COLS = 128
CCONV_HIST = 32
SAMPLE_TILE = 16
W_COLS = 512

f32 = jnp.float32
bf16 = jnp.bfloat16


def _bf(x):
    return x.astype(bf16)


def _dot(a, b):
    return jnp.dot(a, b, preferred_element_type=f32)


def _dot_cols(a, w_refs):
    return jnp.concatenate([_dot(a, r[...]) for r in w_refs], axis=1)


def _col_specs(k, n):
    return [pl.BlockSpec((k, W_COLS), lambda *_, c=c: (0, c), pipeline_mode=pl.Buffered(1))
            for c in range(n // W_COLS)]


def _grouped(kernel, counts):
    def body(*refs, **kw):
        it = iter(refs)
        args = [next(it) if n == 1 else tuple(next(it) for _ in range(n)) for n in counts]
        return kernel(*args, *it, **kw)
    return body


def _sigmoid(x):
    return 0.5 * jnp.tanh(0.5 * x) + 0.5


def _silu(x):
    h = 0.5 * x
    return h * jnp.tanh(h) + h


def _log_sigmoid(x):
    return jnp.minimum(x, 0.0) - jnp.log1p(jnp.exp(-jnp.abs(x)))


def _rows_op(op, x, r):
    m, n = x.shape
    if r.shape[0] in (1, m):
        return op(x, r)
    return op(x.reshape(m // SUBLANES, SUBLANES, n), r[None]).reshape(m, n)


def _mul_rows(x, r):
    return _rows_op(jnp.multiply, x, r)


def _add_rows(x, r):
    return _rows_op(jnp.add, x, r)


def _rep(row):
    return jnp.broadcast_to(row.reshape(1, -1), (SUBLANES, row.size))


def _rms(x, g):
    return _mul_rows(x * lax.rsqrt(jnp.mean(x * x, axis=-1, keepdims=True) + EPS), g)


def _rms_mod(x, g, mod):
    return _add_rows(_mul_rows(_rms(x, g), 1.0 + mod[:, D_MODEL:2 * D_MODEL]), mod[:, :D_MODEL])


def _layernorm(x, g, b=None):
    mu = jnp.mean(x, axis=-1, keepdims=True)
    xc = x - mu
    var = jnp.mean(xc * xc, axis=-1, keepdims=True)
    y = _mul_rows(xc * lax.rsqrt(var + EPS), g)
    return y if b is None else _add_rows(y, b)


def _const_spec(shape):
    n = len(shape)
    return pl.BlockSpec(shape, lambda *_: (0,) * n, pipeline_mode=pl.Buffered(1))


def _params(*sem):
    return pltpu.CompilerParams(dimension_semantics=sem, vmem_limit_bytes=VMEM_LIMIT)


def _ada_kernel(c_ref, w_ref, b_ref, o_ref):
    o_ref[...] = _dot(_bf(_silu(c_ref[...])), _bf(w_ref[...])) + b_ref[...]


def _ada(c_all, w_ada, b_ada):
    depth, d, d3 = w_ada.shape
    rows = c_all.shape[0]
    tn = D_MODEL
    return pl.pallas_call(
        _ada_kernel,
        grid=(depth, d3 // tn),
        in_specs=[
            pl.BlockSpec((rows, d), lambda i, j: (0, 0)),
            pl.BlockSpec((None, d, tn), lambda i, j: (i, 0, j)),
            pl.BlockSpec((None, 1, tn), lambda i, j: (i, 0, j)),
        ],
        out_specs=pl.BlockSpec((None, rows, tn), lambda i, j: (i, 0, j)),
        out_shape=jax.ShapeDtypeStruct((depth, rows, d3), f32),
        compiler_params=_params("arbitrary", "arbitrary"),
        name="ada",
    )(c_all, w_ada, b_ada.reshape(depth, 1, d3))


def _ml_proj(xm, xc, wq_ref, wk_ref, wv_ref):
    xcb, xmb = _bf(xc), _bf(xm)
    qs, ks, vs = [], [], []
    for h in range(HEADS):
        seg = slice(h * DH, (h + 1) * DH)
        qs.append(_dot(xcb[:, seg], wq_ref[h]))
        ks.append(_dot(xcb[:, seg], wk_ref[h]))
        vs.append(_dot(xmb[:, seg], wv_ref[h]))
    return jnp.concatenate(qs, axis=1), jnp.concatenate(ks, axis=1), jnp.concatenate(vs, axis=1)


def _ml_qkv(xm, xc, wq_ref, wk_ref, wv_ref, wigf_ref, bigf_ref):
    q, k, v = _ml_proj(xm, xc, wq_ref, wk_ref, wv_ref)
    pre = (_dot(_bf(q), wigf_ref[0:INNER, :]) + _dot(_bf(k), wigf_ref[INNER:2 * INNER, :])
           + _dot(_bf(v), wigf_ref[2 * INNER:3 * INNER, :]) + bigf_ref[...])
    lane = lax.broadcasted_iota(jnp.int32, pre.shape, 1)
    gates = jnp.where(lane >= HEADS, _log_sigmoid(pre), pre)
    return q, k, v, gates


def _ml_act(hh, xc, z, skip):
    return _bf((hh + _mul_rows(xc, skip)) * _silu(z))


def _ml_residual(act, x, mod, wout_ref):
    return x + _mul_rows(_dot_cols(act, wout_ref), mod[:, 2 * D_MODEL:])


def _tap(wconv_ref, j):
    return wconv_ref[SUBLANES * j:SUBLANES * (j + 1), :]


def _ml_conv_prompt(xm, wconv_ref, bconv_ref, xbuf_ref, tm):
    xbuf_ref[SUBLANES:SUBLANES + tm, :] = xm
    acc = _add_rows(_mul_rows(xm, _tap(wconv_ref, MCONV - 1)), bconv_ref[...])
    for j in range(MCONV - 1):
        start = SUBLANES - (MCONV - 1) + j
        acc = acc + _mul_rows(xbuf_ref[start:start + tm, :], _tap(wconv_ref, j))
    return acc


def _mlstm_chunk(q, ks, v, gates, lng_ref, C_ref, n_ref, m_ref):
    L = q.shape[0]
    row = lax.broadcasted_iota(jnp.int32, (L, L), 0)
    col = lax.broadcasted_iota(jnp.int32, (L, L), 1)
    causal = row >= col
    tri = jnp.where(causal, 1.0, 0.0).astype(bf16)
    hi = _bf(gates)
    lo = _bf(gates - hi.astype(f32))
    bcum = _dot(tri, hi) + _dot(tri, lo)
    gates_t = gates.T
    bcum_t = bcum.T

    hs = []
    for h in range(HEADS):
        seg = slice(h * DH, (h + 1) * DH)
        ig_col = gates[:, h:h + 1]
        b_col = bcum[:, HEADS + h:HEADS + h + 1]
        ig_row = gates_t[h:h + 1, :]
        b_row = bcum_t[HEADS + h:HEADS + h + 1, :]
        m_prev = m_ref[:, h:h + 1]
        qh, kh, vh = q[:, seg], ks[:, seg], v[:, seg]

        dmat = jnp.where(causal, b_col - b_row + ig_row, NEG)
        inter = b_col + m_prev
        m_t = jnp.maximum(inter, jnp.max(dmat, axis=1, keepdims=True))
        w_intra = jnp.exp(dmat - m_t)
        w_inter = jnp.exp(inter - m_t)
        s = lax.dot_general(qh, kh, (((1,), (1,)), ((), ())), preferred_element_type=f32) * w_intra
        num = _dot(_bf(s), vh) + w_inter * _dot(qh, _bf(C_ref[h]))
        qn = jnp.sum(qh.astype(f32) * n_ref[h:h + 1, :], axis=1, keepdims=True)
        den = jnp.sum(s, axis=1, keepdims=True) + w_inter * qn
        hc = num / jnp.maximum(jnp.abs(den), jnp.exp(-m_t))
        hs.append(_layernorm(hc, lng_ref[:, seg]))

        m_new = m_t[L - 1:L, :]
        b_last = b_col[L - 1:L, :]
        decay = jnp.exp(b_last + m_prev - m_new)
        kw = kh.astype(f32) * jnp.exp(b_last - b_col + ig_col - m_new)
        C_ref[h] = decay * C_ref[h] + lax.dot_general(_bf(kw), vh, (((0,), (0,)), ((), ())),
                                                      preferred_element_type=f32)
        n_ref[h:h + 1, :] = decay * n_ref[h:h + 1, :] + jnp.sum(kw, axis=0, keepdims=True)
        m_ref[:, h:h + 1] = m_new
    return jnp.concatenate(hs, axis=1)


def _ml_prompt_kernel(x_ref, mod_ref, g_ref, win_ref, wconv_ref, bconv_ref, wq_ref, wk_ref, wv_ref, wigf_ref,
                      bigf_ref, lng_ref, skip_ref, wout_ref,
                      o_ref, mconv_ref, C_ref, n_ref, m_ref, xbuf_ref, *, tm):
    @pl.when(pl.program_id(1) == 0)
    def _():
        xbuf_ref[0:SUBLANES, :] = jnp.zeros((SUBLANES, INNER), f32)
        C_ref[...] = jnp.zeros_like(C_ref)
        n_ref[...] = jnp.zeros_like(n_ref)
        m_ref[...] = jnp.zeros_like(m_ref)

    x = x_ref[...]
    mod = mod_ref[...]
    xmz = _dot_cols(_bf(_rms_mod(x, g_ref[...], mod)), win_ref)
    xm = xmz[:, :INNER]
    z = xmz[:, INNER:]
    xc = _silu(_ml_conv_prompt(xm, wconv_ref, bconv_ref, xbuf_ref, tm))
    mconv_ref[...] = xbuf_ref[SUBLANES + tm - (MCONV - 1):SUBLANES + tm, :]
    xbuf_ref[0:SUBLANES, :] = xbuf_ref[tm:tm + SUBLANES, :]

    q, k, v, gates = _ml_qkv(xm, xc, wq_ref, wk_ref, wv_ref, wigf_ref, bigf_ref)
    hh = _mlstm_chunk(_bf(q), _bf(k * K_SCALE), _bf(v), gates, lng_ref, C_ref, n_ref, m_ref)
    o_ref[...] = _ml_residual(_ml_act(hh, xc, z, skip_ref[...]), x, mod, wout_ref)


def _ml_prompt(x, mod4, g, w):
    B, T, D = x.shape
    tm = CHUNK
    row = lambda b, t: (b, t, 0)
    win, wout = _col_specs(D, 2 * INNER), _col_specs(INNER, D)
    return pl.pallas_call(
        functools.partial(_grouped(_ml_prompt_kernel, (1, 1, 1, len(win), 1, 1, 1, 1, 1, 1, 1, 1, 1, len(wout))),
                          tm=tm),
        grid=(B, T // tm),
        in_specs=[
            pl.BlockSpec((None, tm, D), row),
            pl.BlockSpec((None, None, SUBLANES, 3 * D), lambda b, t: (0, b, 0, 0)),
            _const_spec((SUBLANES, D)),
            *win,
            _const_spec((MCONV * SUBLANES, INNER)),
            _const_spec((SUBLANES, INNER)),
            _const_spec((HEADS, DH, DH)),
            _const_spec((HEADS, DH, DH)),
            _const_spec((HEADS, DH, DH)),
            _const_spec((3 * INNER, LANES)),
            _const_spec((1, LANES)),
            _const_spec((SUBLANES, INNER)),
            _const_spec((SUBLANES, INNER)),
            *wout,
        ],
        out_specs=[
            pl.BlockSpec((None, tm, D), row),
            pl.BlockSpec((None, MCONV - 1, INNER), lambda b, t: (b, 0, 0)),
            pl.BlockSpec((None, HEADS, DH, DH), lambda b, t: (b, 0, 0, 0)),
            pl.BlockSpec((None, HEADS, DH), lambda b, t: (b, 0, 0)),
            pl.BlockSpec((None, 1, LANES), lambda b, t: (b, 0, 0)),
        ],
        out_shape=[
            jax.ShapeDtypeStruct((B, T, D), f32),
            jax.ShapeDtypeStruct((B, MCONV - 1, INNER), f32),
            jax.ShapeDtypeStruct((B, HEADS, DH, DH), f32),
            jax.ShapeDtypeStruct((B, HEADS, DH), f32),
            jax.ShapeDtypeStruct((B, 1, LANES), f32),
        ],
        scratch_shapes=[pltpu.VMEM((SUBLANES + tm, INNER), f32)],
        compiler_params=_params("arbitrary", "arbitrary"),
        name="ml_prompt",
    )(x, mod4, g, *[w["ml_w_in"]] * len(win), w["ml_w_conv"], w["ml_b_conv"], w["ml_w_q"], w["ml_w_k"],
      w["ml_w_v"], w["ml_w_igf"], w["ml_b_igf"], w["ml_ln_g"], w["ml_skip"], *[w["ml_w_out"]] * len(wout))


def _ml_front_sample_kernel(x_ref, mod_ref, g_ref, win_ref, wconv_ref, bconv_ref, wq_ref, wk_ref, wv_ref,
                            wigf_ref, bigf_ref, st_ref, n_ref, m_ref,
                            q_ref, kw_ref, v_ref, xc_ref, z_ref, nnew_ref, a1_ref, a2_ref, dec_ref, mnew_ref,
                            mconv_ref):
    h = _rms_mod(x_ref[...], g_ref[...], mod_ref[...])
    xmz = _dot_cols(_bf(h), win_ref)
    xm = xmz[:, :INNER]
    z_ref[...] = xmz[:, INNER:]
    acc = _add_rows(_mul_rows(xm, _tap(wconv_ref, MCONV - 1)), bconv_ref[...])
    for j in range(MCONV - 1):
        acc = acc + _mul_rows(st_ref[j], _tap(wconv_ref, j))
    xc = _silu(acc)
    xc_ref[...] = xc
    mconv_ref[0:MCONV - 2] = st_ref[1:MCONV - 1]
    mconv_ref[MCONV - 2] = xm

    q, k, v, gates = _ml_qkv(xm, xc, wq_ref, wk_ref, wv_ref, wigf_ref, bigf_ref)
    ks = k * K_SCALE
    q_ref[...] = q
    v_ref[...] = v

    ig = gates
    lf = pltpu.roll(gates, LANES - HEADS, axis=1)
    inter = lf + m_ref[...]
    m_new = jnp.maximum(inter, ig)
    w_intra = jnp.exp(ig - m_new)
    w_inter = jnp.exp(inter - m_new)
    lane = lax.broadcasted_iota(jnp.int32, ig.shape, 1)
    qk = jnp.zeros_like(ig)
    qn = jnp.zeros_like(ig)
    for h in range(HEADS):
        seg = slice(h * DH, (h + 1) * DH)
        qk = jnp.where(lane == h, jnp.sum(q[:, seg] * ks[:, seg], axis=1, keepdims=True), qk)
        qn = jnp.where(lane == h, jnp.sum(q[:, seg] * n_ref[:, seg], axis=1, keepdims=True), qn)
        kw = w_intra[:, h:h + 1] * ks[:, seg]
        kw_ref[:, seg] = kw
        nnew_ref[:, seg] = w_inter[:, h:h + 1] * n_ref[:, seg] + kw
    s = qk * w_intra
    den = s + w_inter * qn
    dn = jnp.maximum(jnp.abs(den), jnp.exp(-m_new))
    valid = lane < HEADS
    a1_ref[...] = jnp.where(valid, s / dn, 0.0)
    a2_ref[...] = jnp.where(valid, w_inter / dn, 0.0)
    dec_ref[...] = jnp.where(valid, w_inter, 0.0)
    mnew_ref[...] = jnp.where(valid, m_new, 0.0)


def _ml_front_sample(x, mod, g, w, st, n, m):
    R, D = x.shape
    full = lambda shape: pl.BlockSpec(shape, lambda i: (0,) * len(shape))
    act = jax.ShapeDtypeStruct((R, INNER), f32)
    small = jax.ShapeDtypeStruct((R, LANES), f32)
    win = _col_specs(D, 2 * INNER)
    return pl.pallas_call(
        _grouped(_ml_front_sample_kernel, (1, 1, 1, len(win))),
        grid=(1,),
        in_specs=[
            full((R, D)), full((R, 3 * D)), full((1, D)), *win,
            full((MCONV * SUBLANES, INNER)),
            full((1, INNER)), full((HEADS, DH, DH)), full((HEADS, DH, DH)), full((HEADS, DH, DH)),
            full((3 * INNER, LANES)), full((1, LANES)),
            full((MCONV - 1, R, INNER)), full((R, INNER)), full((R, LANES)),
        ],
        out_specs=[full((R, INNER))] * 6 + [full((R, LANES))] * 4 + [full((MCONV - 1, R, INNER))],
        out_shape=[act] * 6 + [small] * 4 + [jax.ShapeDtypeStruct((MCONV - 1, R, INNER), f32)],
        compiler_params=_params("arbitrary"),
        name="ml_front_sample",
    )(x, mod, g, *[w["ml_w_in"]] * len(win), w["ml_w_conv"], w["ml_b_conv"], w["ml_w_q"], w["ml_w_k"],
      w["ml_w_v"], w["ml_w_igf"], w["ml_b_igf"], st, n, m)


def _ml_back_sample_kernel(a1_ref, a2_ref, v_ref, qc_ref, lng_ref, xc_ref, z_ref, x_ref, mod_ref, skip_ref,
                           wout_ref, o_ref):
    hs = []
    for h in range(HEADS):
        seg = slice(h * DH, (h + 1) * DH)
        hc = a1_ref[:, h:h + 1] * v_ref[:, seg] + a2_ref[:, h:h + 1] * qc_ref[:, seg]
        hs.append(_layernorm(hc, lng_ref[:, seg]))
    hh = jnp.concatenate(hs, axis=1)
    act = _ml_act(hh, xc_ref[...], z_ref[...], skip_ref[...])
    o_ref[...] = _ml_residual(act, x_ref[...], mod_ref[...], wout_ref)


def _ml_back_sample(a1, a2, v, qc, xc, z, x, mod, w):
    R, D = x.shape
    full = lambda shape: pl.BlockSpec(shape, lambda i: (0,) * len(shape))
    wout = _col_specs(INNER, D)
    return pl.pallas_call(
        _grouped(_ml_back_sample_kernel, (1,) * 10 + (len(wout),)),
        grid=(1,),
        in_specs=[full((R, LANES)), full((R, LANES)), full((R, INNER)), full((R, INNER)), full((1, INNER)),
                  full((R, INNER)), full((R, INNER)), full((R, D)), full((R, 3 * D)), full((1, INNER)),
                  *wout],
        out_specs=full((R, D)),
        out_shape=jax.ShapeDtypeStruct((R, D), f32),
        compiler_params=_params("arbitrary"),
        name="ml_back_sample",
    )(a1, a2, v, qc, w["ml_ln_g"], xc, z, x, mod, w["ml_skip"], *[w["ml_w_out"]] * len(wout))


def _cf_in(x, mod, g_ref, win_ref, bin_ref):
    h = _rms_mod(x, g_ref[...], mod)
    agz = _add_rows(_dot_cols(_bf(h), win_ref), bin_ref[...])
    u = agz[:, :INNER] * _sigmoid(agz[:, INNER:2 * INNER])
    return u, agz[:, 2 * INNER:]


def _cf_out(y, z, x, mod, lng_ref, lnb_ref, wout_ref, fg_ref):
    yn = _layernorm(y, lng_ref[...], lnb_ref[...])
    out = _dot_cols(_bf(_silu(yn) * _silu(z)), wout_ref)
    return _rms(x + _mul_rows(out, mod[:, 2 * D_MODEL:]), fg_ref[...])


def _channel_block(cb):
    start = cb * CONV_COLS
    return pl.ds(start if isinstance(cb, int) else pl.multiple_of(start, CONV_COLS), CONV_COLS)


def _dwconv_block(ubuf_ref, sh_ref, wb_ref, bdw_ref, y_ref, tm, cols):
    base = CCONV_HIST - (CCONV - 1)
    max_row_off = (base + CCONV - 2) // SUBLANES * SUBLANES
    for r in range(1, SUBLANES):
        sh_ref[r - 1] = ubuf_ref[r:r + tm + max_row_off, cols]
    b = bdw_ref[:, cols]
    for rc in range(tm // CONV_ROWS):
        acc = jnp.broadcast_to(b, (CONV_ROWS, CONV_COLS))
        for j in range(CCONV):
            a, r = divmod(base + j, SUBLANES)
            rows = pl.ds(SUBLANES * a + CONV_ROWS * rc, CONV_ROWS)
            src = ubuf_ref[rows, cols] if r == 0 else sh_ref[r - 1, rows, :]
            acc = acc + _mul_rows(src, wb_ref[SUBLANES * j:SUBLANES * (j + 1), cols])
        y_ref[pl.ds(CONV_ROWS * rc, CONV_ROWS), cols] = acc


def _memory_step(step, dec_ref, c_ref, q_ref, kw_ref, v_ref, cout_ref):
    head = (step // 2) % HEADS
    first = (step // (2 * HEADS)) * STEP_BATCH
    qt = q_ref[...].T
    kwt = kw_ref[...].T
    parts = []
    for i in range(STEP_BATCH):
        decay = dec_ref[(first + i) * HEADS + head]
        c = c_ref[i]
        parts.append(jnp.sum(c * qt[:, i:i + 1], axis=0, keepdims=True))
        cout_ref[i] = decay * c + kwt[:, i:i + 1] * v_ref[i:i + 1, :]
    return jnp.concatenate(parts, axis=0)


def _cf_prompt_kernel(x_ref, mod_ref, g_ref, win_ref, bin_ref, wdw_ref, bdw_ref, lng_ref, lnb_ref, wout_ref,
                      fg_ref, dec_ref, cs_ref, qs_ref, kws_ref, vs_ref,
                      o_ref, cconv_ref, csout_ref, qc_ref, ubuf_ref, y_ref, sh_ref, wb_ref, *, tm):
    step = pl.program_id(0) * pl.num_programs(1) + pl.program_id(1)
    nblk = INNER // CONV_COLS

    @pl.when(step == 0)
    def _():
        for j in range(CCONV):
            wb_ref[SUBLANES * j:SUBLANES * (j + 1), :] = jnp.broadcast_to(wdw_ref[j:j + 1, :], (SUBLANES, INNER))

    @pl.when(pl.program_id(1) == 0)
    def _():
        ubuf_ref[0:CCONV_HIST, :] = jnp.zeros((CCONV_HIST, INNER), f32)

    def conv_blocks(lo, hi):
        def block_body(cb, carry):
            _dwconv_block(ubuf_ref, sh_ref, wb_ref, bdw_ref, y_ref, tm, _channel_block(cb))
            return carry

        lax.fori_loop(lo, hi, block_body, 0)

    n_ag = 2 * INNER // W_COLS

    def normed():
        return _bf(_rms_mod(x_ref[...], g_ref[...], mod_ref[...]))

    @pl.when(step % 2 == 0)
    def _():
        qc_ref[...] = _memory_step(step, dec_ref, cs_ref, qs_ref, kws_ref, vs_ref, csout_ref)
        ag = _add_rows(_dot_cols(normed(), win_ref[:n_ag]), bin_ref[:, :2 * INNER])
        ubuf_ref[CCONV_HIST:CCONV_HIST + tm, :] = ag[:, :INNER] * _sigmoid(ag[:, INNER:])
        conv_blocks(0, nblk // 2)

    @pl.when(step % 2 == 1)
    def _():
        qc_ref[...] += _memory_step(step, dec_ref, cs_ref, qs_ref, kws_ref, vs_ref, csout_ref)
        z = _add_rows(_dot_cols(normed(), win_ref[n_ag:]), bin_ref[:, 2 * INNER:])
        conv_blocks(nblk // 2, nblk)
        cconv_ref[...] = ubuf_ref[CCONV_HIST + tm - (CCONV - 1):CCONV_HIST + tm, :]
        ubuf_ref[0:CCONV_HIST, :] = ubuf_ref[tm:tm + CCONV_HIST, :]
        o_ref[...] = _cf_out(y_ref[...], z, x_ref[...], mod_ref[...], lng_ref, lnb_ref, wout_ref, fg_ref)


def _cf_prompt(x, mod4, g, w, final_g, dec_flat, C_s, q_s, kw_s, v_s):
    B, T, D = x.shape
    R = C_s.shape[0]
    tm = ROW_TILE
    steps_t = 2 * (T // tm)
    assert B * steps_t * STEP_BATCH == R * HEADS * 2, "one (sequence group, head, key half) per grid step"
    row = lambda b, t: (b, t // 2, 0)

    def ghk(b, t):
        s = b * steps_t + t
        return s // (2 * HEADS), (s // 2) % HEADS, s % 2

    c_spec = pl.BlockSpec((STEP_BATCH, None, DH // 2, DH), lambda b, t: (*ghk(b, t), 0))
    half_spec = pl.BlockSpec((STEP_BATCH, DH // 2), lambda b, t: (ghk(b, t)[0], 2 * ghk(b, t)[1] + ghk(b, t)[2]))
    head_spec = pl.BlockSpec((STEP_BATCH, DH), lambda b, t: ghk(b, t)[:2])
    win, wout = _col_specs(D, 3 * INNER), _col_specs(INNER, D)
    return pl.pallas_call(
        functools.partial(_grouped(_cf_prompt_kernel, (1, 1, 1, len(win), 1, 1, 1, 1, 1, len(wout))), tm=tm),
        grid=(B, steps_t),
        in_specs=[
            pl.BlockSpec((None, tm, D), row),
            pl.BlockSpec((None, None, SUBLANES, 3 * D), lambda b, t: (1, b, 0, 0)),
            _const_spec((SUBLANES, D)),
            *win,
            _const_spec((SUBLANES, 3 * INNER)),
            _const_spec((CCONV, INNER)),
            _const_spec((1, INNER)),
            _const_spec((SUBLANES, INNER)),
            _const_spec((SUBLANES, INNER)),
            *wout,
            _const_spec((SUBLANES, D)),
            pl.BlockSpec(memory_space=pltpu.SMEM),
            c_spec, half_spec, half_spec, head_spec,
        ],
        out_specs=[
            pl.BlockSpec((None, tm, D), row),
            pl.BlockSpec((None, CCONV - 1, INNER), lambda b, t: (b, 0, 0)),
            c_spec,
            head_spec,
        ],
        out_shape=[
            jax.ShapeDtypeStruct((B, T, D), f32),
            jax.ShapeDtypeStruct((B, CCONV - 1, INNER), f32),
            jax.ShapeDtypeStruct(C_s.shape, f32),
            jax.ShapeDtypeStruct((R, INNER), f32),
        ],
        scratch_shapes=[
            pltpu.VMEM((CCONV_HIST + tm, INNER), f32),
            pltpu.VMEM((tm, INNER), f32),
            pltpu.VMEM((SUBLANES - 1, tm + CCONV_HIST - SUBLANES, CONV_COLS), f32),
            pltpu.VMEM((SUBLANES * CCONV, INNER), f32),
        ],
        compiler_params=_params("arbitrary", "arbitrary"),
        name="cf_prompt",
    )(x, mod4, g, *[w["cf_w_in"]] * len(win), w["cf_b_in"], w["cf_w_dw"], w["cf_b_dw"], w["cf_ln_g"],
      w["cf_ln_b"], *[w["cf_w_out"]] * len(wout), final_g, dec_flat, C_s, q_s, kw_s, v_s)


def _cf_sample_kernel(x_ref, mod_ref, g_ref, win_ref, bin_ref, wdw_ref, bdw_ref, lng_ref, lnb_ref, wout_ref,
                      fg_ref, st_ref, o_ref, stout_ref, u_ref, z_ref, y_ref, *, tb):
    i = pl.program_id(0)

    @pl.when(i == 0)
    def _():
        u, z = _cf_in(x_ref[...], mod_ref[...], g_ref, win_ref, bin_ref)
        u_ref[...] = u
        z_ref[...] = z

    rows = pl.ds(pl.multiple_of(i * tb, tb), tb)
    u = u_ref[rows, :]
    acc = bdw_ref[...] + wdw_ref[CCONV - 1:CCONV, :] * u
    for j in range(CCONV - 1):
        acc = acc + wdw_ref[j:j + 1, :] * st_ref[j]
    y_ref[rows, :] = acc
    stout_ref[0:CCONV - 2] = st_ref[1:CCONV - 1]
    stout_ref[CCONV - 2] = u

    @pl.when(i == pl.num_programs(0) - 1)
    def _():
        o_ref[...] = _cf_out(y_ref[...], z_ref[...], x_ref[...], mod_ref[...], lng_ref, lnb_ref, wout_ref, fg_ref)


def _cf_sample(x, mod, g, w, final_g, st):
    R, D = x.shape
    tb = SAMPLE_TILE
    state = pl.BlockSpec((CCONV - 1, tb, INNER), lambda i: (0, i, 0))
    win, wout = _col_specs(D, 3 * INNER), _col_specs(INNER, D)
    return pl.pallas_call(
        functools.partial(_grouped(_cf_sample_kernel, (1, 1, 1, len(win), 1, 1, 1, 1, 1, len(wout))), tb=tb),
        grid=(R // tb,),
        in_specs=[
            _const_spec((R, D)),
            _const_spec((R, 3 * D)),
            _const_spec((1, D)),
            *win,
            _const_spec((1, 3 * INNER)),
            _const_spec((CCONV, INNER)),
            _const_spec((1, INNER)),
            _const_spec((1, INNER)),
            _const_spec((1, INNER)),
            *wout,
            _const_spec((1, D)),
            state,
        ],
        out_specs=[pl.BlockSpec((R, D), lambda i: (0, 0)), state],
        out_shape=[jax.ShapeDtypeStruct((R, D), f32), jax.ShapeDtypeStruct((CCONV - 1, R, INNER), f32)],
        scratch_shapes=[pltpu.VMEM((R, INNER), f32)] * 3,
        compiler_params=_params("arbitrary"),
        name="cf_sample",
    )(x, mod, g, *[w["cf_w_in"]] * len(win), w["cf_b_in"], w["cf_w_dw"], w["cf_b_dw"], w["cf_ln_g"],
      w["cf_ln_b"], *[w["cf_w_out"]] * len(wout), final_g, st)


def kernel(x_prompt, x_sample, c_prompt, c_sample, state_mlstm_C, state_mlstm_n, state_mlstm_m, state_mlstm_conv, state_conf_conv, norm_g, w_ada, b_ada, ml_w_in, ml_w_conv, ml_b_conv, ml_w_q, ml_w_k, ml_w_v, ml_w_ig, ml_b_ig, ml_w_fg, ml_b_fg, ml_ln_g, ml_skip, ml_w_out, cf_w_in, cf_b_in, cf_w_dw, cf_b_dw, cf_ln_g, cf_ln_b, cf_w_out, final_g):
    B = x_prompt.shape[0]
    R = x_sample.shape[0]
    D = D_MODEL

    gate_pad = LANES - 2 * HEADS
    w = {
        "ml_w_in": _bf(ml_w_in[0]),
        "ml_w_conv": jnp.repeat(ml_w_conv[0], SUBLANES, axis=0),
        "ml_b_conv": ml_b_conv,
        "ml_w_q": _bf(ml_w_q[0]),
        "ml_w_k": _bf(ml_w_k[0]),
        "ml_w_v": _bf(ml_w_v[0]),
        "ml_w_igf": _bf(jnp.pad(jnp.concatenate([ml_w_ig[0], ml_w_fg[0]], axis=1), ((0, 0), (0, gate_pad)))),
        "ml_b_igf": jnp.pad(jnp.concatenate([ml_b_ig, ml_b_fg], axis=1), ((0, 0), (0, gate_pad))),        "ml_ln_g": ml_ln_g.reshape(1, INNER),
        "ml_skip": ml_skip,
        "ml_w_out": _bf(ml_w_out[0]),
        "cf_w_in": _bf(cf_w_in[0]),
        "cf_b_in": cf_b_in,
        "cf_w_dw": cf_w_dw[0],
        "cf_b_dw": cf_b_dw,
        "cf_ln_g": cf_ln_g,
        "cf_ln_b": cf_ln_b,
        "cf_w_out": _bf(cf_w_out[0]),
    }
    final_g2 = final_g.reshape(1, D)
    wp = dict(w, **{name: _rep(w[name]) for name in
                    ("ml_b_conv", "ml_ln_g", "ml_skip", "cf_b_in", "cf_ln_g", "cf_ln_b")})

    mod = _ada(jnp.concatenate([c_prompt, c_sample], axis=0), w_ada, b_ada)
    mod_p = jnp.broadcast_to(mod[:, :B, None, :], (mod.shape[0], B, SUBLANES, 3 * D))
    mod_s = mod[:, B:]

    tap_major = lambda a: jnp.transpose(a, (1, 0, 2))
    xs = x_sample.reshape(R, D)
    m_pad = jnp.pad(state_mlstm_m[0], ((0, 0), (0, LANES - HEADS)))
    (q_s, kw_s, v_s, xc_s, z_s, n_s, a1, a2, dec, m_s, mconv_s) = _ml_front_sample(
        xs, mod_s[0], norm_g[0:1], w, tap_major(state_mlstm_conv[0]), state_mlstm_n[0].reshape(R, INNER), m_pad)

    x1, mconv_p, C_p, n_p, m_p = _ml_prompt(x_prompt, mod_p, _rep(norm_g[0]), wp)
    y_prompt, cconv_p, C_s, qc = _cf_prompt(
        x1, mod_p, _rep(norm_g[1]), wp, _rep(final_g), dec[:, :HEADS].reshape(R * HEADS), state_mlstm_C[0],
        q_s, kw_s, v_s)

    x1_s = _ml_back_sample(a1, a2, v_s, qc, xc_s, z_s, xs, mod_s[0], w)
    y_sample, cconv_s = _cf_sample(x1_s, mod_s[1], norm_g[1:2], w, final_g2, tap_major(state_conf_conv[0]))

    return (
        y_prompt,
        y_sample.reshape(R, 1, D),
        C_p[None],
        C_s[None],
        n_p[None],
        n_s.reshape(1, R, HEADS, DH),
        m_p[:, 0, :HEADS][None],
        m_s[:, :HEADS][None],
        mconv_p[None],
        tap_major(mconv_s)[None],
        cconv_p[None],
        tap_major(cconv_s)[None],
    )
```

```python
import functools

import jax
import jax.numpy as jnp
from jax import lax
from jax.experimental import pallas as pl
from jax.experimental.pallas import tpu as pltpu

D_MODEL = 1024
INNER = 2 * D_MODEL
HEADS = 4
DH = INNER // HEADS
MCONV = 4
CCONV = 31
EPS = 1e-6
NEG = -1e30
K_SCALE = DH ** -0.5

LANES = 128
SUBLANES = 8
VMEM_LIMIT = 56 * 2 ** 20

ROW_TILE = 256
CHUNK = 256
STEP_BATCH = 8
CONV_ROWS = 64
CONV_COLS = 128
CCONV_HIST = 32
SAMPLE_TILE = 16
W_COLS = 512

f32 = jnp.float32
bf16 = jnp.bfloat16


def _bf(x):
    return x.astype(bf16)


def _dot(a, b):
    return jnp.dot(a, b, preferred_element_type=f32)


def _dot_cols(a, w_refs):
    return jnp.concatenate([_dot(a, r[...]) for r in w_refs], axis=1)


def _col_specs(k, n):
    return [pl.BlockSpec((k, W_COLS), lambda *_, c=c: (0, c), pipeline_mode=pl.Buffered(1))
            for c in range(n // W_COLS)]


def _grouped(kernel, counts):
    def body(*refs, **kw):
        it = iter(refs)
        args = [next(it) if n == 1 else tuple(next(it) for _ in range(n)) for n in counts]
        return kernel(*args, *it, **kw)
    return body


def _sigmoid(x):
    return 0.5 * jnp.tanh(0.5 * x) + 0.5


def _silu(x):
    h = 0.5 * x
    return h * jnp.tanh(h) + h


def _log_sigmoid(x):
    return jnp.minimum(x, 0.0) - jnp.log1p(jnp.exp(-jnp.abs(x)))


def _rows_op(op, x, r):
    m, n = x.shape
    if r.shape[0] in (1, m):
        return op(x, r)
    return op(x.reshape(m // SUBLANES, SUBLANES, n), r[None]).reshape(m, n)


def _mul_rows(x, r):
    return _rows_op(jnp.multiply, x, r)


def _add_rows(x, r):
    return _rows_op(jnp.add, x, r)


def _replicate_rows(when, pairs):
    @pl.when(when)
    def _():
        for src, dst in pairs:
            dst[...] = jnp.broadcast_to(src[...], dst.shape)


def _replicate_mod(mod_ref, mod8_ref, b):
    @pl.when(pl.program_id(1) == 0)
    def _():
        mod8_ref[...] = jnp.broadcast_to(mod_ref[pl.ds(b, 1), :], mod8_ref.shape)


def _rms(x, g):
    return _mul_rows(x * lax.rsqrt(jnp.mean(x * x, axis=-1, keepdims=True) + EPS), g)


def _rms_mod(x, g, mod):
    return _add_rows(_mul_rows(_rms(x, g), 1.0 + mod[:, D_MODEL:2 * D_MODEL]), mod[:, :D_MODEL])


def _layernorm(x, g, b=None):
    mu = jnp.mean(x, axis=-1, keepdims=True)
    xc = x - mu
    var = jnp.mean(xc * xc, axis=-1, keepdims=True)
    y = _mul_rows(xc * lax.rsqrt(var + EPS), g)
    return y if b is None else _add_rows(y, b)


def _const_spec(shape):
    n = len(shape)
    return pl.BlockSpec(shape, lambda *_: (0,) * n, pipeline_mode=pl.Buffered(1))


def _params(*sem):
    return pltpu.CompilerParams(dimension_semantics=sem, vmem_limit_bytes=VMEM_LIMIT)


def _ada_kernel(c_ref, w_ref, b_ref, o_ref):
    o_ref[...] = _dot(_bf(_silu(c_ref[...])), _bf(w_ref[...])) + b_ref[...]


def _ada(c_all, w_ada, b_ada):
    depth, d, d3 = w_ada.shape
    rows = c_all.shape[0]
    tn = D_MODEL
    return pl.pallas_call(
        _ada_kernel,
        grid=(depth, d3 // tn),
        in_specs=[
            pl.BlockSpec((rows, d), lambda i, j: (0, 0)),
            pl.BlockSpec((None, d, tn), lambda i, j: (i, 0, j)),
            pl.BlockSpec((None, 1, tn), lambda i, j: (i, 0, j)),
        ],
        out_specs=pl.BlockSpec((None, rows, tn), lambda i, j: (i, 0, j)),
        out_shape=jax.ShapeDtypeStruct((depth, rows, d3), f32),
        compiler_params=_params("arbitrary", "arbitrary"),
        name="ada",
    )(c_all, w_ada, b_ada.reshape(depth, 1, d3))


def _ml_proj(xm, xc, wq_ref, wk_ref, wv_ref):
    xcb, xmb = _bf(xc), _bf(xm)
    qs, ks, vs = [], [], []
    for h in range(HEADS):
        seg = slice(h * DH, (h + 1) * DH)
        qs.append(_dot(xcb[:, seg], wq_ref[h]))
        ks.append(_dot(xcb[:, seg], wk_ref[h]))
        vs.append(_dot(xmb[:, seg], wv_ref[h]))
    return jnp.concatenate(qs, axis=1), jnp.concatenate(ks, axis=1), jnp.concatenate(vs, axis=1)


def _ml_qkv(xm, xc, wq_ref, wk_ref, wv_ref, wigf_ref, bigf_ref):
    q, k, v = _ml_proj(xm, xc, wq_ref, wk_ref, wv_ref)
    pre = (_dot(_bf(q), wigf_ref[0:INNER, :]) + _dot(_bf(k), wigf_ref[INNER:2 * INNER, :])
           + _dot(_bf(v), wigf_ref[2 * INNER:3 * INNER, :]) + bigf_ref[...])
    lane = lax.broadcasted_iota(jnp.int32, pre.shape, 1)
    gates = jnp.where(lane >= HEADS, _log_sigmoid(pre), pre)
    return q, k, v, gates


def _ml_act(hh, xc, z, skip):
    return _bf((hh + _mul_rows(xc, skip)) * _silu(z))


def _ml_residual(act, x, mod, wout_ref):
    return x + _mul_rows(_dot_cols(act, wout_ref), mod[:, 2 * D_MODEL:])


def _tap(wconv_ref, j):
    return wconv_ref[SUBLANES * j:SUBLANES * (j + 1), :]


def _ml_conv_prompt(xm, wconv_ref, bconv_ref, xbuf_ref, tm):
    xbuf_ref[SUBLANES:SUBLANES + tm, :] = xm
    acc = _add_rows(_mul_rows(xm, _tap(wconv_ref, MCONV - 1)), bconv_ref[...])
    for j in range(MCONV - 1):
        start = SUBLANES - (MCONV - 1) + j
        acc = acc + _mul_rows(xbuf_ref[start:start + tm, :], _tap(wconv_ref, j))
    return acc


def _mlstm_chunk(q, ks, v, gates, lng_ref, C_ref, n_ref, m_ref):
    L = q.shape[0]
    row = lax.broadcasted_iota(jnp.int32, (L, L), 0)
    col = lax.broadcasted_iota(jnp.int32, (L, L), 1)
    causal = row >= col
    tri = jnp.where(causal, 1.0, 0.0).astype(bf16)
    hi = _bf(gates)
    lo = _bf(gates - hi.astype(f32))
    bcum = _dot(tri, hi) + _dot(tri, lo)
    gates_t = gates.T
    bcum_t = bcum.T

    hs = []
    for h in range(HEADS):
        seg = slice(h * DH, (h + 1) * DH)
        ig_col = gates[:, h:h + 1]
        b_col = bcum[:, HEADS + h:HEADS + h + 1]
        ig_row = gates_t[h:h + 1, :]
        b_row = bcum_t[HEADS + h:HEADS + h + 1, :]
        m_prev = m_ref[:, h:h + 1]
        qh, kh, vh = q[:, seg], ks[:, seg], v[:, seg]

        dmat = jnp.where(causal, b_col - b_row + ig_row, NEG)
        inter = b_col + m_prev
        m_t = jnp.maximum(inter, jnp.max(dmat, axis=1, keepdims=True))
        w_intra = jnp.exp(dmat - m_t)
        w_inter = jnp.exp(inter - m_t)
        s = lax.dot_general(qh, kh, (((1,), (1,)), ((), ())), preferred_element_type=f32) * w_intra
        num = _dot(_bf(s), vh) + w_inter * _dot(qh, _bf(C_ref[h]))
        qn = jnp.sum(qh.astype(f32) * n_ref[h:h + 1, :], axis=1, keepdims=True)
        den = jnp.sum(s, axis=1, keepdims=True) + w_inter * qn
        hc = num / jnp.maximum(jnp.abs(den), jnp.exp(-m_t))
        hs.append(_layernorm(hc, lng_ref[:, seg]))

        m_new = m_t[L - 1:L, :]
        b_last = b_col[L - 1:L, :]
        decay = jnp.exp(b_last + m_prev - m_new)
        kw = kh.astype(f32) * jnp.exp(b_last - b_col + ig_col - m_new)
        C_ref[h] = decay * C_ref[h] + lax.dot_general(_bf(kw), vh, (((0,), (0,)), ((), ())),
                                                      preferred_element_type=f32)
        n_ref[h:h + 1, :] = decay * n_ref[h:h + 1, :] + jnp.sum(kw, axis=0, keepdims=True)
        m_ref[:, h:h + 1] = m_new
    return jnp.concatenate(hs, axis=1)


def _ml_prompt_kernel(x_ref, mod_ref, g_ref, win_ref, wconv_ref, bconv_ref, wq_ref, wk_ref, wv_ref, wigf_ref,
                      bigf_ref, lng_ref, skip_ref, wout_ref,
                      o_ref, mconv_ref, C_ref, n_ref, m_ref, xbuf_ref, g8, bconv8, lng8, skip8, mod8, *, tm):
    first = (pl.program_id(0) == 0) & (pl.program_id(1) == 0)
    _replicate_rows(first, [(g_ref, g8), (bconv_ref, bconv8), (lng_ref, lng8), (skip_ref, skip8)])
    _replicate_mod(mod_ref, mod8, pl.program_id(0))
    g_ref, bconv_ref, lng_ref, skip_ref, mod_ref = g8, bconv8, lng8, skip8, mod8

    @pl.when(pl.program_id(1) == 0)
    def _():
        xbuf_ref[0:SUBLANES, :] = jnp.zeros((SUBLANES, INNER), f32)
        C_ref[...] = jnp.zeros_like(C_ref)
        n_ref[...] = jnp.zeros_like(n_ref)
        m_ref[...] = jnp.zeros_like(m_ref)

    x = x_ref[...]
    mod = mod_ref[...]
    xmz = _dot_cols(_bf(_rms_mod(x, g_ref[...], mod)), win_ref)
    xm = xmz[:, :INNER]
    z = xmz[:, INNER:]
    xc = _silu(_ml_conv_prompt(xm, wconv_ref, bconv_ref, xbuf_ref, tm))
    mconv_ref[...] = xbuf_ref[SUBLANES + tm - (MCONV - 1):SUBLANES + tm, :]
    xbuf_ref[0:SUBLANES, :] = xbuf_ref[tm:tm + SUBLANES, :]

    q, k, v, gates = _ml_qkv(xm, xc, wq_ref, wk_ref, wv_ref, wigf_ref, bigf_ref)
    hh = _mlstm_chunk(_bf(q), _bf(k * K_SCALE), _bf(v), gates, lng_ref, C_ref, n_ref, m_ref)
    o_ref[...] = _ml_residual(_ml_act(hh, xc, z, skip_ref[...]), x, mod, wout_ref)


def _ml_prompt(x, mod4, g, w):
    B, T, D = x.shape
    tm = CHUNK
    row = lambda b, t: (b, t, 0)
    win, wout = _col_specs(D, 2 * INNER), _col_specs(INNER, D)
    return pl.pallas_call(
        functools.partial(_grouped(_ml_prompt_kernel, (1, 1, 1, len(win), 1, 1, 1, 1, 1, 1, 1, 1, 1, len(wout))),
                          tm=tm),
        grid=(B, T // tm),
        in_specs=[
            pl.BlockSpec((None, tm, D), row),
            pl.BlockSpec((None, B, 3 * D), lambda b, t: (0, 0, 0)),
            _const_spec((1, D)),
            *win,
            _const_spec((MCONV * SUBLANES, INNER)),
            _const_spec((1, INNER)),
            _const_spec((HEADS, DH, DH)),
            _const_spec((HEADS, DH, DH)),
            _const_spec((HEADS, DH, DH)),
            _const_spec((3 * INNER, LANES)),
            _const_spec((1, LANES)),
            _const_spec((1, INNER)),
            _const_spec((1, INNER)),
            *wout,
        ],
        out_specs=[
            pl.BlockSpec((None, tm, D), row),
            pl.BlockSpec((None, MCONV - 1, INNER), lambda b, t: (b, 0, 0)),
            pl.BlockSpec((None, HEADS, DH, DH), lambda b, t: (b, 0, 0, 0)),
            pl.BlockSpec((None, HEADS, DH), lambda b, t: (b, 0, 0)),
            pl.BlockSpec((None, 1, LANES), lambda b, t: (b, 0, 0)),
        ],
        out_shape=[
            jax.ShapeDtypeStruct((B, T, D), f32),
            jax.ShapeDtypeStruct((B, MCONV - 1, INNER), f32),
            jax.ShapeDtypeStruct((B, HEADS, DH, DH), f32),
            jax.ShapeDtypeStruct((B, HEADS, DH), f32),
            jax.ShapeDtypeStruct((B, 1, LANES), f32),
        ],
        scratch_shapes=[
            pltpu.VMEM((SUBLANES + tm, INNER), f32),
            pltpu.VMEM((SUBLANES, D), f32),
            pltpu.VMEM((SUBLANES, INNER), f32),
            pltpu.VMEM((SUBLANES, INNER), f32),
            pltpu.VMEM((SUBLANES, INNER), f32),
            pltpu.VMEM((SUBLANES, 3 * D), f32),
        ],
        compiler_params=_params("arbitrary", "arbitrary"),
        name="ml_prompt",
    )(x, mod4, g, *[w["ml_w_in"]] * len(win), w["ml_w_conv"], w["ml_b_conv"], w["ml_w_q"], w["ml_w_k"],
      w["ml_w_v"], w["ml_w_igf"], w["ml_b_igf"], w["ml_ln_g"], w["ml_skip"], *[w["ml_w_out"]] * len(wout))


def _ml_front_sample_kernel(x_ref, mod_ref, g_ref, win_ref, wconv_ref, bconv_ref, wq_ref, wk_ref, wv_ref,
                            wigf_ref, bigf_ref, st_ref, n_ref, m_ref,
                            q_ref, kw_ref, v_ref, xc_ref, z_ref, nnew_ref, a1_ref, a2_ref, dec_ref, mnew_ref,
                            mconv_ref):
    h = _rms_mod(x_ref[...], g_ref[...], mod_ref[...])
    xmz = _dot_cols(_bf(h), win_ref)
    xm = xmz[:, :INNER]
    z_ref[...] = xmz[:, INNER:]
    acc = _add_rows(_mul_rows(xm, _tap(wconv_ref, MCONV - 1)), bconv_ref[...])
    for j in range(MCONV - 1):
        acc = acc + _mul_rows(st_ref[j], _tap(wconv_ref, j))
    xc = _silu(acc)
    xc_ref[...] = xc
    mconv_ref[0:MCONV - 2] = st_ref[1:MCONV - 1]
    mconv_ref[MCONV - 2] = xm

    q, k, v, gates = _ml_qkv(xm, xc, wq_ref, wk_ref, wv_ref, wigf_ref, bigf_ref)
    ks = k * K_SCALE
    q_ref[...] = q
    v_ref[...] = v

    ig = gates
    lf = pltpu.roll(gates, LANES - HEADS, axis=1)
    inter = lf + m_ref[...]
    m_new = jnp.maximum(inter, ig)
    w_intra = jnp.exp(ig - m_new)
    w_inter = jnp.exp(inter - m_new)
    lane = lax.broadcasted_iota(jnp.int32, ig.shape, 1)
    qk = jnp.zeros_like(ig)
    qn = jnp.zeros_like(ig)
    for h in range(HEADS):
        seg = slice(h * DH, (h + 1) * DH)
        qk = jnp.where(lane == h, jnp.sum(q[:, seg] * ks[:, seg], axis=1, keepdims=True), qk)
        qn = jnp.where(lane == h, jnp.sum(q[:, seg] * n_ref[:, seg], axis=1, keepdims=True), qn)
        kw = w_intra[:, h:h + 1] * ks[:, seg]
        kw_ref[:, seg] = kw
        nnew_ref[:, seg] = w_inter[:, h:h + 1] * n_ref[:, seg] + kw
    s = qk * w_intra
    den = s + w_inter * qn
    dn = jnp.maximum(jnp.abs(den), jnp.exp(-m_new))
    valid = lane < HEADS
    a1_ref[...] = jnp.where(valid, s / dn, 0.0)
    a2_ref[...] = jnp.where(valid, w_inter / dn, 0.0)
    dec_ref[...] = jnp.where(valid, w_inter, 0.0)
    mnew_ref[...] = jnp.where(valid, m_new, 0.0)


def _ml_front_sample(x, mod, g, w, st, n, m):
    R, D = x.shape
    full = lambda shape: pl.BlockSpec(shape, lambda i: (0,) * len(shape))
    act = jax.ShapeDtypeStruct((R, INNER), f32)
    small = jax.ShapeDtypeStruct((R, LANES), f32)
    win = _col_specs(D, 2 * INNER)
    return pl.pallas_call(
        _grouped(_ml_front_sample_kernel, (1, 1, 1, len(win))),
        grid=(1,),
        in_specs=[
            full((R, D)), full((R, 3 * D)), full((1, D)), *win,
            full((MCONV * SUBLANES, INNER)),
            full((1, INNER)), full((HEADS, DH, DH)), full((HEADS, DH, DH)), full((HEADS, DH, DH)),
            full((3 * INNER, LANES)), full((1, LANES)),
            full((MCONV - 1, R, INNER)), full((R, INNER)), full((R, LANES)),
        ],
        out_specs=[full((R, INNER))] * 6 + [full((R, LANES))] * 4 + [full((MCONV - 1, R, INNER))],
        out_shape=[act] * 6 + [small] * 4 + [jax.ShapeDtypeStruct((MCONV - 1, R, INNER), f32)],
        compiler_params=_params("arbitrary"),
        name="ml_front_sample",
    )(x, mod, g, *[w["ml_w_in"]] * len(win), w["ml_w_conv"], w["ml_b_conv"], w["ml_w_q"], w["ml_w_k"],
      w["ml_w_v"], w["ml_w_igf"], w["ml_b_igf"], st, n, m)


def _ml_back_sample_kernel(a1_ref, a2_ref, v_ref, qc_ref, lng_ref, xc_ref, z_ref, x_ref, mod_ref, skip_ref,
                           wout_ref, o_ref):
    hs = []
    for h in range(HEADS):
        seg = slice(h * DH, (h + 1) * DH)
        hc = a1_ref[:, h:h + 1] * v_ref[:, seg] + a2_ref[:, h:h + 1] * qc_ref[:, seg]
        hs.append(_layernorm(hc, lng_ref[:, seg]))
    hh = jnp.concatenate(hs, axis=1)
    act = _ml_act(hh, xc_ref[...], z_ref[...], skip_ref[...])
    o_ref[...] = _ml_residual(act, x_ref[...], mod_ref[...], wout_ref)


def _ml_back_sample(a1, a2, v, qc, xc, z, x, mod, w):
    R, D = x.shape
    full = lambda shape: pl.BlockSpec(shape, lambda i: (0,) * len(shape))
    wout = _col_specs(INNER, D)
    return pl.pallas_call(
        _grouped(_ml_back_sample_kernel, (1,) * 10 + (len(wout),)),
        grid=(1,),
        in_specs=[full((R, LANES)), full((R, LANES)), full((R, INNER)), full((R, INNER)), full((1, INNER)),
                  full((R, INNER)), full((R, INNER)), full((R, D)), full((R, 3 * D)), full((1, INNER)),
                  *wout],
        out_specs=full((R, D)),
        out_shape=jax.ShapeDtypeStruct((R, D), f32),
        compiler_params=_params("arbitrary"),
        name="ml_back_sample",
    )(a1, a2, v, qc, w["ml_ln_g"], xc, z, x, mod, w["ml_skip"], *[w["ml_w_out"]] * len(wout))


def _cf_in(x, mod, g_ref, win_ref, bin_ref):
    h = _rms_mod(x, g_ref[...], mod)
    agz = _add_rows(_dot_cols(_bf(h), win_ref), bin_ref[...])
    u = agz[:, :INNER] * _sigmoid(agz[:, INNER:2 * INNER])
    return u, agz[:, 2 * INNER:]


def _cf_out(y, z, x, mod, lng_ref, lnb_ref, wout_ref, fg_ref):
    yn = _layernorm(y, lng_ref[...], lnb_ref[...])
    out = _dot_cols(_bf(_silu(yn) * _silu(z)), wout_ref)
    return _rms(x + _mul_rows(out, mod[:, 2 * D_MODEL:]), fg_ref[...])


def _channel_block(cb):
    start = cb * CONV_COLS
    return pl.ds(start if isinstance(cb, int) else pl.multiple_of(start, CONV_COLS), CONV_COLS)


def _dwconv_block(ubuf_ref, sh_ref, wb_ref, bdw_ref, y_ref, tm, cols):
    base = CCONV_HIST - (CCONV - 1)
    max_row_off = (base + CCONV - 2) // SUBLANES * SUBLANES
    for r in range(1, SUBLANES):
        sh_ref[r - 1] = ubuf_ref[r:r + tm + max_row_off, cols]
    b = bdw_ref[:, cols]
    for rc in range(tm // CONV_ROWS):
        acc = jnp.broadcast_to(b, (CONV_ROWS, CONV_COLS))
        for j in range(CCONV):
            a, r = divmod(base + j, SUBLANES)
            rows = pl.ds(SUBLANES * a + CONV_ROWS * rc, CONV_ROWS)
            src = ubuf_ref[rows, cols] if r == 0 else sh_ref[r - 1, rows, :]
            acc = acc + _mul_rows(src, wb_ref[SUBLANES * j:SUBLANES * (j + 1), cols])
        y_ref[pl.ds(CONV_ROWS * rc, CONV_ROWS), cols] = acc


def _memory_step(step, dec_ref, c_ref, q_ref, kw_ref, v_ref, cout_ref):
    head = (step // 2) % HEADS
    first = (step // (2 * HEADS)) * STEP_BATCH
    qt = q_ref[...].T
    kwt = kw_ref[...].T
    parts = []
    for i in range(STEP_BATCH):
        decay = dec_ref[(first + i) * HEADS + head]
        c = c_ref[i]
        parts.append(jnp.sum(c * qt[:, i:i + 1], axis=0, keepdims=True))
        cout_ref[i] = decay * c + kwt[:, i:i + 1] * v_ref[i:i + 1, :]
    return jnp.concatenate(parts, axis=0)


def _cf_prompt_kernel(x_ref, mod_ref, g_ref, win_ref, bin_ref, wdw_ref, bdw_ref, lng_ref, lnb_ref, wout_ref,
                      fg_ref, dec_ref, cs_ref, qs_ref, kws_ref, vs_ref,
                      o_ref, cconv_ref, csout_ref, qc_ref, ubuf_ref, y_ref, sh_ref, wb_ref,
                      g8, bin8, lng8, lnb8, fg8, mod8, *, tm):
    step = pl.program_id(0) * pl.num_programs(1) + pl.program_id(1)
    nblk = INNER // CONV_COLS

    _replicate_rows(step == 0, [(g_ref, g8), (bin_ref, bin8), (lng_ref, lng8), (lnb_ref, lnb8), (fg_ref, fg8)])
    _replicate_mod(mod_ref, mod8, pl.program_id(0))
    g_ref, bin_ref, lng_ref, lnb_ref, fg_ref, mod_ref = g8, bin8, lng8, lnb8, fg8, mod8

    @pl.when(step == 0)
    def _():
        for j in range(CCONV):
            wb_ref[SUBLANES * j:SUBLANES * (j + 1), :] = jnp.broadcast_to(wdw_ref[j:j + 1, :], (SUBLANES, INNER))

    @pl.when(pl.program_id(1) == 0)
    def _():
        ubuf_ref[0:CCONV_HIST, :] = jnp.zeros((CCONV_HIST, INNER), f32)

    def conv_blocks(lo, hi):
        def block_body(cb, carry):
            _dwconv_block(ubuf_ref, sh_ref, wb_ref, bdw_ref, y_ref, tm, _channel_block(cb))
            return carry

        lax.fori_loop(lo, hi, block_body, 0)

    n_ag = 2 * INNER // W_COLS

    def normed():
        return _bf(_rms_mod(x_ref[...], g_ref[...], mod_ref[...]))

    @pl.when(step % 2 == 0)
    def _():
        qc_ref[...] = _memory_step(step, dec_ref, cs_ref, qs_ref, kws_ref, vs_ref, csout_ref)
        ag = _add_rows(_dot_cols(normed(), win_ref[:n_ag]), bin_ref[:, :2 * INNER])
        ubuf_ref[CCONV_HIST:CCONV_HIST + tm, :] = ag[:, :INNER] * _sigmoid(ag[:, INNER:])
        conv_blocks(0, nblk // 2)

    @pl.when(step % 2 == 1)
    def _():
        qc_ref[...] += _memory_step(step, dec_ref, cs_ref, qs_ref, kws_ref, vs_ref, csout_ref)
        z = _add_rows(_dot_cols(normed(), win_ref[n_ag:]), bin_ref[:, 2 * INNER:])
        conv_blocks(nblk // 2, nblk)
        cconv_ref[...] = ubuf_ref[CCONV_HIST + tm - (CCONV - 1):CCONV_HIST + tm, :]
        ubuf_ref[0:CCONV_HIST, :] = ubuf_ref[tm:tm + CCONV_HIST, :]
        o_ref[...] = _cf_out(y_ref[...], z, x_ref[...], mod_ref[...], lng_ref, lnb_ref, wout_ref, fg_ref)


def _cf_prompt(x, mod4, g, w, final_g, dec_flat, C_s, q_s, kw_s, v_s):
    B, T, D = x.shape
    R = C_s.shape[0]
    tm = ROW_TILE
    steps_t = 2 * (T // tm)
    assert B * steps_t * STEP_BATCH == R * HEADS * 2, "one (sequence group, head, key half) per grid step"
    row = lambda b, t: (b, t // 2, 0)

    def ghk(b, t):
        s = b * steps_t + t
        return s // (2 * HEADS), (s // 2) % HEADS, s % 2

    c_spec = pl.BlockSpec((STEP_BATCH, None, DH // 2, DH), lambda b, t: (*ghk(b, t), 0))
    half_spec = pl.BlockSpec((STEP_BATCH, DH // 2), lambda b, t: (ghk(b, t)[0], 2 * ghk(b, t)[1] + ghk(b, t)[2]))
    head_spec = pl.BlockSpec((STEP_BATCH, DH), lambda b, t: ghk(b, t)[:2])
    win, wout = _col_specs(D, 3 * INNER), _col_specs(INNER, D)
    return pl.pallas_call(
        functools.partial(_grouped(_cf_prompt_kernel, (1, 1, 1, len(win), 1, 1, 1, 1, 1, len(wout))), tm=tm),
        grid=(B, steps_t),
        in_specs=[
            pl.BlockSpec((None, tm, D), row),
            pl.BlockSpec((None, B, 3 * D), lambda b, t: (1, 0, 0)),
            _const_spec((1, D)),
            *win,
            _const_spec((1, 3 * INNER)),
            _const_spec((CCONV, INNER)),
            _const_spec((1, INNER)),
            _const_spec((1, INNER)),
            _const_spec((1, INNER)),
            *wout,
            _const_spec((1, D)),
            pl.BlockSpec(memory_space=pltpu.SMEM),
            c_spec, half_spec, half_spec, head_spec,
        ],
        out_specs=[
            pl.BlockSpec((None, tm, D), row),
            pl.BlockSpec((None, CCONV - 1, INNER), lambda b, t: (b, 0, 0)),
            c_spec,
            head_spec,
        ],
        out_shape=[
            jax.ShapeDtypeStruct((B, T, D), f32),
            jax.ShapeDtypeStruct((B, CCONV - 1, INNER), f32),
            jax.ShapeDtypeStruct(C_s.shape, f32),
            jax.ShapeDtypeStruct((R, INNER), f32),
        ],
        scratch_shapes=[
            pltpu.VMEM((CCONV_HIST + tm, INNER), f32),
            pltpu.VMEM((tm, INNER), f32),
            pltpu.VMEM((SUBLANES - 1, tm + CCONV_HIST - SUBLANES, CONV_COLS), f32),
            pltpu.VMEM((SUBLANES * CCONV, INNER), f32),
            pltpu.VMEM((SUBLANES, D), f32),
            pltpu.VMEM((SUBLANES, 3 * INNER), f32),
            pltpu.VMEM((SUBLANES, INNER), f32),
            pltpu.VMEM((SUBLANES, INNER), f32),
            pltpu.VMEM((SUBLANES, D), f32),
            pltpu.VMEM((SUBLANES, 3 * D), f32),
        ],
        compiler_params=_params("arbitrary", "arbitrary"),
        name="cf_prompt",
    )(x, mod4, g, *[w["cf_w_in"]] * len(win), w["cf_b_in"], w["cf_w_dw"], w["cf_b_dw"], w["cf_ln_g"],
      w["cf_ln_b"], *[w["cf_w_out"]] * len(wout), final_g, dec_flat, C_s, q_s, kw_s, v_s)


def _cf_sample_kernel(x_ref, mod_ref, g_ref, win_ref, bin_ref, wdw_ref, bdw_ref, lng_ref, lnb_ref, wout_ref,
                      fg_ref, st_ref, o_ref, stout_ref, u_ref, z_ref, y_ref, *, tb):
    i = pl.program_id(0)

    @pl.when(i == 0)
    def _():
        u, z = _cf_in(x_ref[...], mod_ref[...], g_ref, win_ref, bin_ref)
        u_ref[...] = u
        z_ref[...] = z

    rows = pl.ds(pl.multiple_of(i * tb, tb), tb)
    u = u_ref[rows, :]
    acc = bdw_ref[...] + wdw_ref[CCONV - 1:CCONV, :] * u
    for j in range(CCONV - 1):
        acc = acc + wdw_ref[j:j + 1, :] * st_ref[j]
    y_ref[rows, :] = acc
    stout_ref[0:CCONV - 2] = st_ref[1:CCONV - 1]
    stout_ref[CCONV - 2] = u

    @pl.when(i == pl.num_programs(0) - 1)
    def _():
        o_ref[...] = _cf_out(y_ref[...], z_ref[...], x_ref[...], mod_ref[...], lng_ref, lnb_ref, wout_ref, fg_ref)


def _cf_sample(x, mod, g, w, final_g, st):
    R, D = x.shape
    tb = SAMPLE_TILE
    state = pl.BlockSpec((CCONV - 1, tb, INNER), lambda i: (0, i, 0))
    win, wout = _col_specs(D, 3 * INNER), _col_specs(INNER, D)
    return pl.pallas_call(
        functools.partial(_grouped(_cf_sample_kernel, (1, 1, 1, len(win), 1, 1, 1, 1, 1, len(wout))), tb=tb),
        grid=(R // tb,),
        in_specs=[
            _const_spec((R, D)),
            _const_spec((R, 3 * D)),
            _const_spec((1, D)),
            *win,
            _const_spec((1, 3 * INNER)),
            _const_spec((CCONV, INNER)),
            _const_spec((1, INNER)),
            _const_spec((1, INNER)),
            _const_spec((1, INNER)),
            *wout,
            _const_spec((1, D)),
            state,
        ],
        out_specs=[pl.BlockSpec((R, D), lambda i: (0, 0)), state],
        out_shape=[jax.ShapeDtypeStruct((R, D), f32), jax.ShapeDtypeStruct((CCONV - 1, R, INNER), f32)],
        scratch_shapes=[pltpu.VMEM((R, INNER), f32)] * 3,
        compiler_params=_params("arbitrary"),
        name="cf_sample",
    )(x, mod, g, *[w["cf_w_in"]] * len(win), w["cf_b_in"], w["cf_w_dw"], w["cf_b_dw"], w["cf_ln_g"],
      w["cf_ln_b"], *[w["cf_w_out"]] * len(wout), final_g, st)


def kernel(x_prompt, x_sample, c_prompt, c_sample, state_mlstm_C, state_mlstm_n, state_mlstm_m, state_mlstm_conv, state_conf_conv, norm_g, w_ada, b_ada, ml_w_in, ml_w_conv, ml_b_conv, ml_w_q, ml_w_k, ml_w_v, ml_w_ig, ml_b_ig, ml_w_fg, ml_b_fg, ml_ln_g, ml_skip, ml_w_out, cf_w_in, cf_b_in, cf_w_dw, cf_b_dw, cf_ln_g, cf_ln_b, cf_w_out, final_g):
    B = x_prompt.shape[0]
    R = x_sample.shape[0]
    D = D_MODEL

    gate_pad = LANES - 2 * HEADS
    w = {
        "ml_w_in": _bf(ml_w_in[0]),
        "ml_w_conv": jnp.repeat(ml_w_conv[0], SUBLANES, axis=0),
        "ml_b_conv": ml_b_conv,
        "ml_w_q": _bf(ml_w_q[0]),
        "ml_w_k": _bf(ml_w_k[0]),
        "ml_w_v": _bf(ml_w_v[0]),
        "ml_w_igf": _bf(jnp.pad(jnp.concatenate([ml_w_ig[0], ml_w_fg[0]], axis=1), ((0, 0), (0, gate_pad)))),
        "ml_b_igf": jnp.pad(jnp.concatenate([ml_b_ig, ml_b_fg], axis=1), ((0, 0), (0, gate_pad))),        "ml_ln_g": ml_ln_g.reshape(1, INNER),
        "ml_skip": ml_skip,
        "ml_w_out": _bf(ml_w_out[0]),
        "cf_w_in": _bf(cf_w_in[0]),
        "cf_b_in": cf_b_in,
        "cf_w_dw": cf_w_dw[0],
        "cf_b_dw": cf_b_dw,
        "cf_ln_g": cf_ln_g,
        "cf_ln_b": cf_ln_b,
        "cf_w_out": _bf(cf_w_out[0]),
    }
    final_g2 = final_g.reshape(1, D)

    mod = _ada(jnp.concatenate([c_prompt, c_sample], axis=0), w_ada, b_ada)
    mod_s = mod[:, B:]

    tap_major = lambda a: jnp.transpose(a, (1, 0, 2))
    xs = x_sample.reshape(R, D)
    m_pad = jnp.pad(state_mlstm_m[0], ((0, 0), (0, LANES - HEADS)))
    (q_s, kw_s, v_s, xc_s, z_s, n_s, a1, a2, dec, m_s, mconv_s) = _ml_front_sample(
        xs, mod_s[0], norm_g[0:1], w, tap_major(state_mlstm_conv[0]), state_mlstm_n[0].reshape(R, INNER), m_pad)

    x1, mconv_p, C_p, n_p, m_p = _ml_prompt(x_prompt, mod, norm_g[0:1], w)
    y_prompt, cconv_p, C_s, qc = _cf_prompt(
        x1, mod, norm_g[1:2], w, final_g2, dec[:, :HEADS].reshape(R * HEADS), state_mlstm_C[0], q_s, kw_s, v_s)

    x1_s = _ml_back_sample(a1, a2, v_s, qc, xc_s, z_s, xs, mod_s[0], w)
    y_sample, cconv_s = _cf_sample(x1_s, mod_s[1], norm_g[1:2], w, final_g2, tap_major(state_conf_conv[0]))

    return (
        y_prompt,
        y_sample.reshape(R, 1, D),
        C_p[None],
        C_s[None],
        n_p[None],
        n_s.reshape(1, R, HEADS, DH),
        m_p[:, 0, :HEADS][None],
        m_s[:, :HEADS][None],
        mconv_p[None],
        tap_major(mconv_s)[None],
        cconv_p[None],
        tap_major(cconv_s)[None],
    )
```

```python
import functools

import jax
import jax.numpy as jnp
from jax import lax
from jax.experimental import pallas as pl
from jax.experimental.pallas import tpu as pltpu

D_MODEL = 1024
INNER = 2 * D_MODEL
HEADS = 4
DH = INNER // HEADS
MCONV = 4
CCONV = 31
EPS = 1e-6
NEG = -1e30
K_SCALE = DH ** -0.5

LANES = 128
SUBLANES = 8
VMEM_LIMIT = 56 * 2 ** 20

ROW_TILE = 256
CHUNK = 256
STEP_BATCH = 8
CONV_ROWS = 64
CONV_COLS = 128
CCONV_HIST = 32
SAMPLE_TILE = 16
W_COLS = 512

f32 = jnp.float32
bf16 = jnp.bfloat16


def _bf(x):
    return x.astype(bf16)


def _dot(a, b):
    return jnp.dot(a, b, preferred_element_type=f32)


def _dot_cols(a, w_refs):
    return jnp.concatenate([_dot(a, r[...]) for r in w_refs], axis=1)


def _col_specs(k, n):
    return [pl.BlockSpec((k, W_COLS), lambda *_, c=c: (0, c), pipeline_mode=pl.Buffered(1))
            for c in range(n // W_COLS)]


def _grouped(kernel, counts):
    def body(*refs, **kw):
        it = iter(refs)
        args = [next(it) if n == 1 else tuple(next(it) for _ in range(n)) for n in counts]
        return kernel(*args, *it, **kw)
    return body


def _sigmoid(x):
    return 0.5 * jnp.tanh(0.5 * x) + 0.5


def _silu(x):
    h = 0.5 * x
    return h * jnp.tanh(h) + h


def _log_sigmoid(x):
    return jnp.minimum(x, 0.0) - jnp.log1p(jnp.exp(-jnp.abs(x)))


def _rows_op(op, x, r):
    m, n = x.shape
    if r.shape[0] in (1, m):
        return op(x, r)
    return op(x.reshape(m // SUBLANES, SUBLANES, n), r[None]).reshape(m, n)


def _mul_rows(x, r):
    return _rows_op(jnp.multiply, x, r)


def _add_rows(x, r):
    return _rows_op(jnp.add, x, r)


def _replicate_rows(when, pairs):
    @pl.when(when)
    def _():
        for src, dst in pairs:
            dst[...] = jnp.broadcast_to(src[...], dst.shape)


def _replicate_mod(mod_ref, mod8_ref, b):
    @pl.when(pl.program_id(1) == 0)
    def _():
        mod8_ref[...] = jnp.broadcast_to(mod_ref[pl.ds(b, 1), :], mod8_ref.shape)


def _rms(x, g):
    return _mul_rows(x * lax.rsqrt(jnp.mean(x * x, axis=-1, keepdims=True) + EPS), g)


def _rms_mod(x, g, mod):
    return _add_rows(_mul_rows(_rms(x, g), 1.0 + mod[:, D_MODEL:2 * D_MODEL]), mod[:, :D_MODEL])


def _layernorm(x, g, b=None):
    mu = jnp.mean(x, axis=-1, keepdims=True)
    xc = x - mu
    var = jnp.mean(xc * xc, axis=-1, keepdims=True)
    y = _mul_rows(xc * lax.rsqrt(var + EPS), g)
    return y if b is None else _add_rows(y, b)


def _const_spec(shape):
    n = len(shape)
    return pl.BlockSpec(shape, lambda *_: (0,) * n, pipeline_mode=pl.Buffered(1))


def _params(*sem):
    return pltpu.CompilerParams(dimension_semantics=sem, vmem_limit_bytes=VMEM_LIMIT)


def _ada_kernel(c_ref, w_ref, b_ref, o_ref):
    o_ref[...] = _dot(_bf(_silu(c_ref[...])), _bf(w_ref[...])) + b_ref[...]


def _ada(c_all, w_ada, b_ada):
    depth, d, d3 = w_ada.shape
    rows = c_all.shape[0]
    tn = D_MODEL
    return pl.pallas_call(
        _ada_kernel,
        grid=(depth, d3 // tn),
        in_specs=[
            pl.BlockSpec((rows, d), lambda i, j: (0, 0)),
            pl.BlockSpec((None, d, tn), lambda i, j: (i, 0, j)),
            pl.BlockSpec((None, 1, tn), lambda i, j: (i, 0, j)),
        ],
        out_specs=pl.BlockSpec((None, rows, tn), lambda i, j: (i, 0, j)),
        out_shape=jax.ShapeDtypeStruct((depth, rows, d3), f32),
        compiler_params=_params("arbitrary", "arbitrary"),
        name="ada",
    )(c_all, w_ada, b_ada.reshape(depth, 1, d3))


def _ml_proj(xm, xc, wq_ref, wk_ref, wv_ref):
    xcb, xmb = _bf(xc), _bf(xm)
    qs, ks, vs = [], [], []
    for h in range(HEADS):
        seg = slice(h * DH, (h + 1) * DH)
        qs.append(_dot(xcb[:, seg], wq_ref[h]))
        ks.append(_dot(xcb[:, seg], wk_ref[h]))
        vs.append(_dot(xmb[:, seg], wv_ref[h]))
    return jnp.concatenate(qs, axis=1), jnp.concatenate(ks, axis=1), jnp.concatenate(vs, axis=1)


def _ml_qkv(xm, xc, wq_ref, wk_ref, wv_ref, wigf_ref, bigf_ref):
    q, k, v = _ml_proj(xm, xc, wq_ref, wk_ref, wv_ref)
    pre = (_dot(_bf(q), wigf_ref[0:INNER, :]) + _dot(_bf(k), wigf_ref[INNER:2 * INNER, :])
           + _dot(_bf(v), wigf_ref[2 * INNER:3 * INNER, :]) + bigf_ref[...])
    lane = lax.broadcasted_iota(jnp.int32, pre.shape, 1)
    gates = jnp.where(lane >= HEADS, _log_sigmoid(pre), pre)
    return q, k, v, gates


def _ml_act(hh, xc, z, skip):
    return _bf((hh + _mul_rows(xc, skip)) * _silu(z))


def _ml_residual(act, x, mod, wout_ref):
    return x + _mul_rows(_dot_cols(act, wout_ref), mod[:, 2 * D_MODEL:])


def _tap(wconv_ref, j):
    return wconv_ref[SUBLANES * j:SUBLANES * (j + 1), :]


def _ml_conv_prompt(xm, wconv_ref, bconv_ref, xbuf_ref, tm):
    xbuf_ref[SUBLANES:SUBLANES + tm, :] = xm
    acc = _add_rows(_mul_rows(xm, _tap(wconv_ref, MCONV - 1)), bconv_ref[...])
    for j in range(MCONV - 1):
        start = SUBLANES - (MCONV - 1) + j
        acc = acc + _mul_rows(xbuf_ref[start:start + tm, :], _tap(wconv_ref, j))
    return acc


def _mlstm_chunk(q, ks, v, gates, lng_ref, C_ref, n_ref, m_ref):
    L = q.shape[0]
    row = lax.broadcasted_iota(jnp.int32, (L, L), 0)
    col = lax.broadcasted_iota(jnp.int32, (L, L), 1)
    causal = row >= col
    tri = jnp.where(causal, 1.0, 0.0).astype(bf16)
    hi = _bf(gates)
    lo = _bf(gates - hi.astype(f32))
    bcum = _dot(tri, hi) + _dot(tri, lo)
    gates_t = gates.T
    bcum_t = bcum.T

    hs = []
    for h in range(HEADS):
        seg = slice(h * DH, (h + 1) * DH)
        ig_col = gates[:, h:h + 1]
        b_col = bcum[:, HEADS + h:HEADS + h + 1]
        ig_row = gates_t[h:h + 1, :]
        b_row = bcum_t[HEADS + h:HEADS + h + 1, :]
        m_prev = m_ref[:, h:h + 1]
        qh, kh, vh = q[:, seg], ks[:, seg], v[:, seg]

        dmat = jnp.where(causal, b_col - b_row + ig_row, NEG)
        inter = b_col + m_prev
        m_t = jnp.maximum(inter, jnp.max(dmat, axis=1, keepdims=True))
        w_intra = jnp.exp(dmat - m_t)
        w_inter = jnp.exp(inter - m_t)
        s = lax.dot_general(qh, kh, (((1,), (1,)), ((), ())), preferred_element_type=f32) * w_intra
        num = _dot(_bf(s), vh) + w_inter * _dot(qh, _bf(C_ref[h]))
        qn = jnp.sum(qh.astype(f32) * n_ref[h:h + 1, :], axis=1, keepdims=True)
        den = jnp.sum(s, axis=1, keepdims=True) + w_inter * qn
        hc = num / jnp.maximum(jnp.abs(den), jnp.exp(-m_t))
        hs.append(_layernorm(hc, lng_ref[:, seg]))

        m_new = m_t[L - 1:L, :]
        b_last = b_col[L - 1:L, :]
        decay = jnp.exp(b_last + m_prev - m_new)
        kw = kh.astype(f32) * jnp.exp(b_last - b_col + ig_col - m_new)
        C_ref[h] = decay * C_ref[h] + lax.dot_general(_bf(kw), vh, (((0,), (0,)), ((), ())),
                                                      preferred_element_type=f32)
        n_ref[h:h + 1, :] = decay * n_ref[h:h + 1, :] + jnp.sum(kw, axis=0, keepdims=True)
        m_ref[:, h:h + 1] = m_new
    return jnp.concatenate(hs, axis=1)


def _ml_prompt_kernel(x_ref, mod_ref, g_ref, win_ref, wconv_ref, bconv_ref, wq_ref, wk_ref, wv_ref, wigf_ref,
                      bigf_ref, lng_ref, skip_ref, wout_ref,
                      o_ref, mconv_ref, C_ref, n_ref, m_ref, xbuf_ref, g8, bconv8, lng8, skip8, mod8, *, tm):
    first = (pl.program_id(0) == 0) & (pl.program_id(1) == 0)
    _replicate_rows(first, [(g_ref, g8), (bconv_ref, bconv8), (lng_ref, lng8), (skip_ref, skip8)])
    _replicate_mod(mod_ref, mod8, pl.program_id(0))
    g_ref, bconv_ref, lng_ref, skip_ref, mod_ref = g8, bconv8, lng8, skip8, mod8

    @pl.when(pl.program_id(1) == 0)
    def _():
        xbuf_ref[0:SUBLANES, :] = jnp.zeros((SUBLANES, INNER), f32)
        C_ref[...] = jnp.zeros_like(C_ref)
        n_ref[...] = jnp.zeros_like(n_ref)
        m_ref[...] = jnp.zeros_like(m_ref)

    x = x_ref[...]
    mod = mod_ref[...]
    xmz = _dot_cols(_bf(_rms_mod(x, g_ref[...], mod)), win_ref)
    xm = xmz[:, :INNER]
    z = xmz[:, INNER:]
    xc = _silu(_ml_conv_prompt(xm, wconv_ref, bconv_ref, xbuf_ref, tm))
    mconv_ref[...] = xbuf_ref[SUBLANES + tm - (MCONV - 1):SUBLANES + tm, :]
    xbuf_ref[0:SUBLANES, :] = xbuf_ref[tm:tm + SUBLANES, :]

    q, k, v, gates = _ml_qkv(xm, xc, wq_ref, wk_ref, wv_ref, wigf_ref, bigf_ref)
    hh = _mlstm_chunk(_bf(q), _bf(k * K_SCALE), _bf(v), gates, lng_ref, C_ref, n_ref, m_ref)
    o_ref[...] = _ml_residual(_ml_act(hh, xc, z, skip_ref[...]), x, mod, wout_ref)


def _ml_prompt(x, mod4, g, w):
    B, T, D = x.shape
    tm = CHUNK
    row = lambda b, t: (b, t, 0)
    win, wout = _col_specs(D, 2 * INNER), _col_specs(INNER, D)
    return pl.pallas_call(
        functools.partial(_grouped(_ml_prompt_kernel, (1, 1, 1, len(win), 1, 1, 1, 1, 1, 1, 1, 1, 1, len(wout))),
                          tm=tm),
        grid=(B, T // tm),
        in_specs=[
            pl.BlockSpec((None, tm, D), row),
            pl.BlockSpec((None, B, 3 * D), lambda b, t: (0, mod4.shape[1] // B - 1, 0)),
            _const_spec((1, D)),
            *win,
            _const_spec((MCONV * SUBLANES, INNER)),
            _const_spec((1, INNER)),
            _const_spec((HEADS, DH, DH)),
            _const_spec((HEADS, DH, DH)),
            _const_spec((HEADS, DH, DH)),
            _const_spec((3 * INNER, LANES)),
            _const_spec((1, LANES)),
            _const_spec((1, INNER)),
            _const_spec((1, INNER)),
            *wout,
        ],
        out_specs=[
            pl.BlockSpec((None, tm, D), row),
            pl.BlockSpec((None, MCONV - 1, INNER), lambda b, t: (b, 0, 0)),
            pl.BlockSpec((None, HEADS, DH, DH), lambda b, t: (b, 0, 0, 0)),
            pl.BlockSpec((None, HEADS, DH), lambda b, t: (b, 0, 0)),
            pl.BlockSpec((None, 1, LANES), lambda b, t: (b, 0, 0)),
        ],
        out_shape=[
            jax.ShapeDtypeStruct((B, T, D), f32),
            jax.ShapeDtypeStruct((B, MCONV - 1, INNER), f32),
            jax.ShapeDtypeStruct((B, HEADS, DH, DH), f32),
            jax.ShapeDtypeStruct((B, HEADS, DH), f32),
            jax.ShapeDtypeStruct((B, 1, LANES), f32),
        ],
        scratch_shapes=[
            pltpu.VMEM((SUBLANES + tm, INNER), f32),
            pltpu.VMEM((SUBLANES, D), f32),
            pltpu.VMEM((SUBLANES, INNER), f32),
            pltpu.VMEM((SUBLANES, INNER), f32),
            pltpu.VMEM((SUBLANES, INNER), f32),
            pltpu.VMEM((SUBLANES, 3 * D), f32),
        ],
        compiler_params=_params("arbitrary", "arbitrary"),
        name="ml_prompt",
    )(x, mod4, g, *[w["ml_w_in"]] * len(win), w["ml_w_conv"], w["ml_b_conv"], w["ml_w_q"], w["ml_w_k"],
      w["ml_w_v"], w["ml_w_igf"], w["ml_b_igf"], w["ml_ln_g"], w["ml_skip"], *[w["ml_w_out"]] * len(wout))


def _ml_front_sample_kernel(x_ref, mod_ref, g_ref, win_ref, wconv_ref, bconv_ref, wq_ref, wk_ref, wv_ref,
                            wigf_ref, bigf_ref, st_ref, n_ref, m_ref,
                            q_ref, kw_ref, v_ref, xc_ref, z_ref, nnew_ref, a1_ref, a2_ref, dec_ref, mnew_ref,
                            mconv_ref):
    h = _rms_mod(x_ref[:, 0, :], g_ref[...], mod_ref[...])
    xmz = _dot_cols(_bf(h), win_ref)
    xm = xmz[:, :INNER]
    z_ref[...] = xmz[:, INNER:]
    acc = _add_rows(_mul_rows(xm, _tap(wconv_ref, MCONV - 1)), bconv_ref[...])
    for j in range(MCONV - 1):
        acc = acc + _mul_rows(st_ref[j], _tap(wconv_ref, j))
    xc = _silu(acc)
    xc_ref[...] = xc
    mconv_ref[0:MCONV - 2] = st_ref[1:MCONV - 1]
    mconv_ref[MCONV - 2] = xm

    q, k, v, gates = _ml_qkv(xm, xc, wq_ref, wk_ref, wv_ref, wigf_ref, bigf_ref)
    ks = k * K_SCALE
    q_ref[...] = q
    v_ref[...] = v

    ig = gates
    lf = pltpu.roll(gates, LANES - HEADS, axis=1)
    inter = lf + m_ref[...]
    m_new = jnp.maximum(inter, ig)
    w_intra = jnp.exp(ig - m_new)
    w_inter = jnp.exp(inter - m_new)
    lane = lax.broadcasted_iota(jnp.int32, ig.shape, 1)
    qk = jnp.zeros_like(ig)
    qn = jnp.zeros_like(ig)
    for h in range(HEADS):
        seg = slice(h * DH, (h + 1) * DH)
        qk = jnp.where(lane == h, jnp.sum(q[:, seg] * ks[:, seg], axis=1, keepdims=True), qk)
        n_h = n_ref[:, h, :]
        qn = jnp.where(lane == h, jnp.sum(q[:, seg] * n_h, axis=1, keepdims=True), qn)
        kw = w_intra[:, h:h + 1] * ks[:, seg]
        kw_ref[:, seg] = kw
        nnew_ref[:, h, :] = w_inter[:, h:h + 1] * n_h + kw
    s = qk * w_intra
    den = s + w_inter * qn
    dn = jnp.maximum(jnp.abs(den), jnp.exp(-m_new))
    valid = lane < HEADS
    a1_ref[...] = jnp.where(valid, s / dn, 0.0)
    a2_ref[...] = jnp.where(valid, w_inter / dn, 0.0)
    dec_ref[...] = jnp.where(valid, w_inter, 0.0)
    mnew_ref[...] = jnp.where(valid, m_new, 0.0)


def _ml_front_sample(x, mod, g, w, st, n, m):
    R, _, D = x.shape
    full = lambda shape: pl.BlockSpec(shape, lambda i: (0,) * len(shape))
    act = jax.ShapeDtypeStruct((R, INNER), f32)
    small = jax.ShapeDtypeStruct((R, LANES), f32)
    win = _col_specs(D, 2 * INNER)
    return pl.pallas_call(
        _grouped(_ml_front_sample_kernel, (1, 1, 1, len(win))),
        grid=(1,),
        in_specs=[
            full((R, 1, D)), pl.BlockSpec((None, R, 3 * D), lambda i: (0, 0, 0)), full((1, D)), *win,
            full((MCONV * SUBLANES, INNER)),
            full((1, INNER)), full((HEADS, DH, DH)), full((HEADS, DH, DH)), full((HEADS, DH, DH)),
            full((3 * INNER, LANES)), full((1, LANES)),
            full((MCONV - 1, R, INNER)), full((R, HEADS, DH)), full((R, LANES)),
        ],
        out_specs=([full((R, INNER))] * 5 + [full((R, HEADS, DH))] + [full((R, LANES))] * 4
                   + [full((MCONV - 1, R, INNER))]),
        out_shape=([act] * 5 + [jax.ShapeDtypeStruct((R, HEADS, DH), f32)] + [small] * 4
                   + [jax.ShapeDtypeStruct((MCONV - 1, R, INNER), f32)]),
        compiler_params=_params("arbitrary"),
        name="ml_front_sample",
    )(x, mod, g, *[w["ml_w_in"]] * len(win), w["ml_w_conv"], w["ml_b_conv"], w["ml_w_q"], w["ml_w_k"],
      w["ml_w_v"], w["ml_w_igf"], w["ml_b_igf"], st, n, m)


def _ml_back_sample_kernel(a1_ref, a2_ref, v_ref, qc_ref, lng_ref, xc_ref, z_ref, x_ref, mod_ref, skip_ref,
                           wout_ref, o_ref):
    hs = []
    for h in range(HEADS):
        seg = slice(h * DH, (h + 1) * DH)
        hc = a1_ref[:, h:h + 1] * v_ref[:, seg] + a2_ref[:, h:h + 1] * qc_ref[:, seg]
        hs.append(_layernorm(hc, lng_ref[:, seg]))
    hh = jnp.concatenate(hs, axis=1)
    act = _ml_act(hh, xc_ref[...], z_ref[...], skip_ref[...])
    o_ref[...] = _ml_residual(act, x_ref[:, 0, :], mod_ref[...], wout_ref)


def _ml_back_sample(a1, a2, v, qc, xc, z, x, mod, w):
    R, _, D = x.shape
    full = lambda shape: pl.BlockSpec(shape, lambda i: (0,) * len(shape))
    wout = _col_specs(INNER, D)
    return pl.pallas_call(
        _grouped(_ml_back_sample_kernel, (1,) * 10 + (len(wout),)),
        grid=(1,),
        in_specs=[full((R, LANES)), full((R, LANES)), full((R, INNER)), full((R, INNER)), full((1, INNER)),
                  full((R, INNER)), full((R, INNER)), full((R, 1, D)),
                  pl.BlockSpec((None, R, 3 * D), lambda i: (0, 0, 0)), full((1, INNER)),
                  *wout],
        out_specs=full((R, D)),
        out_shape=jax.ShapeDtypeStruct((R, D), f32),
        compiler_params=_params("arbitrary"),
        name="ml_back_sample",
    )(a1, a2, v, qc, w["ml_ln_g"], xc, z, x, mod, w["ml_skip"], *[w["ml_w_out"]] * len(wout))


def _cf_in(x, mod, g_ref, win_ref, bin_ref):
    h = _rms_mod(x, g_ref[...], mod)
    agz = _add_rows(_dot_cols(_bf(h), win_ref), bin_ref[...])
    u = agz[:, :INNER] * _sigmoid(agz[:, INNER:2 * INNER])
    return u, agz[:, 2 * INNER:]


def _cf_out(y, z, x, mod, lng_ref, lnb_ref, wout_ref, fg_ref):
    yn = _layernorm(y, lng_ref[...], lnb_ref[...])
    out = _dot_cols(_bf(_silu(yn) * _silu(z)), wout_ref)
    return _rms(x + _mul_rows(out, mod[:, 2 * D_MODEL:]), fg_ref[...])


def _channel_block(cb):
    start = cb * CONV_COLS
    return pl.ds(start if isinstance(cb, int) else pl.multiple_of(start, CONV_COLS), CONV_COLS)


def _dwconv_block(ubuf_ref, sh_ref, wb_ref, bdw_ref, y_ref, tm, cols):
    base = CCONV_HIST - (CCONV - 1)
    max_row_off = (base + CCONV - 2) // SUBLANES * SUBLANES
    for r in range(1, SUBLANES):
        sh_ref[r - 1] = ubuf_ref[r:r + tm + max_row_off, cols]
    b = bdw_ref[:, cols]
    for rc in range(tm // CONV_ROWS):
        acc = jnp.broadcast_to(b, (CONV_ROWS, CONV_COLS))
        for j in range(CCONV):
            a, r = divmod(base + j, SUBLANES)
            rows = pl.ds(SUBLANES * a + CONV_ROWS * rc, CONV_ROWS)
            src = ubuf_ref[rows, cols] if r == 0 else sh_ref[r - 1, rows, :]
            acc = acc + _mul_rows(src, wb_ref[SUBLANES * j:SUBLANES * (j + 1), cols])
        y_ref[pl.ds(CONV_ROWS * rc, CONV_ROWS), cols] = acc


def _memory_step(step, dec_ref, c_ref, q_ref, kw_ref, v_ref, cout_ref):
    head = (step // 2) % HEADS
    first = (step // (2 * HEADS)) * STEP_BATCH
    qt = q_ref[...].T
    kwt = kw_ref[...].T
    parts = []
    for i in range(STEP_BATCH):
        decay = dec_ref[(first + i) * HEADS + head]
        c = c_ref[i]
        parts.append(jnp.sum(c * qt[:, i:i + 1], axis=0, keepdims=True))
        cout_ref[i] = decay * c + kwt[:, i:i + 1] * v_ref[i:i + 1, :]
    return jnp.concatenate(parts, axis=0)


def _cf_prompt_kernel(x_ref, mod_ref, g_ref, win_ref, bin_ref, wdw_ref, bdw_ref, lng_ref, lnb_ref, wout_ref,
                      fg_ref, dec_ref, cs_ref, qs_ref, kws_ref, vs_ref,
                      o_ref, cconv_ref, csout_ref, qc_ref, ubuf_ref, y_ref, sh_ref, wb_ref,
                      g8, bin8, lng8, lnb8, fg8, mod8, *, tm):
    step = pl.program_id(0) * pl.num_programs(1) + pl.program_id(1)
    nblk = INNER // CONV_COLS

    _replicate_rows(step == 0, [(g_ref, g8), (bin_ref, bin8), (lng_ref, lng8), (lnb_ref, lnb8), (fg_ref, fg8)])
    _replicate_mod(mod_ref, mod8, pl.program_id(0))
    g_ref, bin_ref, lng_ref, lnb_ref, fg_ref, mod_ref = g8, bin8, lng8, lnb8, fg8, mod8

    @pl.when(step == 0)
    def _():
        for j in range(CCONV):
            wb_ref[SUBLANES * j:SUBLANES * (j + 1), :] = jnp.broadcast_to(wdw_ref[j:j + 1, :], (SUBLANES, INNER))

    @pl.when(pl.program_id(1) == 0)
    def _():
        ubuf_ref[0:CCONV_HIST, :] = jnp.zeros((CCONV_HIST, INNER), f32)

    def conv_blocks(lo, hi):
        def block_body(cb, carry):
            _dwconv_block(ubuf_ref, sh_ref, wb_ref, bdw_ref, y_ref, tm, _channel_block(cb))
            return carry

        lax.fori_loop(lo, hi, block_body, 0)

    n_ag = 2 * INNER // W_COLS

    def normed():
        return _bf(_rms_mod(x_ref[...], g_ref[...], mod_ref[...]))

    @pl.when(step % 2 == 0)
    def _():
        qc_ref[...] = _memory_step(step, dec_ref, cs_ref, qs_ref, kws_ref, vs_ref, csout_ref)
        ag = _add_rows(_dot_cols(normed(), win_ref[:n_ag]), bin_ref[:, :2 * INNER])
        ubuf_ref[CCONV_HIST:CCONV_HIST + tm, :] = ag[:, :INNER] * _sigmoid(ag[:, INNER:])
        conv_blocks(0, nblk // 2)

    @pl.when(step % 2 == 1)
    def _():
        qc_ref[...] += _memory_step(step, dec_ref, cs_ref, qs_ref, kws_ref, vs_ref, csout_ref)
        z = _add_rows(_dot_cols(normed(), win_ref[n_ag:]), bin_ref[:, 2 * INNER:])
        conv_blocks(nblk // 2, nblk)
        cconv_ref[...] = ubuf_ref[CCONV_HIST + tm - (CCONV - 1):CCONV_HIST + tm, :]
        ubuf_ref[0:CCONV_HIST, :] = ubuf_ref[tm:tm + CCONV_HIST, :]
        o_ref[...] = _cf_out(y_ref[...], z, x_ref[...], mod_ref[...], lng_ref, lnb_ref, wout_ref, fg_ref)


def _cf_prompt(x, mod4, g, w, final_g, dec_flat, C_s, q_s, kw_s, v_s):
    B, T, D = x.shape
    R = C_s.shape[0]
    tm = ROW_TILE
    steps_t = 2 * (T // tm)
    assert B * steps_t * STEP_BATCH == R * HEADS * 2, "one (sequence group, head, key half) per grid step"
    row = lambda b, t: (b, t // 2, 0)

    def ghk(b, t):
        s = b * steps_t + t
        return s // (2 * HEADS), (s // 2) % HEADS, s % 2

    c_spec = pl.BlockSpec((STEP_BATCH, None, DH // 2, DH), lambda b, t: (*ghk(b, t), 0))
    half_spec = pl.BlockSpec((STEP_BATCH, DH // 2), lambda b, t: (ghk(b, t)[0], 2 * ghk(b, t)[1] + ghk(b, t)[2]))
    head_spec = pl.BlockSpec((STEP_BATCH, DH), lambda b, t: ghk(b, t)[:2])
    win, wout = _col_specs(D, 3 * INNER), _col_specs(INNER, D)
    return pl.pallas_call(
        functools.partial(_grouped(_cf_prompt_kernel, (1, 1, 1, len(win), 1, 1, 1, 1, 1, len(wout))), tm=tm),
        grid=(B, steps_t),
        in_specs=[
            pl.BlockSpec((None, tm, D), row),
            pl.BlockSpec((None, B, 3 * D), lambda b, t: (1, mod4.shape[1] // B - 1, 0)),
            _const_spec((1, D)),
            *win,
            _const_spec((1, 3 * INNER)),
            _const_spec((CCONV, INNER)),
            _const_spec((1, INNER)),
            _const_spec((1, INNER)),
            _const_spec((1, INNER)),
            *wout,
            _const_spec((1, D)),
            pl.BlockSpec(memory_space=pltpu.SMEM),
            c_spec, half_spec, half_spec, head_spec,
        ],
        out_specs=[
            pl.BlockSpec((None, tm, D), row),
            pl.BlockSpec((None, CCONV - 1, INNER), lambda b, t: (b, 0, 0)),
            c_spec,
            head_spec,
        ],
        out_shape=[
            jax.ShapeDtypeStruct((B, T, D), f32),
            jax.ShapeDtypeStruct((B, CCONV - 1, INNER), f32),
            jax.ShapeDtypeStruct(C_s.shape, f32),
            jax.ShapeDtypeStruct((R, INNER), f32),
        ],
        scratch_shapes=[
            pltpu.VMEM((CCONV_HIST + tm, INNER), f32),
            pltpu.VMEM((tm, INNER), f32),
            pltpu.VMEM((SUBLANES - 1, tm + CCONV_HIST - SUBLANES, CONV_COLS), f32),
            pltpu.VMEM((SUBLANES * CCONV, INNER), f32),
            pltpu.VMEM((SUBLANES, D), f32),
            pltpu.VMEM((SUBLANES, 3 * INNER), f32),
            pltpu.VMEM((SUBLANES, INNER), f32),
            pltpu.VMEM((SUBLANES, INNER), f32),
            pltpu.VMEM((SUBLANES, D), f32),
            pltpu.VMEM((SUBLANES, 3 * D), f32),
        ],
        compiler_params=_params("arbitrary", "arbitrary"),
        name="cf_prompt",
    )(x, mod4, g, *[w["cf_w_in"]] * len(win), w["cf_b_in"], w["cf_w_dw"], w["cf_b_dw"], w["cf_ln_g"],
      w["cf_ln_b"], *[w["cf_w_out"]] * len(wout), final_g, dec_flat, C_s, q_s, kw_s, v_s)


def _cf_sample_kernel(x_ref, mod_ref, g_ref, win_ref, bin_ref, wdw_ref, bdw_ref, lng_ref, lnb_ref, wout_ref,
                      fg_ref, st_ref, o_ref, stout_ref, u_ref, z_ref, y_ref, *, tb):
    i = pl.program_id(0)

    @pl.when(i == 0)
    def _():
        u, z = _cf_in(x_ref[...], mod_ref[...], g_ref, win_ref, bin_ref)
        u_ref[...] = u
        z_ref[...] = z

    rows = pl.ds(pl.multiple_of(i * tb, tb), tb)
    u = u_ref[rows, :]
    acc = bdw_ref[...] + wdw_ref[CCONV - 1:CCONV, :] * u
    for j in range(CCONV - 1):
        acc = acc + wdw_ref[j:j + 1, :] * st_ref[j]
    y_ref[rows, :] = acc
    stout_ref[0:CCONV - 2] = st_ref[1:CCONV - 1]
    stout_ref[CCONV - 2] = u

    @pl.when(i == pl.num_programs(0) - 1)
    def _():
        o_ref[:, 0, :] = _cf_out(y_ref[...], z_ref[...], x_ref[...], mod_ref[...], lng_ref, lnb_ref, wout_ref,
                                 fg_ref)


def _cf_sample(x, mod, g, w, final_g, st):
    R, D = x.shape
    tb = SAMPLE_TILE
    state = pl.BlockSpec((CCONV - 1, tb, INNER), lambda i: (0, i, 0))
    win, wout = _col_specs(D, 3 * INNER), _col_specs(INNER, D)
    return pl.pallas_call(
        functools.partial(_grouped(_cf_sample_kernel, (1, 1, 1, len(win), 1, 1, 1, 1, 1, len(wout))), tb=tb),
        grid=(R // tb,),
        in_specs=[
            _const_spec((R, D)),
            pl.BlockSpec((None, R, 3 * D), lambda i: (1, 0, 0), pipeline_mode=pl.Buffered(1)),
            _const_spec((1, D)),
            *win,
            _const_spec((1, 3 * INNER)),
            _const_spec((CCONV, INNER)),
            _const_spec((1, INNER)),
            _const_spec((1, INNER)),
            _const_spec((1, INNER)),
            *wout,
            _const_spec((1, D)),
            state,
        ],
        out_specs=[pl.BlockSpec((R, 1, D), lambda i: (0, 0, 0)), state],
        out_shape=[jax.ShapeDtypeStruct((R, 1, D), f32), jax.ShapeDtypeStruct((CCONV - 1, R, INNER), f32)],
        scratch_shapes=[pltpu.VMEM((R, INNER), f32)] * 3,
        compiler_params=_params("arbitrary"),
        name="cf_sample",
    )(x, mod, g, *[w["cf_w_in"]] * len(win), w["cf_b_in"], w["cf_w_dw"], w["cf_b_dw"], w["cf_ln_g"],
      w["cf_ln_b"], *[w["cf_w_out"]] * len(wout), final_g, st)


def kernel(x_prompt, x_sample, c_prompt, c_sample, state_mlstm_C, state_mlstm_n, state_mlstm_m, state_mlstm_conv, state_conf_conv, norm_g, w_ada, b_ada, ml_w_in, ml_w_conv, ml_b_conv, ml_w_q, ml_w_k, ml_w_v, ml_w_ig, ml_b_ig, ml_w_fg, ml_b_fg, ml_ln_g, ml_skip, ml_w_out, cf_w_in, cf_b_in, cf_w_dw, cf_b_dw, cf_ln_g, cf_ln_b, cf_w_out, final_g):
    B = x_prompt.shape[0]
    R = x_sample.shape[0]
    D = D_MODEL

    gate_pad = LANES - 2 * HEADS
    w = {
        "ml_w_in": _bf(ml_w_in[0]),
        "ml_w_conv": jnp.repeat(ml_w_conv[0], SUBLANES, axis=0),
        "ml_b_conv": ml_b_conv,
        "ml_w_q": _bf(ml_w_q[0]),
        "ml_w_k": _bf(ml_w_k[0]),
        "ml_w_v": _bf(ml_w_v[0]),
        "ml_w_igf": _bf(jnp.pad(jnp.concatenate([ml_w_ig[0], ml_w_fg[0]], axis=1), ((0, 0), (0, gate_pad)))),
        "ml_b_igf": jnp.pad(jnp.concatenate([ml_b_ig, ml_b_fg], axis=1), ((0, 0), (0, gate_pad))),        "ml_ln_g": ml_ln_g.reshape(1, INNER),
        "ml_skip": ml_skip,
        "ml_w_out": _bf(ml_w_out[0]),
        "cf_w_in": _bf(cf_w_in[0]),
        "cf_b_in": cf_b_in,
        "cf_w_dw": cf_w_dw[0],
        "cf_b_dw": cf_b_dw,
        "cf_ln_g": cf_ln_g,
        "cf_ln_b": cf_ln_b,
        "cf_w_out": _bf(cf_w_out[0]),
    }
    final_g2 = final_g.reshape(1, D)

    assert R % B == 0
    mod = _ada(jnp.concatenate([c_sample, c_prompt], axis=0), w_ada, b_ada)

    tap_major = lambda a: jnp.transpose(a, (1, 0, 2))
    xs = x_sample
    m_pad = jnp.pad(state_mlstm_m[0], ((0, 0), (0, LANES - HEADS)))
    (q_s, kw_s, v_s, xc_s, z_s, n_s, a1, a2, dec, m_s, mconv_s) = _ml_front_sample(
        xs, mod, norm_g[0:1], w, tap_major(state_mlstm_conv[0]), state_mlstm_n[0], m_pad)

    x1, mconv_p, C_p, n_p, m_p = _ml_prompt(x_prompt, mod, norm_g[0:1], w)
    y_prompt, cconv_p, C_s, qc = _cf_prompt(
        x1, mod, norm_g[1:2], w, final_g2, dec[:, :HEADS].reshape(R * HEADS), state_mlstm_C[0], q_s, kw_s, v_s)

    x1_s = _ml_back_sample(a1, a2, v_s, qc, xc_s, z_s, xs, mod, w)
    y_sample, cconv_s = _cf_sample(x1_s, mod, norm_g[1:2], w, final_g2, tap_major(state_conf_conv[0]))

    return (
        y_prompt,
        y_sample,
        C_p[None],
        C_s[None],
        n_p[None],
        n_s[None],
        m_p[:, 0, :HEADS][None],
        m_s[:, :HEADS][None],
        mconv_p[None],
        tap_major(mconv_s)[None],
        cconv_p[None],
        tap_major(cconv_s)[None],
    )
```

```python
import functools

import jax
import jax.numpy as jnp
from jax import lax
from jax.experimental import pallas as pl
from jax.experimental.pallas import tpu as pltpu

D_MODEL = 1024
INNER = 2 * D_MODEL
HEADS = 4
DH = INNER // HEADS
MCONV = 4
CCONV = 31
EPS = 1e-6
NEG = -1e30
K_SCALE = DH ** -0.5

LANES = 128
SUBLANES = 8
VMEM_LIMIT = 56 * 2 ** 20

ROW_TILE = 256
CHUNK = 256
STEP_BATCH = 8
CONV_ROWS = 64
CONV_COLS = 128
CCONV_HIST = 32
SAMPLE_TILE = 16
W_COLS = 512

f32 = jnp.float32
bf16 = jnp.bfloat16


def _bf(x):
    return x.astype(bf16)


def _dot(a, b):
    return jnp.dot(a, b, preferred_element_type=f32)


def _dot_cols(a, w_refs):
    return jnp.concatenate([_dot(a, r[...]) for r in w_refs], axis=1)


def _col_specs(k, n):
    return [pl.BlockSpec((k, W_COLS), lambda *_, c=c: (0, c), pipeline_mode=pl.Buffered(1))
            for c in range(n // W_COLS)]


def _grouped(kernel, counts):
    def body(*refs, **kw):
        it = iter(refs)
        args = [next(it) if n == 1 else tuple(next(it) for _ in range(n)) for n in counts]
        return kernel(*args, *it, **kw)
    return body


def _sigmoid(x):
    return 0.5 * jnp.tanh(0.5 * x) + 0.5


def _silu(x):
    h = 0.5 * x
    return h * jnp.tanh(h) + h


def _log_sigmoid(x):
    return jnp.minimum(x, 0.0) - jnp.log1p(jnp.exp(-jnp.abs(x)))


def _rows_op(op, x, r):
    m, n = x.shape
    if r.shape[0] in (1, m):
        return op(x, r)
    return op(x.reshape(m // SUBLANES, SUBLANES, n), r[None]).reshape(m, n)


def _mul_rows(x, r):
    return _rows_op(jnp.multiply, x, r)


def _add_rows(x, r):
    return _rows_op(jnp.add, x, r)


def _replicate_rows(when, pairs):
    @pl.when(when)
    def _():
        for src, dst in pairs:
            dst[...] = jnp.broadcast_to(src[...], dst.shape)


def _replicate_mod(mod_ref, mod8_ref, b):
    @pl.when(pl.program_id(1) == 0)
    def _():
        mod8_ref[...] = jnp.broadcast_to(mod_ref[pl.ds(b, 1), :], mod8_ref.shape)


def _rms(x, g):
    return _mul_rows(x * lax.rsqrt(jnp.mean(x * x, axis=-1, keepdims=True) + EPS), g)


def _rms_mod(x, g, mod):
    return _add_rows(_mul_rows(_rms(x, g), 1.0 + mod[:, D_MODEL:2 * D_MODEL]), mod[:, :D_MODEL])


def _layernorm(x, g, b=None):
    mu = jnp.mean(x, axis=-1, keepdims=True)
    xc = x - mu
    var = jnp.mean(xc * xc, axis=-1, keepdims=True)
    y = _mul_rows(xc * lax.rsqrt(var + EPS), g)
    return y if b is None else _add_rows(y, b)


def _const_spec(shape):
    n = len(shape)
    return pl.BlockSpec(shape, lambda *_: (0,) * n, pipeline_mode=pl.Buffered(1))


def _params(*sem):
    return pltpu.CompilerParams(dimension_semantics=sem, vmem_limit_bytes=VMEM_LIMIT)


def _ada_kernel(c_ref, w_ref, b_ref, o_ref):
    o_ref[...] = _dot(_bf(_silu(c_ref[...])), _bf(w_ref[...])) + b_ref[...]


def _ada(c_all, w_ada, b_ada):
    depth, d, d3 = w_ada.shape
    rows = c_all.shape[0]
    tn = D_MODEL
    return pl.pallas_call(
        _ada_kernel,
        grid=(depth, d3 // tn),
        in_specs=[
            pl.BlockSpec((rows, d), lambda i, j: (0, 0)),
            pl.BlockSpec((None, d, tn), lambda i, j: (i, 0, j)),
            pl.BlockSpec((None, 1, tn), lambda i, j: (i, 0, j)),
        ],
        out_specs=pl.BlockSpec((None, rows, tn), lambda i, j: (i, 0, j)),
        out_shape=jax.ShapeDtypeStruct((depth, rows, d3), f32),
        compiler_params=_params("arbitrary", "arbitrary"),
        name="ada",
    )(c_all, w_ada, b_ada.reshape(depth, 1, d3))


def _ml_proj(xm, xc, wq_ref, wk_ref, wv_ref):
    xcb, xmb = _bf(xc), _bf(xm)
    qs, ks, vs = [], [], []
    for h in range(HEADS):
        seg = slice(h * DH, (h + 1) * DH)
        qs.append(_dot(xcb[:, seg], wq_ref[h]))
        ks.append(_dot(xcb[:, seg], wk_ref[h]))
        vs.append(_dot(xmb[:, seg], wv_ref[h]))
    return jnp.concatenate(qs, axis=1), jnp.concatenate(ks, axis=1), jnp.concatenate(vs, axis=1)


def _ml_qkv(xm, xc, wq_ref, wk_ref, wv_ref, wigf_ref, bigf_ref):
    q, k, v = _ml_proj(xm, xc, wq_ref, wk_ref, wv_ref)
    pre = (_dot(_bf(q), wigf_ref[0:INNER, :]) + _dot(_bf(k), wigf_ref[INNER:2 * INNER, :])
           + _dot(_bf(v), wigf_ref[2 * INNER:3 * INNER, :]) + bigf_ref[...])
    lane = lax.broadcasted_iota(jnp.int32, pre.shape, 1)
    gates = jnp.where(lane >= HEADS, _log_sigmoid(pre), pre)
    return q, k, v, gates


def _ml_act(hh, xc, z, skip):
    return _bf((hh + _mul_rows(xc, skip)) * _silu(z))


def _ml_residual(act, x, mod, wout_ref):
    return x + _mul_rows(_dot_cols(act, wout_ref), mod[:, 2 * D_MODEL:])


def _tap(wconv_ref, j):
    return wconv_ref[SUBLANES * j:SUBLANES * (j + 1), :]


def _ml_conv_prompt(xm, wconv_ref, bconv_ref, xbuf_ref, tm):
    xbuf_ref[SUBLANES:SUBLANES + tm, :] = xm
    acc = _add_rows(_mul_rows(xm, _tap(wconv_ref, MCONV - 1)), bconv_ref[...])
    for j in range(MCONV - 1):
        start = SUBLANES - (MCONV - 1) + j
        acc = acc + _mul_rows(xbuf_ref[start:start + tm, :], _tap(wconv_ref, j))
    return acc


def _mlstm_chunk(q, ks, v, gates, lng_ref, C_ref, n_ref, m_ref):
    L = q.shape[0]
    row = lax.broadcasted_iota(jnp.int32, (L, L), 0)
    col = lax.broadcasted_iota(jnp.int32, (L, L), 1)
    causal = row >= col
    tri = jnp.where(causal, 1.0, 0.0).astype(bf16)
    hi = _bf(gates)
    lo = _bf(gates - hi.astype(f32))
    bcum = _dot(tri, hi) + _dot(tri, lo)
    gates_t = gates.T
    bcum_t = bcum.T

    hs = []
    for h in range(HEADS):
        seg = slice(h * DH, (h + 1) * DH)
        ig_col = gates[:, h:h + 1]
        b_col = bcum[:, HEADS + h:HEADS + h + 1]
        ig_row = gates_t[h:h + 1, :]
        b_row = bcum_t[HEADS + h:HEADS + h + 1, :]
        m_prev = m_ref[:, h:h + 1]
        qh, kh, vh = q[:, seg], ks[:, seg], v[:, seg]

        dmat = jnp.where(causal, b_col - b_row + ig_row, NEG)
        inter = b_col + m_prev
        m_t = jnp.maximum(inter, jnp.max(dmat, axis=1, keepdims=True))
        w_intra = jnp.exp(dmat - m_t)
        w_inter = jnp.exp(inter - m_t)
        s = lax.dot_general(qh, kh, (((1,), (1,)), ((), ())), preferred_element_type=f32) * w_intra
        num = _dot(_bf(s), vh) + w_inter * _dot(qh, _bf(C_ref[h]))
        qn = jnp.sum(qh.astype(f32) * n_ref[h:h + 1, :], axis=1, keepdims=True)
        den = jnp.sum(s, axis=1, keepdims=True) + w_inter * qn
        hc = num / jnp.maximum(jnp.abs(den), jnp.exp(-m_t))
        hs.append(_layernorm(hc, lng_ref[:, seg]))

        m_new = m_t[L - 1:L, :]
        b_last = b_col[L - 1:L, :]
        decay = jnp.exp(b_last + m_prev - m_new)
        kw = kh.astype(f32) * jnp.exp(b_last - b_col + ig_col - m_new)
        C_ref[h] = decay * C_ref[h] + lax.dot_general(_bf(kw), vh, (((0,), (0,)), ((), ())),
                                                      preferred_element_type=f32)
        n_ref[h:h + 1, :] = decay * n_ref[h:h + 1, :] + jnp.sum(kw, axis=0, keepdims=True)
        m_ref[:, h:h + 1] = m_new
    return jnp.concatenate(hs, axis=1)


def _ml_prompt_kernel(x_ref, mod_ref, g_ref, win_ref, wconv_ref, bconv_ref, wq_ref, wk_ref, wv_ref, wigf_ref,
                      bigf_ref, lng_ref, skip_ref, wout_ref,
                      o_ref, mconv_ref, C_ref, n_ref, m_ref, xbuf_ref, g8, bconv8, lng8, skip8, mod8, *, tm):
    first = (pl.program_id(0) == 0) & (pl.program_id(1) == 0)
    _replicate_rows(first, [(g_ref, g8), (bconv_ref, bconv8), (lng_ref, lng8), (skip_ref, skip8)])
    _replicate_mod(mod_ref, mod8, pl.program_id(0))
    g_ref, bconv_ref, lng_ref, skip_ref, mod_ref = g8, bconv8, lng8, skip8, mod8

    @pl.when(pl.program_id(1) == 0)
    def _():
        xbuf_ref[0:SUBLANES, :] = jnp.zeros((SUBLANES, INNER), f32)
        C_ref[...] = jnp.zeros_like(C_ref)
        n_ref[...] = jnp.zeros_like(n_ref)
        m_ref[...] = jnp.zeros_like(m_ref)

    x = x_ref[...]
    mod = mod_ref[...]
    xmz = _dot_cols(_bf(_rms_mod(x, g_ref[...], mod)), win_ref)
    xm = xmz[:, :INNER]
    z = xmz[:, INNER:]
    xc = _silu(_ml_conv_prompt(xm, wconv_ref, bconv_ref, xbuf_ref, tm))
    mconv_ref[...] = xbuf_ref[SUBLANES + tm - (MCONV - 1):SUBLANES + tm, :]
    xbuf_ref[0:SUBLANES, :] = xbuf_ref[tm:tm + SUBLANES, :]

    q, k, v, gates = _ml_qkv(xm, xc, wq_ref, wk_ref, wv_ref, wigf_ref, bigf_ref)
    hh = _mlstm_chunk(_bf(q), _bf(k * K_SCALE), _bf(v), gates, lng_ref, C_ref, n_ref, m_ref)
    o_ref[...] = _ml_residual(_ml_act(hh, xc, z, skip_ref[...]), x, mod, wout_ref)


def _ml_prompt(x, mod, g, w):
    B, T, D = x.shape
    tm = CHUNK
    row = lambda b, t: (b, t, 0)
    win, wout = _col_specs(D, 2 * INNER), _col_specs(INNER, D)
    return pl.pallas_call(
        functools.partial(_grouped(_ml_prompt_kernel, (1, 1, 1, len(win), 1, 1, 1, 1, 1, 1, 1, 1, 1, len(wout))),
                          tm=tm),
        grid=(B, T // tm),
        in_specs=[
            pl.BlockSpec((None, tm, D), row),
            pl.BlockSpec((None, B, 3 * D), lambda b, t: (0, mod.shape[1] // B - 1, 0)),
            _const_spec((1, D)),
            *win,
            _const_spec((MCONV * SUBLANES, INNER)),
            _const_spec((1, INNER)),
            _const_spec((HEADS, DH, DH)),
            _const_spec((HEADS, DH, DH)),
            _const_spec((HEADS, DH, DH)),
            _const_spec((3 * INNER, LANES)),
            _const_spec((1, LANES)),
            _const_spec((1, INNER)),
            _const_spec((1, INNER)),
            *wout,
        ],
        out_specs=[
            pl.BlockSpec((None, tm, D), row),
            pl.BlockSpec((None, MCONV - 1, INNER), lambda b, t: (b, 0, 0)),
            pl.BlockSpec((None, HEADS, DH, DH), lambda b, t: (b, 0, 0, 0)),
            pl.BlockSpec((None, HEADS, DH), lambda b, t: (b, 0, 0)),
            pl.BlockSpec((None, 1, LANES), lambda b, t: (b, 0, 0)),
        ],
        out_shape=[
            jax.ShapeDtypeStruct((B, T, D), f32),
            jax.ShapeDtypeStruct((B, MCONV - 1, INNER), f32),
            jax.ShapeDtypeStruct((B, HEADS, DH, DH), f32),
            jax.ShapeDtypeStruct((B, HEADS, DH), f32),
            jax.ShapeDtypeStruct((B, 1, LANES), f32),
        ],
        scratch_shapes=[
            pltpu.VMEM((SUBLANES + tm, INNER), f32),
            pltpu.VMEM((SUBLANES, D), f32),
            pltpu.VMEM((SUBLANES, INNER), f32),
            pltpu.VMEM((SUBLANES, INNER), f32),
            pltpu.VMEM((SUBLANES, INNER), f32),
            pltpu.VMEM((SUBLANES, 3 * D), f32),
        ],
        compiler_params=_params("arbitrary", "arbitrary"),
        name="ml_prompt",
    )(x, mod, g, *[w["ml_w_in"]] * len(win), w["ml_w_conv"], w["ml_b_conv"], w["ml_w_q"], w["ml_w_k"],
      w["ml_w_v"], w["ml_w_igf"], w["ml_b_igf"], w["ml_ln_g"], w["ml_skip"], *[w["ml_w_out"]] * len(wout))


def _ml_front_sample_kernel(x_ref, mod_ref, g_ref, win_ref, wconv_ref, bconv_ref, wq_ref, wk_ref, wv_ref,
                            wigf_ref, bigf_ref, st_ref, n_ref, m_ref,
                            q_ref, kw_ref, v_ref, xc_ref, z_ref, nnew_ref, a1_ref, a2_ref, dec_ref, mnew_ref,
                            mconv_ref):
    h = _rms_mod(x_ref[:, 0, :], g_ref[...], mod_ref[...])
    xmz = _dot_cols(_bf(h), win_ref)
    xm = xmz[:, :INNER]
    z_ref[...] = xmz[:, INNER:]
    acc = _add_rows(_mul_rows(xm, _tap(wconv_ref, MCONV - 1)), bconv_ref[...])
    for j in range(MCONV - 1):
        acc = acc + _mul_rows(st_ref[j], _tap(wconv_ref, j))
    xc = _silu(acc)
    xc_ref[...] = xc
    mconv_ref[0:MCONV - 2] = st_ref[1:MCONV - 1]
    mconv_ref[MCONV - 2] = xm

    q, k, v, gates = _ml_qkv(xm, xc, wq_ref, wk_ref, wv_ref, wigf_ref, bigf_ref)
    ks = k * K_SCALE
    q_ref[...] = q
    v_ref[...] = v

    ig = gates
    lf = pltpu.roll(gates, LANES - HEADS, axis=1)
    inter = lf + m_ref[...]
    m_new = jnp.maximum(inter, ig)
    w_intra = jnp.exp(ig - m_new)
    w_inter = jnp.exp(inter - m_new)
    lane = lax.broadcasted_iota(jnp.int32, ig.shape, 1)
    qk = jnp.zeros_like(ig)
    qn = jnp.zeros_like(ig)
    for h in range(HEADS):
        seg = slice(h * DH, (h + 1) * DH)
        qk = jnp.where(lane == h, jnp.sum(q[:, seg] * ks[:, seg], axis=1, keepdims=True), qk)
        n_h = n_ref[:, h, :]
        qn = jnp.where(lane == h, jnp.sum(q[:, seg] * n_h, axis=1, keepdims=True), qn)
        kw = w_intra[:, h:h + 1] * ks[:, seg]
        kw_ref[:, seg] = kw
        nnew_ref[:, h, :] = w_inter[:, h:h + 1] * n_h + kw
    s = qk * w_intra
    den = s + w_inter * qn
    dn = jnp.maximum(jnp.abs(den), jnp.exp(-m_new))
    valid = lane < HEADS
    a1_ref[...] = jnp.where(valid, s / dn, 0.0)
    a2_ref[...] = jnp.where(valid, w_inter / dn, 0.0)
    dec_ref[...] = jnp.where(valid, w_inter, 0.0)
    mnew_ref[...] = jnp.where(valid, m_new, 0.0)


def _ml_front_sample(x, mod, g, w, st, n, m):
    R, _, D = x.shape
    full = lambda shape: pl.BlockSpec(shape, lambda i: (0,) * len(shape))
    act = jax.ShapeDtypeStruct((R, INNER), f32)
    small = jax.ShapeDtypeStruct((R, LANES), f32)
    win = _col_specs(D, 2 * INNER)
    return pl.pallas_call(
        _grouped(_ml_front_sample_kernel, (1, 1, 1, len(win))),
        grid=(1,),
        in_specs=[
            full((R, 1, D)), pl.BlockSpec((None, R, 3 * D), lambda i: (0, 0, 0)), full((1, D)), *win,
            full((MCONV * SUBLANES, INNER)),
            full((1, INNER)), full((HEADS, DH, DH)), full((HEADS, DH, DH)), full((HEADS, DH, DH)),
            full((3 * INNER, LANES)), full((1, LANES)),
            full((MCONV - 1, R, INNER)), full((R, HEADS, DH)), full((R, LANES)),
        ],
        out_specs=([full((R, INNER))] * 5 + [full((R, HEADS, DH))] + [full((R, LANES))] * 4
                   + [full((MCONV - 1, R, INNER))]),
        out_shape=([act] * 5 + [jax.ShapeDtypeStruct((R, HEADS, DH), f32)] + [small] * 4
                   + [jax.ShapeDtypeStruct((MCONV - 1, R, INNER), f32)]),
        compiler_params=_params("arbitrary"),
        name="ml_front_sample",
    )(x, mod, g, *[w["ml_w_in"]] * len(win), w["ml_w_conv"], w["ml_b_conv"], w["ml_w_q"], w["ml_w_k"],
      w["ml_w_v"], w["ml_w_igf"], w["ml_b_igf"], st, n, m)


def _ml_back_sample_kernel(a1_ref, a2_ref, v_ref, qc_ref, lng_ref, xc_ref, z_ref, x_ref, mod_ref, skip_ref,
                           wout_ref, o_ref):
    hs = []
    for h in range(HEADS):
        seg = slice(h * DH, (h + 1) * DH)
        hc = a1_ref[:, h:h + 1] * v_ref[:, seg] + a2_ref[:, h:h + 1] * qc_ref[:, seg]
        hs.append(_layernorm(hc, lng_ref[:, seg]))
    hh = jnp.concatenate(hs, axis=1)
    act = _ml_act(hh, xc_ref[...], z_ref[...], skip_ref[...])
    o_ref[...] = _ml_residual(act, x_ref[:, 0, :], mod_ref[...], wout_ref)


def _ml_back_sample(a1, a2, v, qc, xc, z, x, mod, w):
    R, _, D = x.shape
    full = lambda shape: pl.BlockSpec(shape, lambda i: (0,) * len(shape))
    wout = _col_specs(INNER, D)
    return pl.pallas_call(
        _grouped(_ml_back_sample_kernel, (1,) * 10 + (len(wout),)),
        grid=(1,),
        in_specs=[full((R, LANES)), full((R, LANES)), full((R, INNER)), full((R, INNER)), full((1, INNER)),
                  full((R, INNER)), full((R, INNER)), full((R, 1, D)),
                  pl.BlockSpec((None, R, 3 * D), lambda i: (0, 0, 0)), full((1, INNER)),
                  *wout],
        out_specs=full((R, D)),
        out_shape=jax.ShapeDtypeStruct((R, D), f32),
        compiler_params=_params("arbitrary"),
        name="ml_back_sample",
    )(a1, a2, v, qc, w["ml_ln_g"], xc, z, x, mod, w["ml_skip"], *[w["ml_w_out"]] * len(wout))


def _cf_in(x, mod, g_ref, win_ref, bin_ref):
    h = _rms_mod(x, g_ref[...], mod)
    agz = _add_rows(_dot_cols(_bf(h), win_ref), bin_ref[...])
    u = agz[:, :INNER] * _sigmoid(agz[:, INNER:2 * INNER])
    return u, agz[:, 2 * INNER:]


def _cf_out(y, z, x, mod, lng_ref, lnb_ref, wout_ref, fg_ref):
    yn = _layernorm(y, lng_ref[...], lnb_ref[...])
    out = _dot_cols(_bf(_silu(yn) * _silu(z)), wout_ref)
    return _rms(x + _mul_rows(out, mod[:, 2 * D_MODEL:]), fg_ref[...])


def _channel_block(cb):
    start = cb * CONV_COLS
    return pl.ds(start if isinstance(cb, int) else pl.multiple_of(start, CONV_COLS), CONV_COLS)


def _dwconv_block(ubuf_ref, sh_ref, wb_ref, bdw_ref, y_ref, tm, cols):
    base = CCONV_HIST - (CCONV - 1)
    max_row_off = (base + CCONV - 2) // SUBLANES * SUBLANES
    for r in range(1, SUBLANES):
        sh_ref[r - 1] = ubuf_ref[r:r + tm + max_row_off, cols]
    b = bdw_ref[:, cols]
    for rc in range(tm // CONV_ROWS):
        acc = jnp.broadcast_to(b, (CONV_ROWS, CONV_COLS))
        for j in range(CCONV):
            a, r = divmod(base + j, SUBLANES)
            rows = pl.ds(SUBLANES * a + CONV_ROWS * rc, CONV_ROWS)
            src = ubuf_ref[rows, cols] if r == 0 else sh_ref[r - 1, rows, :]
            acc = acc + _mul_rows(src, wb_ref[SUBLANES * j:SUBLANES * (j + 1), cols])
        y_ref[pl.ds(CONV_ROWS * rc, CONV_ROWS), cols] = acc


def _memory_step(step, dec_ref, c_ref, q_ref, kw_ref, v_ref, cout_ref):
    head = (step // 2) % HEADS
    first = (step // (2 * HEADS)) * STEP_BATCH
    qt = q_ref[...].T
    kwt = kw_ref[...].T
    parts = []
    for i in range(STEP_BATCH):
        decay = dec_ref[(first + i) * HEADS + head]
        c = c_ref[i]
        parts.append(jnp.sum(c * qt[:, i:i + 1], axis=0, keepdims=True))
        cout_ref[i] = decay * c + kwt[:, i:i + 1] * v_ref[i:i + 1, :]
    return jnp.concatenate(parts, axis=0)


def _cf_prompt_kernel(x_ref, mod_ref, g_ref, win_ref, bin_ref, wdw_ref, bdw_ref, lng_ref, lnb_ref, wout_ref,
                      fg_ref, dec_ref, cs_ref, qs_ref, kws_ref, vs_ref,
                      o_ref, cconv_ref, csout_ref, qc_ref, ubuf_ref, y_ref, sh_ref, wb_ref,
                      g8, bin8, lng8, lnb8, fg8, mod8, *, tm):
    step = pl.program_id(0) * pl.num_programs(1) + pl.program_id(1)
    nblk = INNER // CONV_COLS

    _replicate_rows(step == 0, [(g_ref, g8), (bin_ref, bin8), (lng_ref, lng8), (lnb_ref, lnb8), (fg_ref, fg8)])
    _replicate_mod(mod_ref, mod8, pl.program_id(0))
    g_ref, bin_ref, lng_ref, lnb_ref, fg_ref, mod_ref = g8, bin8, lng8, lnb8, fg8, mod8

    @pl.when(step == 0)
    def _():
        for j in range(CCONV):
            wb_ref[SUBLANES * j:SUBLANES * (j + 1), :] = jnp.broadcast_to(wdw_ref[j:j + 1, :], (SUBLANES, INNER))

    @pl.when(pl.program_id(1) == 0)
    def _():
        ubuf_ref[0:CCONV_HIST, :] = jnp.zeros((CCONV_HIST, INNER), f32)

    def conv_blocks(lo, hi):
        def block_body(cb, carry):
            _dwconv_block(ubuf_ref, sh_ref, wb_ref, bdw_ref, y_ref, tm, _channel_block(cb))
            return carry

        lax.fori_loop(lo, hi, block_body, 0)

    n_ag = 2 * INNER // W_COLS

    def normed():
        return _bf(_rms_mod(x_ref[...], g_ref[...], mod_ref[...]))

    @pl.when(step % 2 == 0)
    def _():
        qc_ref[...] = _memory_step(step, dec_ref, cs_ref, qs_ref, kws_ref, vs_ref, csout_ref)
        ag = _add_rows(_dot_cols(normed(), win_ref[:n_ag]), bin_ref[:, :2 * INNER])
        ubuf_ref[CCONV_HIST:CCONV_HIST + tm, :] = ag[:, :INNER] * _sigmoid(ag[:, INNER:])
        conv_blocks(0, nblk // 2)

    @pl.when(step % 2 == 1)
    def _():
        qc_ref[...] += _memory_step(step, dec_ref, cs_ref, qs_ref, kws_ref, vs_ref, csout_ref)
        z = _add_rows(_dot_cols(normed(), win_ref[n_ag:]), bin_ref[:, 2 * INNER:])
        conv_blocks(nblk // 2, nblk)
        cconv_ref[...] = ubuf_ref[CCONV_HIST + tm - (CCONV - 1):CCONV_HIST + tm, :]
        ubuf_ref[0:CCONV_HIST, :] = ubuf_ref[tm:tm + CCONV_HIST, :]
        o_ref[...] = _cf_out(y_ref[...], z, x_ref[...], mod_ref[...], lng_ref, lnb_ref, wout_ref, fg_ref)


def _cf_prompt(x, mod, g, w, final_g, dec_flat, C_s, q_s, kw_s, v_s):
    B, T, D = x.shape
    R = C_s.shape[0]
    tm = ROW_TILE
    steps_t = 2 * (T // tm)
    assert B * steps_t * STEP_BATCH == R * HEADS * 2, "one (sequence group, head, key half) per grid step"
    row = lambda b, t: (b, t // 2, 0)

    def ghk(b, t):
        s = b * steps_t + t
        return s // (2 * HEADS), (s // 2) % HEADS, s % 2

    c_spec = pl.BlockSpec((STEP_BATCH, None, DH // 2, DH), lambda b, t: (*ghk(b, t), 0))
    half_spec = pl.BlockSpec((STEP_BATCH, DH // 2), lambda b, t: (ghk(b, t)[0], 2 * ghk(b, t)[1] + ghk(b, t)[2]))
    head_spec = pl.BlockSpec((STEP_BATCH, DH), lambda b, t: ghk(b, t)[:2])
    win, wout = _col_specs(D, 3 * INNER), _col_specs(INNER, D)
    return pl.pallas_call(
        functools.partial(_grouped(_cf_prompt_kernel, (1, 1, 1, len(win), 1, 1, 1, 1, 1, len(wout))), tm=tm),
        grid=(B, steps_t),
        in_specs=[
            pl.BlockSpec((None, tm, D), row),
            pl.BlockSpec((None, B, 3 * D), lambda b, t: (1, mod.shape[1] // B - 1, 0)),
            _const_spec((1, D)),
            *win,
            _const_spec((1, 3 * INNER)),
            _const_spec((CCONV, INNER)),
            _const_spec((1, INNER)),
            _const_spec((1, INNER)),
            _const_spec((1, INNER)),
            *wout,
            _const_spec((1, D)),
            pl.BlockSpec(memory_space=pltpu.SMEM),
            c_spec, half_spec, half_spec, head_spec,
        ],
        out_specs=[
            pl.BlockSpec((None, tm, D), row),
            pl.BlockSpec((None, CCONV - 1, INNER), lambda b, t: (b, 0, 0)),
            c_spec,
            head_spec,
        ],
        out_shape=[
            jax.ShapeDtypeStruct((B, T, D), f32),
            jax.ShapeDtypeStruct((B, CCONV - 1, INNER), f32),
            jax.ShapeDtypeStruct(C_s.shape, f32),
            jax.ShapeDtypeStruct((R, INNER), f32),
        ],
        scratch_shapes=[
            pltpu.VMEM((CCONV_HIST + tm, INNER), f32),
            pltpu.VMEM((tm, INNER), f32),
            pltpu.VMEM((SUBLANES - 1, tm + CCONV_HIST - SUBLANES, CONV_COLS), f32),
            pltpu.VMEM((SUBLANES * CCONV, INNER), f32),
            pltpu.VMEM((SUBLANES, D), f32),
            pltpu.VMEM((SUBLANES, 3 * INNER), f32),
            pltpu.VMEM((SUBLANES, INNER), f32),
            pltpu.VMEM((SUBLANES, INNER), f32),
            pltpu.VMEM((SUBLANES, D), f32),
            pltpu.VMEM((SUBLANES, 3 * D), f32),
        ],
        compiler_params=_params("arbitrary", "arbitrary"),
        name="cf_prompt",
    )(x, mod, g, *[w["cf_w_in"]] * len(win), w["cf_b_in"], w["cf_w_dw"], w["cf_b_dw"], w["cf_ln_g"],
      w["cf_ln_b"], *[w["cf_w_out"]] * len(wout), final_g, dec_flat, C_s, q_s, kw_s, v_s)


def _cf_sample_kernel(x_ref, mod_ref, g_ref, win_ref, bin_ref, wdw_ref, bdw_ref, lng_ref, lnb_ref, wout_ref,
                      fg_ref, st_ref, o_ref, stout_ref, u_ref, z_ref, y_ref, *, tb):
    i = pl.program_id(0)

    @pl.when(i == 0)
    def _():
        u, z = _cf_in(x_ref[...], mod_ref[...], g_ref, win_ref, bin_ref)
        u_ref[...] = u
        z_ref[...] = z

    rows = pl.ds(pl.multiple_of(i * tb, tb), tb)
    u = u_ref[rows, :]
    acc = bdw_ref[...] + wdw_ref[CCONV - 1:CCONV, :] * u
    for j in range(CCONV - 1):
        acc = acc + wdw_ref[j:j + 1, :] * st_ref[j]
    y_ref[rows, :] = acc
    stout_ref[0:CCONV - 2] = st_ref[1:CCONV - 1]
    stout_ref[CCONV - 2] = u

    @pl.when(i == pl.num_programs(0) - 1)
    def _():
        o_ref[:, 0, :] = _cf_out(y_ref[...], z_ref[...], x_ref[...], mod_ref[...], lng_ref, lnb_ref, wout_ref,
                                 fg_ref)


def _cf_sample(x, mod, g, w, final_g, st):
    R, D = x.shape
    tb = SAMPLE_TILE
    state = pl.BlockSpec((CCONV - 1, tb, INNER), lambda i: (0, i, 0))
    win, wout = _col_specs(D, 3 * INNER), _col_specs(INNER, D)
    return pl.pallas_call(
        functools.partial(_grouped(_cf_sample_kernel, (1, 1, 1, len(win), 1, 1, 1, 1, 1, len(wout))), tb=tb),
        grid=(R // tb,),
        in_specs=[
            _const_spec((R, D)),
            pl.BlockSpec((None, R, 3 * D), lambda i: (1, 0, 0), pipeline_mode=pl.Buffered(1)),
            _const_spec((1, D)),
            *win,
            _const_spec((1, 3 * INNER)),
            _const_spec((CCONV, INNER)),
            _const_spec((1, INNER)),
            _const_spec((1, INNER)),
            _const_spec((1, INNER)),
            *wout,
            _const_spec((1, D)),
            state,
        ],
        out_specs=[pl.BlockSpec((R, 1, D), lambda i: (0, 0, 0)), state],
        out_shape=[jax.ShapeDtypeStruct((R, 1, D), f32), jax.ShapeDtypeStruct((CCONV - 1, R, INNER), f32)],
        scratch_shapes=[pltpu.VMEM((R, INNER), f32)] * 3,
        compiler_params=_params("arbitrary"),
        name="cf_sample",
    )(x, mod, g, *[w["cf_w_in"]] * len(win), w["cf_b_in"], w["cf_w_dw"], w["cf_b_dw"], w["cf_ln_g"],
      w["cf_ln_b"], *[w["cf_w_out"]] * len(wout), final_g, st)


def kernel(x_prompt, x_sample, c_prompt, c_sample, state_mlstm_C, state_mlstm_n, state_mlstm_m, state_mlstm_conv, state_conf_conv, norm_g, w_ada, b_ada, ml_w_in, ml_w_conv, ml_b_conv, ml_w_q, ml_w_k, ml_w_v, ml_w_ig, ml_b_ig, ml_w_fg, ml_b_fg, ml_ln_g, ml_skip, ml_w_out, cf_w_in, cf_b_in, cf_w_dw, cf_b_dw, cf_ln_g, cf_ln_b, cf_w_out, final_g):
    B = x_prompt.shape[0]
    R = x_sample.shape[0]
    D = D_MODEL

    gate_pad = LANES - 2 * HEADS
    w = {
        "ml_w_in": _bf(ml_w_in[0]),
        "ml_w_conv": jnp.repeat(ml_w_conv[0], SUBLANES, axis=0),
        "ml_b_conv": ml_b_conv,
        "ml_w_q": _bf(ml_w_q[0]),
        "ml_w_k": _bf(ml_w_k[0]),
        "ml_w_v": _bf(ml_w_v[0]),
        "ml_w_igf": _bf(jnp.pad(jnp.concatenate([ml_w_ig[0], ml_w_fg[0]], axis=1), ((0, 0), (0, gate_pad)))),
        "ml_b_igf": jnp.pad(jnp.concatenate([ml_b_ig, ml_b_fg], axis=1), ((0, 0), (0, gate_pad))),        "ml_ln_g": ml_ln_g.reshape(1, INNER),
        "ml_skip": ml_skip,
        "ml_w_out": _bf(ml_w_out[0]),
        "cf_w_in": _bf(cf_w_in[0]),
        "cf_b_in": cf_b_in,
        "cf_w_dw": cf_w_dw[0],
        "cf_b_dw": cf_b_dw,
        "cf_ln_g": cf_ln_g,
        "cf_ln_b": cf_ln_b,
        "cf_w_out": _bf(cf_w_out[0]),
    }
    final_g2 = final_g.reshape(1, D)

    assert R % B == 0
    mod = _ada(jnp.concatenate([c_sample, c_prompt], axis=0), w_ada, b_ada)

    tap_major = lambda a: jnp.transpose(a, (1, 0, 2))
    xs = x_sample
    m_pad = jnp.pad(state_mlstm_m[0], ((0, 0), (0, LANES - HEADS)))
    (q_s, kw_s, v_s, xc_s, z_s, n_s, a1, a2, dec, m_s, mconv_s) = _ml_front_sample(
        xs, mod, norm_g[0:1], w, tap_major(state_mlstm_conv[0]), state_mlstm_n[0], m_pad)

    x1, mconv_p, C_p, n_p, m_p = _ml_prompt(x_prompt, mod, norm_g[0:1], w)
    y_prompt, cconv_p, C_s, qc = _cf_prompt(
        x1, mod, norm_g[1:2], w, final_g2, dec[:, :HEADS].reshape(R * HEADS), state_mlstm_C[0], q_s, kw_s, v_s)

    x1_s = _ml_back_sample(a1, a2, v_s, qc, xc_s, z_s, xs, mod, w)
    y_sample, cconv_s = _cf_sample(x1_s, mod, norm_g[1:2], w, final_g2, tap_major(state_conf_conv[0]))

    return (
        y_prompt,
        y_sample,
        C_p[None],
        C_s[None],
        n_p[None],
        n_s[None],
        m_p[:, 0, :HEADS][None],
        m_s[:, :HEADS][None],
        mconv_p[None],
        tap_major(mconv_s)[None],
        cconv_p[None],
        tap_major(cconv_s)[None],
    )
```
